```python
import jax, jax.numpy as jnp
from jax import lax
import numpy as np

D_MODEL = 4096
BATCH = 1
SEQ = 16384
DEPTH = 1
DEC_BATCH = 16
DEC_SEQ = 64
PAST_LEN = 2048

CHUNK = 64
N_META = 16
W_MIX = D_MODEL
W_A = W_MIX // 2
W_B = W_MIX - W_A
H_A = 8
H_B = 8
DH_A = W_A // H_A
DH_B = W_B // H_B
PROJ = 5 * W_A + 4 * W_B + 2 * H_A
ROPE_BASE = 10000.0
RET_DECAY_BASE = 5.0
EPS = 1e-6

kernel_name = 'hymba_mlstm_retention_stream_step'


def rmsnorm(x, g):
    xf = x.astype(jnp.float32)
    y = xf * lax.rsqrt(jnp.mean(xf * xf, axis=-1, keepdims=True) + EPS)
    return (y * g.astype(jnp.float32)).astype(x.dtype)


def head_rmsnorm(h, g, n_heads):
    B, T, _ = h.shape
    hh = h.reshape(B, T, n_heads, -1)
    hh = hh * lax.rsqrt(jnp.mean(hh * hh, axis=-1, keepdims=True) + EPS)
    return hh.reshape(B, T, -1) * g.astype(jnp.float32)


def rotary(x, pos):
    d = x.shape[-1]
    inv = ROPE_BASE ** (-jnp.arange(0, d, 2, dtype=jnp.float32) / d)
    ang = pos.astype(jnp.float32)[:, None] * inv[None, :]
    cos = jnp.cos(ang)[None, :, None, :]
    sin = jnp.sin(ang)[None, :, None, :]
    x1, x2 = x[..., : d // 2], x[..., d // 2:]
    return jnp.concatenate([x1 * cos - x2 * sin, x1 * sin + x2 * cos], axis=-1)


def project(xn, pos, w_in, b_igate, b_fgate):
    B, T, _ = xn.shape
    f32 = jnp.float32
    p = jnp.einsum('btd,dp->btp', xn, w_in).astype(f32)
    widths = [W_A] * 5 + [W_B] * 4 + [H_A]
    splits = [int(s) for s in np.cumsum(widths)]
    qa, ka, va, oa, za, qb, kb, vb, zb, ia, fa = jnp.split(p, splits, axis=-1)

    def heads(t, h):
        return jnp.transpose(t.reshape(B, T, h, -1), (0, 2, 1, 3))

    ig = jnp.transpose(ia + b_igate.astype(f32), (0, 2, 1))
    lf = jnp.transpose(jax.nn.log_sigmoid(fa + b_fgate.astype(f32)), (0, 2, 1))
    qb_r = rotary(qb.reshape(B, T, H_B, DH_B), pos)
    kb_r = rotary(kb.reshape(B, T, H_B, DH_B), pos) * (DH_B ** -0.5)
    mix = (
        heads(qa, H_A),
        heads(ka, H_A) * (DH_A ** -0.5),
        heads(va, H_A),
        ig,
        lf,
        jnp.transpose(qb_r, (0, 2, 1, 3)),
        jnp.transpose(kb_r, (0, 2, 1, 3)),
        heads(vb, H_B),
    )
    gates = (jax.nn.sigmoid(oa), jax.nn.silu(za), jax.nn.silu(zb))
    return mix, gates


def mlstm_chunk(state, q, k, v, ig, lf):
    C, n, m = state
    L = q.shape[2]
    b = jnp.cumsum(lf, axis=-1)
    causal = jnp.tril(jnp.ones((L, L), dtype=bool))
    logD = b[..., :, None] - b[..., None, :] + ig[..., None, :]
    logD = jnp.where(causal, logD, -jnp.inf)
    inter = b + m[..., None]
    m_t = jnp.maximum(inter, jnp.max(logD, axis=-1))
    dw = jnp.exp(logD - m_t[..., None])
    w_inter = jnp.exp(inter - m_t)
    s = jnp.einsum('bhtd,bhsd->bhts', q, k) * dw
    num = w_inter[..., None] * jnp.einsum('bhtd,bhde->bhte', q, C) + jnp.einsum('bhts,bhse->bhte', s, v)
    den = w_inter * jnp.einsum('bhtd,bhd->bht', q, n) + jnp.sum(s, axis=-1)
    h = num / jnp.maximum(jnp.abs(den), jnp.exp(-m_t))[..., None]
    m_new = m_t[..., -1]
    w_state = jnp.exp(b[..., -1:] - b + ig - m_new[..., None])
    decay = jnp.exp(b[..., -1] + m - m_new)
    C_new = decay[..., None, None] * C + jnp.einsum('bhs,bhsd,bhse->bhde', w_state, k, v)
    n_new = decay[..., None] * n + jnp.einsum('bhs,bhsd->bhd', w_state, k)
    return h, (C_new, n_new, m_new)


def retention_chunk(S, q, k, v):
    L = q.shape[2]
    log_gamma = jnp.log1p(-jnp.power(2.0, -RET_DECAY_BASE - jnp.arange(H_B, dtype=jnp.float32)))
    j = jnp.arange(L, dtype=jnp.float32)
    diff = j[:, None] - j[None, :]
    decay = jnp.where(diff >= 0, jnp.exp(log_gamma[:, None, None] * jnp.maximum(diff, 0.0)), 0.0)
    scores = jnp.einsum('bhtd,bhsd->bhts', q, k) * decay[None]
    inter = jnp.exp(log_gamma[:, None] * (j + 1.0))
    o = jnp.einsum('bhts,bhse->bhte', scores, v) + inter[None, :, :, None] * jnp.einsum('bhtd,bhde->bhte', q, S)
    w_state = jnp.exp(log_gamma[:, None] * (L - 1.0 - j))
    S_new = jnp.exp(log_gamma * L)[None, :, None, None] * S + jnp.einsum('hs,bhsd,bhse->bhde', w_state, k, v)
    return o, S_new


def chunk_step(state, xs):
    C, n, m, S = state
    qa, ka, va, ia, fa, qb, kb, vb = xs
    ha, (C, n, m) = mlstm_chunk((C, n, m), qa, ka, va, ia, fa)
    hb, S = retention_chunk(S, qb, kb, vb)
    return (C, n, m, S), (ha, hb)


def chunked_scan(state, mix, L):
    def to_chunks(t):
        B, H, T = t.shape[:3]
        return jnp.moveaxis(t.reshape(B, H, T // L, L, *t.shape[3:]), 2, 0)

    def from_chunks(t):
        NC, B, H, _, d = t.shape
        return jnp.moveaxis(t, 0, 2).reshape(B, H, NC * L, d)

    xs = tuple(to_chunks(t) for t in mix)
    state, (ha, hb) = lax.scan(chunk_step, state, xs)
    return state, from_chunks(ha), from_chunks(hb)


def to_tokens(h):
    B, H, T, d = h.shape
    return jnp.transpose(h, (0, 2, 1, 3)).reshape(B, T, H * d)


def mixer_output(ha, hb, gates, g_head_a, g_head_b, w_out, dtype):
    oa, za, zb = gates
    ya = head_rmsnorm(oa * to_tokens(ha), g_head_a, H_A) * za
    yb = head_rmsnorm(to_tokens(hb), g_head_b, H_B) * zb
    y = jnp.concatenate([ya, yb], axis=-1).astype(dtype)
    return jnp.einsum('btw,wd->btd', y, w_out)


def setup_inputs(seed: int = 0) -> dict:
    key = jax.random.key(seed)
    ks = jax.random.split(key, 16)
    f32 = jnp.float32
    x_prompt = jax.random.normal(ks[0], (BATCH, SEQ, D_MODEL), f32)
    x_sample = jax.random.normal(ks[1], (DEC_BATCH, DEC_SEQ, D_MODEL), f32)
    state_mlstm_C = 0.5 * jax.random.normal(ks[2], (DEPTH, DEC_BATCH, H_A, DH_A, DH_A), f32)
    state_mlstm_n = 0.5 * jax.random.normal(ks[3], (DEPTH, DEC_BATCH, H_A, DH_A), f32)
    state_mlstm_m = jax.random.normal(ks[4], (DEPTH, DEC_BATCH, H_A), f32)
    state_ret_S = 0.5 * jax.random.normal(ks[5], (DEPTH, DEC_BATCH, H_B, DH_B, DH_B), f32)
    meta_tokens = jax.random.normal(ks[6], (N_META, D_MODEL), f32)
    g_norm1 = 1.0 + 0.01 * jax.random.normal(ks[7], (DEPTH, D_MODEL), f32)
    w_in = jax.random.normal(ks[8], (DEPTH, D_MODEL, PROJ), f32) * (D_MODEL ** -0.5)
    b_igate = 0.1 * jax.random.normal(ks[9], (DEPTH, H_A), f32)
    b_fgate = jnp.linspace(3.0, 6.0, H_A, dtype=f32)[None, :] + 0.1 * jax.random.normal(ks[10], (DEPTH, H_A), f32)
    g_head_a = 1.0 + 0.01 * jax.random.normal(ks[11], (DEPTH, W_A), f32)
    g_head_b = 1.0 + 0.01 * jax.random.normal(ks[12], (DEPTH, W_B), f32)
    w_out = jax.random.normal(ks[13], (DEPTH, W_MIX, D_MODEL), f32) * (W_MIX ** -0.5)
    g_final = 1.0 + 0.01 * jax.random.normal(ks[14], (D_MODEL,), f32)
    return {'x_prompt': x_prompt, 'x_sample': x_sample,
            'state_mlstm_C': state_mlstm_C, 'state_mlstm_n': state_mlstm_n,
            'state_mlstm_m': state_mlstm_m, 'state_ret_S': state_ret_S,
            'meta_tokens': meta_tokens, 'g_norm1': g_norm1, 'w_in': w_in,
            'b_igate': b_igate, 'b_fgate': b_fgate, 'g_head_a': g_head_a,
            'g_head_b': g_head_b, 'w_out': w_out, 'g_final': g_final}


def reference(x_prompt, x_sample, state_mlstm_C, state_mlstm_n, state_mlstm_m, state_ret_S,
              meta_tokens, g_norm1, w_in, b_igate, b_fgate, g_head_a, g_head_b, w_out, g_final):
    f32 = jnp.float32
    B = x_prompt.shape[0]
    T_p = N_META + x_prompt.shape[1]
    T_s = x_sample.shape[1]
    meta = jnp.broadcast_to(meta_tokens.astype(x_prompt.dtype)[None], (B, N_META, D_MODEL))
    xp = jnp.concatenate([meta, x_prompt], axis=1)
    xs = x_sample
    pos_p = jnp.arange(T_p, dtype=f32)
    pos_s = N_META + PAST_LEN + jnp.arange(T_s, dtype=f32)
    p_states = []
    s_states = []
    for l in range(DEPTH):
        xn = rmsnorm(xp, g_norm1[l])
        mix, gates = project(xn, pos_p, w_in[l], b_igate[l], b_fgate[l])
        zero = (jnp.zeros((B, H_A, DH_A, DH_A), f32), jnp.zeros((B, H_A, DH_A), f32),
                jnp.zeros((B, H_A), f32), jnp.zeros((B, H_B, DH_B, DH_B), f32))
        st, ha_m, hb_m = chunked_scan(zero, tuple(t[:, :, :N_META] for t in mix), N_META)
        st, ha_r, hb_r = chunked_scan(st, tuple(t[:, :, N_META:] for t in mix), CHUNK)
        ha = jnp.concatenate([ha_m, ha_r], axis=2)
        hb = jnp.concatenate([hb_m, hb_r], axis=2)
        xp = xp + mixer_output(ha, hb, gates, g_head_a[l], g_head_b[l], w_out[l], xp.dtype)
        p_states.append(st)
        xn = rmsnorm(xs, g_norm1[l])
        mix, gates = project(xn, pos_s, w_in[l], b_igate[l], b_fgate[l])
        st0 = (state_mlstm_C[l].astype(f32), state_mlstm_n[l].astype(f32),
               state_mlstm_m[l].astype(f32), state_ret_S[l].astype(f32))
        st_s, ha_s, hb_s = chunked_scan(st0, mix, T_s)
        xs = xs + mixer_output(ha_s, hb_s, gates, g_head_a[l], g_head_b[l], w_out[l], xs.dtype)
        s_states.append(st_s)
    y_prompt = rmsnorm(xp, g_final)[:, N_META:]
    y_sample = rmsnorm(xs, g_final)
    dp = x_prompt.dtype
    ds = state_mlstm_C.dtype
    prompt_C = jnp.stack([s[0] for s in p_states]).astype(dp)
    prompt_n = jnp.stack([s[1] for s in p_states]).astype(dp)
    prompt_m = jnp.stack([s[2] for s in p_states]).astype(dp)
    prompt_S = jnp.stack([s[3] for s in p_states]).astype(dp)
    sample_C = jnp.stack([s[0] for s in s_states]).astype(ds)
    sample_n = jnp.stack([s[1] for s in s_states]).astype(ds)
    sample_m = jnp.stack([s[2] for s in s_states]).astype(ds)
    sample_S = jnp.stack([s[3] for s in s_states]).astype(ds)
    return (y_prompt, y_sample, prompt_C, prompt_n, prompt_m, prompt_S, sample_C, sample_n, sample_m, sample_S)
```

```python
import functools

import jax
import jax.numpy as jnp
from jax import lax
from jax.experimental import pallas as pl
from jax.experimental.pallas import tpu as pltpu

F32 = jnp.float32
BF16 = jnp.bfloat16

EPS = 1e-6
ROPE_BASE = 10000.0
RET_DECAY_BASE = 5.0
PAST_LEN = 2048

N_HEADS = 8
D_HEAD = 256
LANES = 128
MAIN_CHUNK = 256

VMEM_LIMIT = 56 * 1024 * 1024


def _params(vmem=VMEM_LIMIT):
    return pltpu.CompilerParams(vmem_limit_bytes=vmem)


def _prenorm_kernel(x_ref, g_ref, wg_ref, xn_ref, gates_ref):
    x = x_ref[...]
    ms = jnp.mean(x * x, axis=-1, keepdims=True)
    xn = (x * lax.rsqrt(ms + EPS) * g_ref[...]).astype(BF16)
    xn_ref[...] = xn
    gates_ref[...] = jnp.dot(xn, wg_ref[...], preferred_element_type=F32)


def _prenorm(x, g_row, wg, tm):
    rows, d = x.shape
    return pl.pallas_call(
        _prenorm_kernel,
        out_shape=(jax.ShapeDtypeStruct((rows, d), BF16),
                   jax.ShapeDtypeStruct((rows, 2 * LANES), F32)),
        grid=(rows // tm,),
        in_specs=[pl.BlockSpec((tm, d), lambda i: (i, 0)),
                  pl.BlockSpec((1, d), lambda i: (0, 0)),
                  pl.BlockSpec((d, 2 * LANES), lambda i: (0, 0))],
        out_specs=(pl.BlockSpec((tm, d), lambda i: (i, 0)),
                   pl.BlockSpec((tm, 2 * LANES), lambda i: (i, 0))),
        compiler_params=_params(),
        name="prenorm",
    )(x, g_row, wg)


def _proj_kernel(xn_ref, w_ref, cos_ref, sin_ref, o_ref, *, tiles_per_sec, tn):
    sec = pl.program_id(1) // tiles_per_sec
    p = jnp.dot(xn_ref[...], w_ref[...], preferred_element_type=F32)

    @pl.when((sec == 0) | (sec == 1) | (sec == 5))
    def _():
        o_ref[...] = p.astype(BF16)

    @pl.when(sec == 2)
    def _():
        o_ref[...] = jax.nn.sigmoid(p).astype(BF16)

    @pl.when((sec == 3) | (sec == 6))
    def _():
        o_ref[...] = (p * jax.nn.sigmoid(p)).astype(BF16)

    @pl.when(sec == 4)
    def _():
        c = cos_ref[...]
        s = sin_ref[...]
        half = D_HEAD // 2
        for hh in range(tn // D_HEAD):
            lo = hh * D_HEAD
            x1 = p[:, lo:lo + half]
            x2 = p[:, lo + half:lo + D_HEAD]
            o_ref[:, lo:lo + half] = (x1 * c - x2 * s).astype(BF16)
            o_ref[:, lo + half:lo + D_HEAD] = (x1 * s + x2 * c).astype(BF16)


def _proj(xn, w_bf, cos, sin, tm, tn):
    rows, d = xn.shape
    width = N_HEADS * D_HEAD
    tiles_per_sec = width // tn
    n_tiles = 7 * tiles_per_sec

    def w_map(i, j):
        sec = j // tiles_per_sec
        wsec = sec + (sec >= 1).astype(jnp.int32) + (sec >= 5).astype(jnp.int32)
        return (0, wsec * tiles_per_sec + j % tiles_per_sec)

    return pl.pallas_call(
        functools.partial(_proj_kernel, tiles_per_sec=tiles_per_sec, tn=tn),
        out_shape=jax.ShapeDtypeStruct((rows, 7 * width), BF16),
        grid=(rows // tm, n_tiles),
        in_specs=[pl.BlockSpec((tm, d), lambda i, j: (i, 0)),
                  pl.BlockSpec((d, tn), w_map),
                  pl.BlockSpec((tm, D_HEAD // 2), lambda i, j: (i, 0)),
                  pl.BlockSpec((tm, D_HEAD // 2), lambda i, j: (i, 0))],
        out_specs=pl.BlockSpec((tm, tn), lambda i, j: (i, j)),
        compiler_params=_params(),
        name="proj",
    )(xn, w_bf, cos, sin)


def _proj_t_kernel(wt_ref, xn_ref, cos_ref, sin_ref, o_ref, *, tiles_a, tn):
    j = pl.program_id(1)
    p = lax.dot_general(wt_ref[...], xn_ref[...], (((1,), (1,)), ((), ())),
                        preferred_element_type=F32)
    scale = D_HEAD ** -0.5

    @pl.when(j < tiles_a)
    def _():
        o_ref[...] = (p * scale).astype(BF16)

    @pl.when(j >= tiles_a)
    def _():
        c = cos_ref[...] * scale
        s = sin_ref[...] * scale
        half = D_HEAD // 2
        for hh in range(tn // D_HEAD):
            lo = hh * D_HEAD
            x1 = p[lo:lo + half, :]
            x2 = p[lo + half:lo + D_HEAD, :]
            o_ref[lo:lo + half, :] = (x1 * c - x2 * s).astype(BF16)
            o_ref[lo + half:lo + D_HEAD, :] = (x1 * s + x2 * c).astype(BF16)


def _proj_t(xn, wk_t, cos_t, sin_t, tm, tn):
    rows, d = xn.shape
    width = N_HEADS * D_HEAD
    return pl.pallas_call(
        functools.partial(_proj_t_kernel, tiles_a=width // tn, tn=tn),
        out_shape=jax.ShapeDtypeStruct((2 * width, rows), BF16),
        grid=(rows // tm, 2 * width // tn),
        in_specs=[pl.BlockSpec((tn, d), lambda i, j: (j, 0)),
                  pl.BlockSpec((tm, d), lambda i, j: (i, 0)),
                  pl.BlockSpec((D_HEAD // 2, tm), lambda i, j: (0, i)),
                  pl.BlockSpec((D_HEAD // 2, tm), lambda i, j: (0, i))],
        out_specs=pl.BlockSpec((tn, tm), lambda i, j: (j, i)),
        compiler_params=_params(),
        name="proj_t",
    )(wk_t, xn, cos_t, sin_t)


def _gate_prep_kernel(g_ref, bi_ref, bf_ref, m0_ref, a_ref, c_ref, wi_ref, e_ref, mo_ref, m_scr, *, L):
    @pl.when(pl.program_id(1) == 0)
    def _():
        m_scr[...] = m0_ref[0]

    g = g_ref[...]
    ig = g[:, :LANES] + bi_ref[...]
    z = g[:, LANES:] + bf_ref[...]
    lf = jnp.minimum(z, 0.0) - jnp.log1p(jnp.exp(-jnp.abs(z)))
    row = lax.broadcasted_iota(jnp.int32, (L, LANES), 0)
    b = lf
    d = 1
    while d < L:
        b = b + jnp.where(row >= d, pltpu.roll(b, d, 0), 0.0)
        d *= 2
    c = ig - b
    cm = c
    d = 1
    while d < L:
        cm = jnp.maximum(cm, jnp.where(row >= d, pltpu.roll(cm, d, 0), -jnp.inf))
        d *= 2
    m_prev = m_scr[...]
    mm = jnp.maximum(m_prev, cm)
    m_t = b + mm
    a_ref[...] = -mm
    c_ref[...] = c
    wi_ref[...] = jnp.exp(m_prev - mm)
    e_ref[...] = jnp.exp(-m_t)
    m_new = m_t[L - 1:L, :]
    m_scr[...] = m_new
    mo_ref[0] = m_new


def _gate_prep(gates, bi_row, bf_row, m0, n_streams, n_chunks, L, row_off):
    rows_out = n_streams * n_chunks * L
    tok = jax.ShapeDtypeStruct((rows_out, LANES), F32)
    tok_spec = pl.BlockSpec((L, LANES), lambda s, c: (s * n_chunks + c, 0))
    return pl.pallas_call(
        functools.partial(_gate_prep_kernel, L=L),
        out_shape=(tok, tok, tok, tok, jax.ShapeDtypeStruct((n_streams, 1, LANES), F32)),
        grid=(n_streams, n_chunks),
        in_specs=[pl.BlockSpec((L, 2 * LANES), lambda s, c: (row_off + s * n_chunks + c, 0)),
                  pl.BlockSpec((1, LANES), lambda s, c: (0, 0)),
                  pl.BlockSpec((1, LANES), lambda s, c: (0, 0)),
                  pl.BlockSpec((1, 1, LANES), lambda s, c: (s, 0, 0))],
        out_specs=(tok_spec, tok_spec, tok_spec, tok_spec,
                   pl.BlockSpec((1, 1, LANES), lambda s, c: (s, 0, 0))),
        scratch_shapes=[pltpu.VMEM((1, LANES), F32)],
        compiler_params=_params(),
        name="gate_prep",
    )(gates, bi_row, bf_row, m0)


def _lane_tile(x, n):
    if n <= LANES:
        return x[:, :n]
    return jnp.concatenate([x] * (n // LANES), axis=1)


def _rec_kernel(qa_ref, va_ref, oa_ref, za_ref, qb_ref, vb_ref, zb_ref, kat_ref, kbt_ref,
                a_ref, wi_ref, e_ref, c_ref, dec_ref, inter_ref, wsb_ref, gl_ref, ga_ref, gb_ref,
                c0_ref, n0_ref, s0_ref,
                ya_ref, yb_ref, co_ref, no_ref, so_ref,
                c_scr, n_scr, s_scr, *, L):
    ci = pl.program_id(2)

    @pl.when(ci == 0)
    def _():
        c_scr[...] = c0_ref[0, 0]
        n_scr[...] = n0_ref[0, 0]
        s_scr[...] = s0_ref[0, 0]

    ones_l = jnp.ones((L, LANES), BF16)
    row = lax.broadcasted_iota(jnp.int32, (L, L), 0)
    col = lax.broadcasted_iota(jnp.int32, (L, L), 1)
    causal = col <= row

    qa = qa_ref[...]
    kat = kat_ref[0]
    va = va_ref[...]
    a = a_ref[0, 0]
    wi = wi_ref[0, 0]
    e = e_ref[0, 0]
    c = c_ref[0, 0]
    logd = jnp.where(causal, _lane_tile(a, L) + c, -jnp.inf)
    s = jnp.dot(qa, kat, preferred_element_type=F32) * jnp.exp(logd)
    s_bf = s.astype(BF16)
    cmat = c_scr[...]
    nrep = n_scr[...]
    sv = jnp.dot(s_bf, va, preferred_element_type=F32)
    rowsum = jnp.dot(s_bf, ones_l, preferred_element_type=F32)
    qc = jnp.dot(qa, cmat.astype(BF16), preferred_element_type=F32)
    qn = jnp.dot(qa, nrep.astype(BF16), preferred_element_type=F32)
    den = wi * qn + rowsum
    r = 1.0 / jnp.maximum(jnp.abs(den), e)
    h = (_lane_tile(wi, D_HEAD) * qc + sv) * _lane_tile(r, D_HEAD)
    hh = oa_ref[...].astype(F32) * h
    ms = jnp.mean(hh * hh, axis=-1, keepdims=True)
    ya = hh * lax.rsqrt(ms + EPS) * ga_ref[0] * za_ref[...].astype(F32)
    ya_ref[...] = ya.astype(BF16)

    w_state = jnp.exp(_lane_tile(a[L - 1:L, :], L) + c)
    kw = (kat.astype(F32) * w_state).astype(BF16)
    decay = wi[L - 1:L, :]
    c_new = _lane_tile(decay, D_HEAD) * cmat + jnp.dot(kw, va, preferred_element_type=F32)
    n_new = decay * nrep + jnp.dot(kw, ones_l, preferred_element_type=F32)
    c_scr[...] = c_new
    n_scr[...] = n_new

    qb = qb_ref[...]
    kbt = kbt_ref[0]
    vb = vb_ref[...]
    smat = s_scr[...]
    sb = (jnp.dot(qb, kbt, preferred_element_type=F32) * dec_ref[0]).astype(BF16)
    o = (jnp.dot(sb, vb, preferred_element_type=F32)
         + _lane_tile(inter_ref[0], D_HEAD) * jnp.dot(qb, smat.astype(BF16), preferred_element_type=F32))
    msb = jnp.mean(o * o, axis=-1, keepdims=True)
    yb = o * lax.rsqrt(msb + EPS) * gb_ref[0] * zb_ref[...].astype(F32)
    yb_ref[...] = yb.astype(BF16)
    kwb = (kbt.astype(F32) * wsb_ref[0]).astype(BF16)
    s_new = _lane_tile(gl_ref[0], D_HEAD) * smat + jnp.dot(kwb, vb, preferred_element_type=F32)
    s_scr[...] = s_new

    @pl.when(ci == pl.num_programs(2) - 1)
    def _():
        co_ref[0, 0] = c_new
        no_ref[0, 0] = n_new
        so_ref[0, 0] = s_new


def _recurrence(p, kt, a_rep, wi_rep, e_rep, c_row, tabs, ga, gb, c0, n0, s0,
                n_streams, n_chunks, L, row_off):
    dec, inter, wsb, gl = tabs
    width = N_HEADS * D_HEAD
    y_rows = n_streams * n_chunks * L

    def p_spec(sec):
        return pl.BlockSpec((L, D_HEAD), lambda s, h, c: (row_off + s * n_chunks + c, sec * N_HEADS + h))

    def kt_spec(grp):
        return pl.BlockSpec((1, D_HEAD, L), lambda s, h, c: (s, grp * N_HEADS + h, c))

    col_spec = pl.BlockSpec((1, 1, L, LANES), lambda s, h, c: (s, h, c, 0))
    state_spec = pl.BlockSpec((1, 1, D_HEAD, D_HEAD), lambda s, h, c: (s, h, 0, 0))
    nstate_spec = pl.BlockSpec((1, 1, D_HEAD, LANES), lambda s, h, c: (s, h, 0, 0))
    in_specs = [p_spec(0), p_spec(1), p_spec(2), p_spec(3), p_spec(4), p_spec(5), p_spec(6),
                kt_spec(0), kt_spec(1),
                col_spec, col_spec, col_spec,
                pl.BlockSpec((1, 1, 1, L), lambda s, h, c: (s, h, 0, c)),
                pl.BlockSpec((1, L, L), lambda s, h, c: (h, 0, 0)),
                pl.BlockSpec((1, L, LANES), lambda s, h, c: (h, 0, 0)),
                pl.BlockSpec((1, 1, L), lambda s, h, c: (h, 0, 0)),
                pl.BlockSpec((1, 1, LANES), lambda s, h, c: (h, 0, 0)),
                pl.BlockSpec((1, 1, D_HEAD), lambda s, h, c: (h, 0, 0)),
                pl.BlockSpec((1, 1, D_HEAD), lambda s, h, c: (h, 0, 0)),
                state_spec, nstate_spec, state_spec]
    y_spec = pl.BlockSpec((L, D_HEAD), lambda s, h, c: (s * n_chunks + c, h))
    st =jax.ShapeDtypeStruct((n_streams, N_HEADS, D_HEAD, D_HEAD), F32)
    return pl.pallas_call(
        functools.partial(_rec_kernel, L=L),
        out_shape=(jax.ShapeDtypeStruct((y_rows, width), BF16),
                   jax.ShapeDtypeStruct((y_rows, width), BF16),
                   st, jax.ShapeDtypeStruct((n_streams, N_HEADS, D_HEAD, LANES), F32), st),
        grid=(n_streams, N_HEADS, n_chunks),
        in_specs=in_specs,
        out_specs=(y_spec, y_spec, state_spec, nstate_spec, state_spec),
        scratch_shapes=[pltpu.VMEM((D_HEAD, D_HEAD), F32),
                        pltpu.VMEM((D_HEAD, LANES), F32),
                        pltpu.VMEM((D_HEAD, D_HEAD), F32)],
        compiler_params=_params(),
        name="recurrence",
    )(p, p, p, p, p, p, p, kt, kt, a_rep, wi_rep, e_rep, c_row, dec, inter, wsb, gl, ga, gb, c0, n0, s0)


def _out_kernel(ya_ref, yb_ref, w_ref, x_ref, g_ref, o_ref, *, tn, n_tiles):
    j = pl.program_id(1)
    y = jnp.concatenate([ya_ref[...], yb_ref[...]], axis=1)
    z = jnp.dot(y, w_ref[...], preferred_element_type=F32) + x_ref[...]
    for jj in range(n_tiles):
        @pl.when(j == jj)
        def _(jj=jj):
            o_ref[:, jj * tn:(jj + 1) * tn] = z

    @pl.when(j == n_tiles - 1)
    def _():
        zz = o_ref[...]
        ms = jnp.mean(zz * zz, axis=-1, keepdims=True)
        o_ref[...] = zz * lax.rsqrt(ms + EPS) * g_ref[...]


def _out_proj(ya, yb, w_bf, x, g_row, tm, tn):
    rows, d = x.shape
    width = ya.shape[1]
    n_tiles = d // tn
    return pl.pallas_call(
        functools.partial(_out_kernel, tn=tn, n_tiles=n_tiles),
        out_shape=jax.ShapeDtypeStruct((rows, d), F32),
        grid=(rows // tm, n_tiles),
        in_specs=[pl.BlockSpec((tm, width), lambda i, j: (i, 0)),
                  pl.BlockSpec((tm, width), lambda i, j: (i, 0)),
                  pl.BlockSpec((2 * width, tn), lambda i, j: (0, j)),
                  pl.BlockSpec((tm, tn), lambda i, j: (i, j)),
                  pl.BlockSpec((1, d), lambda i, j: (0, 0))],
        out_specs=pl.BlockSpec((tm, d), lambda i, j: (i, 0)),
        compiler_params=_params(),
        name="out_proj",
    )(ya, yb, w_bf, x, g_row)


def _rope_tables(pos):
    inv = ROPE_BASE ** (-jnp.arange(0, D_HEAD, 2, dtype=F32) / D_HEAD)
    ang = pos.astype(F32)[:, None] * inv[None, :]
    return jnp.cos(ang), jnp.sin(ang)


def _retention_tables(L):
    log_gamma = jnp.log1p(-jnp.power(2.0, -RET_DECAY_BASE - jnp.arange(N_HEADS, dtype=F32)))
    j = jnp.arange(L, dtype=F32)
    diff = j[:, None] - j[None, :]
    dec = jnp.where(diff >= 0, jnp.exp(log_gamma[:, None, None] * jnp.maximum(diff, 0.0)), 0.0)
    inter = jnp.exp(log_gamma[:, None] * (j + 1.0))
    inter = jnp.broadcast_to(inter[:, :, None], (N_HEADS, L, LANES))
    wsb = jnp.exp(log_gamma[:, None] * (L - 1.0 - j))[:, None, :]
    gl = jnp.broadcast_to(jnp.exp(log_gamma * L)[:, None, None], (N_HEADS, 1, LANES))
    return dec, inter, wsb, gl


def _col_rep(x, n_streams, T):
    x = x[:, :N_HEADS].reshape(n_streams, T, N_HEADS).transpose(0, 2, 1)
    return jnp.broadcast_to(x[..., None], (n_streams, N_HEADS, T, LANES))


def _row(x, n_streams, T):
    x = x[:, :N_HEADS].reshape(n_streams, T, N_HEADS).transpose(0, 2, 1)
    return x[:, :, None, :]


def _pad_lanes(v):
    return jnp.zeros((1, LANES), F32).at[0, :v.shape[0]].set(v.astype(F32))


def _run_group(p, kt, gates, bi_row, bf_row, tabs, ga, gb, state, n_streams, n_chunks, L, row_off):
    c0, n0, m0, s0 = state
    T = n_chunks * L
    a, c, wi, e, m_out = _gate_prep(gates, bi_row, bf_row, m0, n_streams, n_chunks, L, row_off)
    ya, yb, c_out, n_out, s_out = _recurrence(
        p, kt, _col_rep(a, n_streams, T), _col_rep(wi, n_streams, T), _col_rep(e, n_streams, T),
        _row(c, n_streams, T), tabs, ga, gb, c0, n0, s0, n_streams, n_chunks, L, row_off)
    return ya, yb, (c_out, n_out, m_out, s_out)


def kernel(x_prompt, x_sample, state_mlstm_C, state_mlstm_n, state_mlstm_m, state_ret_S,
           meta_tokens, g_norm1, w_in, b_igate, b_fgate, g_head_a, g_head_b, w_out, g_final):
    depth = w_in.shape[0]
    assert depth == 1 and x_prompt.shape[0] == 1
    d = x_prompt.shape[-1]
    seq = x_prompt.shape[1]
    n_dec, t_dec = x_sample.shape[0], x_sample.shape[1]
    n_meta = meta_tokens.shape[0]
    width = N_HEADS * D_HEAD

    w = w_in[0]
    w_bf = w.astype(BF16)
    wk_t = jnp.concatenate([w[:, width:2 * width], w[:, 6 * width:7 * width]], axis=1).T.astype(BF16)
    wg = jnp.zeros((d, 2 * LANES), F32)
    wg = wg.at[:, :N_HEADS].set(w[:, 9 * width:9 * width + N_HEADS])
    wg = wg.at[:, LANES:LANES + N_HEADS].set(w[:, 9 * width + N_HEADS:])
    wg = wg.astype(BF16)
    w_out_bf = w_out[0].astype(BF16)
    g1 = g_norm1[0][None, :]
    bi_row = _pad_lanes(b_igate[0])
    bf_row = _pad_lanes(b_fgate[0])
    ga = g_head_a[0].reshape(N_HEADS, 1, D_HEAD)
    gb = g_head_b[0].reshape(N_HEADS, 1, D_HEAD)

    x_main = x_prompt[0]
    small_rows = n_dec * t_dec + n_meta
    small_pad = -small_rows % 256
    x_small = jnp.concatenate([x_sample.reshape(n_dec * t_dec, d), meta_tokens.astype(F32),
                               jnp.zeros((small_pad, d), F32)], axis=0)
    pos_main = n_meta + jnp.arange(seq, dtype=F32)
    pos_small = jnp.concatenate([jnp.tile(n_meta + PAST_LEN + jnp.arange(t_dec, dtype=F32), n_dec),
                                 jnp.arange(n_meta, dtype=F32), jnp.zeros((small_pad,), F32)])

    def project(x, pos, tm):
        xn, gates = _prenorm(x, g1, wg, 256)
        cos, sin = _rope_tables(pos)
        p = _proj(xn, w_bf, cos, sin, tm, 512)
        kt = _proj_t(xn, wk_t, cos.T, sin.T, tm, 512)
        return p, kt, gates

    p_main, kt_main, gates_main = project(x_main, pos_main, 1024)
    p_small, kt_small, gates_small = project(x_small, pos_small, 256)

    zeros_c = jnp.zeros((1, N_HEADS, D_HEAD, D_HEAD), F32)
    zero_state = (zeros_c, jnp.zeros((1, N_HEADS, D_HEAD, LANES), F32), jnp.zeros((1, 1, LANES), F32), zeros_c)
    n_samp = n_dec * t_dec
    kt_meta = kt_small[:, n_samp:n_samp + n_meta][None]
    _, _, st_meta = _run_group(p_small, kt_meta, gates_small, bi_row, bf_row, _retention_tables(n_meta),
                               ga, gb, zero_state, 1, 1, n_meta, n_samp // n_meta)

    ya_p, yb_p, st_p = _run_group(p_main, kt_main[None], gates_main, bi_row, bf_row,
                                  _retention_tables(MAIN_CHUNK), ga, gb, st_meta,
                                  1, seq // MAIN_CHUNK, MAIN_CHUNK, 0)

    kt_samp = kt_small[:, :n_samp].reshape(2 * width, n_dec, t_dec).transpose(1, 0, 2)
    n0 = jnp.broadcast_to(state_mlstm_n[0].astype(F32)[..., None], (n_dec, N_HEADS, D_HEAD, LANES))
    m0 = jnp.zeros((n_dec, 1, LANES), F32).at[:, 0, :N_HEADS].set(state_mlstm_m[0].astype(F32))
    st0 = (state_mlstm_C[0].astype(F32), n0, m0, state_ret_S[0].astype(F32))
    ya_s, yb_s, st_s = _run_group(p_small, kt_samp, gates_small, bi_row, bf_row, _retention_tables(t_dec),
                                  ga, gb, st0, n_dec, 1, t_dec, 0)

    gf = g_final[None, :]
    y_prompt = _out_proj(ya_p, yb_p, w_out_bf, x_main, gf, 512, 512)[None]
    y_sample = _out_proj(ya_s, yb_s, w_out_bf, x_sample.reshape(n_samp, d), gf, 512, 512)
    y_sample = y_sample.reshape(n_dec, t_dec, d)

    dp = x_prompt.dtype
    ds = state_mlstm_C.dtype

    def states(st, dt):
        c_out, n_out, m_out, s_out = st
        return (c_out[None].astype(dt), n_out[..., 0][None].astype(dt),
                m_out[:, 0, :N_HEADS][None].astype(dt), s_out[None].astype(dt))

    pc, pn, pm, ps = states(st_p, dp)
    sc, sn, sm, ss = states(st_s, ds)
    return (y_prompt, y_sample, pc, pn, pm, ps, sc, sn, sm, ss)
```

```python
import functools

import jax
import jax.numpy as jnp
from jax import lax
from jax.experimental import pallas as pl
from jax.experimental.pallas import tpu as pltpu

F32 = jnp.float32
BF16 = jnp.bfloat16

EPS = 1e-6
ROPE_BASE = 10000.0
RET_DECAY_BASE = 5.0
PAST_LEN = 2048

N_HEADS = 8
D_HEAD = 256
HALF = D_HEAD // 2
LANES = 128
MAIN_CHUNK = 256
P_SECTIONS = 7
P_HEAD = P_SECTIONS * D_HEAD

VMEM_LIMIT = 56 * 1024 * 1024


def _params(vmem=VMEM_LIMIT):
    return pltpu.CompilerParams(vmem_limit_bytes=vmem)


def _prenorm_kernel(x_ref, g_ref, wg_ref, xn_ref, gates_ref):
    x = x_ref[...]
    ms = jnp.mean(x * x, axis=-1, keepdims=True)
    xn = (x * lax.rsqrt(ms + EPS) * g_ref[...]).astype(BF16)
    xn_ref[...] = xn
    gates_ref[...] = jnp.dot(xn, wg_ref[...], preferred_element_type=F32)


def _prenorm(x, g_row, wg, tm):
    rows, d = x.shape
    return pl.pallas_call(
        _prenorm_kernel,
        out_shape=(jax.ShapeDtypeStruct((rows, d), BF16),
                   jax.ShapeDtypeStruct((rows, 2 * LANES), F32)),
        grid=(rows // tm,),
        in_specs=[pl.BlockSpec((tm, d), lambda i: (i, 0)),
                  pl.BlockSpec((1, d), lambda i: (0, 0)),
                  pl.BlockSpec((d, 2 * LANES), lambda i: (0, 0))],
        out_specs=(pl.BlockSpec((tm, d), lambda i: (i, 0)),
                   pl.BlockSpec((tm, 2 * LANES), lambda i: (i, 0))),
        compiler_params=_params(),
        name="prenorm",
    )(x, g_row, wg)


def _silu(z):
    return z * jax.nn.sigmoid(z)


def _proj_kernel(xn_ref, wqa, wva, woa, wza, wqb, wvb, wzb, wka, wkb,
                 cos_ref, sin_ref, cos_t_ref, sin_t_ref, p_ref, kt_ref, wt_scr):
    @pl.when(pl.program_id(1) == 0)
    def _():
        wt_scr[0:D_HEAD, :] = wka[...].T
        wt_scr[D_HEAD:2 * D_HEAD, :] = wkb[...].T

    xn = xn_ref[...]

    def tok(w_ref):
        return jnp.dot(xn, w_ref[...], preferred_element_type=F32)

    def put(sec, val):
        p_ref[:, sec * D_HEAD:(sec + 1) * D_HEAD] = val.astype(BF16)

    put(0, tok(wqa))
    put(1, tok(wva))
    put(2, jax.nn.sigmoid(tok(woa)))
    put(3, _silu(tok(wza)))
    q = tok(wqb)
    c = cos_ref[...]
    s = sin_ref[...]
    x1 = q[:, :HALF]
    x2 = q[:, HALF:]
    p_ref[:, 4 * D_HEAD:4 * D_HEAD + HALF] = (x1 * c - x2 * s).astype(BF16)
    p_ref[:, 4 * D_HEAD + HALF:5 * D_HEAD] = (x1 * s + x2 * c).astype(BF16)
    put(5, tok(wvb))
    put(6, _silu(tok(wzb)))

    kt = lax.dot_general(wt_scr[...], xn, (((1,), (1,)), ((), ())),
                         preferred_element_type=F32)
    scale = D_HEAD ** -0.5
    ct = cos_t_ref[...] * scale
    st = sin_t_ref[...] * scale
    kt_ref[0, 0:D_HEAD, :] = (kt[0:D_HEAD] * scale).astype(BF16)
    k1 = kt[D_HEAD:D_HEAD + HALF]
    k2 = kt[D_HEAD + HALF:2 * D_HEAD]
    kt_ref[0, D_HEAD:D_HEAD + HALF, :] = (k1 * ct - k2 * st).astype(BF16)
    kt_ref[0, D_HEAD + HALF:2 * D_HEAD, :] = (k1 * st + k2 * ct).astype(BF16)


def _proj(xn, w_bf, cos, sin, cos_t, sin_t, tm):
    rows, d = xn.shape
    def w_spec(wsec):
        return pl.BlockSpec((d, D_HEAD), lambda h, m: (0, wsec * N_HEADS + h),
                            pipeline_mode=pl.Buffered(1))

    tab = pl.BlockSpec((tm, HALF), lambda h, m: (m, 0))
    tab_t = pl.BlockSpec((HALF, tm), lambda h, m: (0, m))
    return pl.pallas_call(
        _proj_kernel,
        out_shape=(jax.ShapeDtypeStruct((rows, N_HEADS * P_HEAD), BF16),
                   jax.ShapeDtypeStruct((N_HEADS, 2 * D_HEAD, rows), BF16)),
        grid=(N_HEADS, rows // tm),
        in_specs=[pl.BlockSpec((tm, d), lambda h, m: (m, 0)),
                  w_spec(0), w_spec(2), w_spec(3), w_spec(4), w_spec(5), w_spec(7), w_spec(8),
                  w_spec(1), w_spec(6), tab, tab, tab_t, tab_t],
        out_specs=(pl.BlockSpec((tm, P_HEAD), lambda h, m: (m, h)),
                   pl.BlockSpec((1, 2 * D_HEAD, tm), lambda h, m: (h, 0, m))),
        scratch_shapes=[pltpu.VMEM((2 * D_HEAD, d), BF16)],
        compiler_params=_params(),
        name="proj",
    )(xn, w_bf, w_bf, w_bf, w_bf, w_bf, w_bf, w_bf, w_bf, w_bf, cos, sin, cos_t, sin_t)


def _gate_prep_kernel(g_ref, bi_ref, bf_ref, m0_ref, awe_ref, crow_ref, mo_ref, m_scr, *, L):
    @pl.when(pl.program_id(1) == 0)
    def _():
        m_scr[...] = m0_ref[0]

    g = g_ref[...]
    ig = g[:, :LANES] + bi_ref[...]
    z = g[:, LANES:] + bf_ref[...]
    lf = jnp.minimum(z, 0.0) - jnp.log1p(jnp.exp(-jnp.abs(z)))
    row = lax.broadcasted_iota(jnp.int32, (L, LANES), 0)
    b = lf
    d = 1
    while d < L:
        b = b + jnp.where(row >= d, pltpu.roll(b, d, 0), 0.0)
        d *= 2
    c = ig - b
    cm = c
    d = 1
    while d < L:
        cm = jnp.maximum(cm, jnp.where(row >= d, pltpu.roll(cm, d, 0), -jnp.inf))
        d *= 2
    m_prev = m_scr[...]
    mm = jnp.maximum(m_prev, cm)
    m_t = b + mm
    awe_ref[:, 0:LANES] = -mm
    awe_ref[:, LANES:2 * LANES] = jnp.exp(m_prev - mm)
    awe_ref[:, 2 * LANES:3 * LANES] = jnp.exp(-m_t)
    if L < LANES:
        c_sq = jnp.concatenate([c, jnp.zeros((LANES - L, LANES), F32)], axis=0)
    else:
        c_sq = c
    c_t = c_sq.T
    for hh in range(N_HEADS):
        crow_ref[0, hh] = c_t[hh:hh + 1, :L]
    m_new = m_t[L - 1:L, :]
    m_scr[...] = m_new
    mo_ref[0] = m_new


def _gate_prep(gates, bi_row, bf_row, m0, n_streams, n_chunks, L, row_off):
    rows_out = n_streams * n_chunks * L
    return pl.pallas_call(
        functools.partial(_gate_prep_kernel, L=L),
        out_shape=(jax.ShapeDtypeStruct((rows_out, 3 * LANES), F32),
                   jax.ShapeDtypeStruct((n_streams, N_HEADS, 1, n_chunks * L), F32),
                   jax.ShapeDtypeStruct((n_streams, 1, LANES), F32)),
        grid=(n_streams, n_chunks),
        in_specs=[pl.BlockSpec((L, 2 * LANES), lambda s, c: (row_off + s * n_chunks + c, 0)),
                  pl.BlockSpec((1, LANES), lambda s, c: (0, 0)),
                  pl.BlockSpec((1, LANES), lambda s, c: (0, 0)),
                  pl.BlockSpec((1, 1, LANES), lambda s, c: (s, 0, 0))],
        out_specs=(pl.BlockSpec((L, 3 * LANES), lambda s, c: (s * n_chunks + c, 0)),
                   pl.BlockSpec((1, N_HEADS, 1, L), lambda s, c: (s, 0, 0, c)),
                   pl.BlockSpec((1, 1, LANES), lambda s, c: (s, 0, 0))),
        scratch_shapes=[pltpu.VMEM((1, LANES), F32)],
        compiler_params=_params(),
        name="gate_prep",
    )(gates, bi_row, bf_row, m0)


def _lane_tile(x, n):
    if n <= LANES:
        return x[:, :n]
    return jnp.concatenate([x] * (n // LANES), axis=1)


def _rec_kernel(p_ref, kt_ref, awe_ref, c_ref, dec_ref, inter_ref, wsb_ref, gl_ref, ga_ref, gb_ref,
                c0_ref, n0_ref, s0_ref,
                ya_ref, yb_ref, co_ref, no_ref, so_ref,
                c_scr, n_scr, s_scr, *, L):
    head = pl.program_id(1)
    ci = pl.program_id(2)

    @pl.when(ci == 0)
    def _():
        c_scr[...] = c0_ref[0, 0]
        n_scr[...] = n0_ref[0, 0]
        s_scr[...] = s0_ref[0, 0]

    def sec(i):
        return p_ref[:, i * D_HEAD:(i + 1) * D_HEAD]

    ones_l = jnp.ones((L, LANES), BF16)
    row = lax.broadcasted_iota(jnp.int32, (L, L), 0)
    col = lax.broadcasted_iota(jnp.int32, (L, L), 1)
    causal = col <= row
    mine = lax.broadcasted_iota(jnp.int32, (L, LANES), 1) == head

    def pick(x):
        return jnp.sum(jnp.where(mine, x, 0.0), axis=1, keepdims=True)

    qa = sec(0)
    va = sec(1)
    kat = kt_ref[0, 0, 0:D_HEAD, :]
    a = pick(awe_ref[:, 0:LANES])
    wi = pick(awe_ref[:, LANES:2 * LANES])
    e = pick(awe_ref[:, 2 * LANES:3 * LANES])
    c = c_ref[0, 0]
    logd = jnp.where(causal, a + c, -jnp.inf)
    s = jnp.dot(qa, kat, preferred_element_type=F32) * jnp.exp(logd)
    s_bf = s.astype(BF16)
    cmat = c_scr[...]
    nrep = n_scr[...]
    sv = jnp.dot(s_bf, va, preferred_element_type=F32)
    rowsum = jnp.dot(s_bf, ones_l, preferred_element_type=F32)
    qc = jnp.dot(qa, cmat.astype(BF16), preferred_element_type=F32)
    qn = jnp.dot(qa, nrep.astype(BF16), preferred_element_type=F32)
    den = wi * qn + rowsum
    r = 1.0 / jnp.maximum(jnp.abs(den), e)
    h = (wi * qc + sv) * _lane_tile(r, D_HEAD)
    hh = sec(2).astype(F32) * h
    ms = jnp.mean(hh * hh, axis=-1, keepdims=True)
    ya = hh * lax.rsqrt(ms + EPS) * ga_ref[0] * sec(3).astype(F32)
    ya_ref[...] = ya.astype(BF16)

    w_state = jnp.exp(a[L - 1:L, :] + c)
    kw = (kat.astype(F32) * w_state).astype(BF16)
    decay = wi[L - 1:L, :]
    c_new = decay * cmat + jnp.dot(kw, va, preferred_element_type=F32)
    n_new = decay * nrep + jnp.dot(kw, ones_l, preferred_element_type=F32)
    c_scr[...] = c_new
    n_scr[...] = n_new

    qb = sec(4)
    vb = sec(5)
    kbt = kt_ref[0, 0, D_HEAD:2 * D_HEAD, :]
    smat = s_scr[...]
    sb = (jnp.dot(qb, kbt, preferred_element_type=F32) * dec_ref[0]).astype(BF16)
    o = (jnp.dot(sb, vb, preferred_element_type=F32)
         + _lane_tile(inter_ref[0], D_HEAD) * jnp.dot(qb, smat.astype(BF16), preferred_element_type=F32))
    msb = jnp.mean(o * o, axis=-1, keepdims=True)
    yb = o * lax.rsqrt(msb + EPS) * gb_ref[0] * sec(6).astype(F32)
    yb_ref[...] = yb.astype(BF16)
    kwb = (kbt.astype(F32) * wsb_ref[0]).astype(BF16)
    s_new = _lane_tile(gl_ref[0], D_HEAD) * smat + jnp.dot(kwb, vb, preferred_element_type=F32)
    s_scr[...] = s_new

    @pl.when(ci == pl.num_programs(2) - 1)
    def _():
        co_ref[0, 0] = c_new
        no_ref[0, 0] = n_new
        so_ref[0, 0] = s_new


def _recurrence(p, kt, awe, c_row, tabs, ga, gb, c0, n0, s0, n_streams, n_chunks, L, row_off):
    dec, inter, wsb, gl = tabs
    width = N_HEADS * D_HEAD
    y_rows = n_streams * n_chunks * L
    state_spec = pl.BlockSpec((1, 1, D_HEAD, D_HEAD), lambda s, h, c: (s, h, 0, 0))
    nstate_spec = pl.BlockSpec((1, 1, D_HEAD, LANES), lambda s, h, c: (s, h, 0, 0))
    in_specs = [pl.BlockSpec((L, P_HEAD), lambda s, h, c: (row_off + s * n_chunks + c, h)),
                pl.BlockSpec((1, 1, 2 * D_HEAD, L), lambda s, h, c: (s, h, 0, c)),
                pl.BlockSpec((L, 3 * LANES), lambda s, h, c: (s * n_chunks + c, 0)),
                pl.BlockSpec((1, 1, 1, L), lambda s, h, c: (s, h, 0, c)),
                pl.BlockSpec((1, L, L), lambda s, h, c: (h, 0, 0)),
                pl.BlockSpec((1, L, LANES), lambda s, h, c: (h, 0, 0)),
                pl.BlockSpec((1, 1, L), lambda s, h, c: (h, 0, 0)),
                pl.BlockSpec((1, 1, LANES), lambda s, h, c: (h, 0, 0)),
                pl.BlockSpec((1, 1, D_HEAD), lambda s, h, c: (h, 0, 0)),
                pl.BlockSpec((1, 1, D_HEAD), lambda s, h, c: (h, 0, 0)),
                state_spec, nstate_spec, state_spec]
    y_spec = pl.BlockSpec((L, D_HEAD), lambda s, h, c: (s * n_chunks + c, h))
    st = jax.ShapeDtypeStruct((n_streams, N_HEADS, D_HEAD, D_HEAD), F32)
    return pl.pallas_call(
        functools.partial(_rec_kernel, L=L),
        out_shape=(jax.ShapeDtypeStruct((y_rows, width), BF16),
                   jax.ShapeDtypeStruct((y_rows, width), BF16),
                   st, jax.ShapeDtypeStruct((n_streams, N_HEADS, D_HEAD, LANES), F32), st),
        grid=(n_streams, N_HEADS, n_chunks),
        in_specs=in_specs,
        out_specs=(y_spec, y_spec, state_spec, nstate_spec, state_spec),
        scratch_shapes=[pltpu.VMEM((D_HEAD, D_HEAD), F32),
                        pltpu.VMEM((D_HEAD, LANES), F32),
                        pltpu.VMEM((D_HEAD, D_HEAD), F32)],
        compiler_params=_params(),
        name="recurrence",
    )(p, kt, awe, c_row, dec, inter, wsb, gl, ga, gb, c0, n0, s0)


def _out_kernel(ya_ref, yb_ref, w1_ref, w2_ref, x_ref, g_ref, o_ref, *, n_k, tx):
    k = pl.program_id(1)
    part = (jnp.dot(ya_ref[...], w1_ref[...], preferred_element_type=F32)
            + jnp.dot(yb_ref[...], w2_ref[...], preferred_element_type=F32))

    @pl.when(k == 0)
    def _():
        o_ref[...] = part

    @pl.when(k > 0)
    def _():
        o_ref[...] += part

    for kk in range(n_k):
        @pl.when(k == kk)
        def _(kk=kk):
            o_ref[:, kk * tx:(kk + 1) * tx] += x_ref[...]

    @pl.when(k == n_k - 1)
    def _():
        z = o_ref[...]
        ms = jnp.mean(z * z, axis=-1, keepdims=True)
        o_ref[...] = z * lax.rsqrt(ms + EPS) * g_ref[...]


def _out_proj(ya, yb, w_bf, x, g_row, tm, n_k):
    rows, d = x.shape
    width = ya.shape[1]
    tk = width // n_k
    tx = d // n_k
    return pl.pallas_call(
        functools.partial(_out_kernel, n_k=n_k, tx=tx),
        out_shape=jax.ShapeDtypeStruct((rows, d), F32),
        grid=(rows // tm, n_k),
        in_specs=[pl.BlockSpec((tm, tk), lambda i, k: (i, k)),
                  pl.BlockSpec((tm, tk), lambda i, k: (i, k)),
                  pl.BlockSpec((tk, d), lambda i, k: (k, 0)),
                  pl.BlockSpec((tk, d), lambda i, k: (n_k + k, 0)),
                  pl.BlockSpec((tm, tx), lambda i, k: (i, k)),
                  pl.BlockSpec((1, d), lambda i, k: (0, 0))],
        out_specs=pl.BlockSpec((tm, d), lambda i, k: (i, 0)),
        compiler_params=_params(),
        name="out_proj",
    )(ya, yb, w_bf, w_bf, x, g_row)


def _rope_tables(pos):
    inv = ROPE_BASE ** (-jnp.arange(0, D_HEAD, 2, dtype=F32) / D_HEAD)
    ang = pos.astype(F32)[:, None] * inv[None, :]
    return jnp.cos(ang), jnp.sin(ang)


def _retention_tables(L):
    log_gamma = jnp.log1p(-jnp.power(2.0, -RET_DECAY_BASE - jnp.arange(N_HEADS, dtype=F32)))
    j = jnp.arange(L, dtype=F32)
    diff = j[:, None] - j[None, :]
    dec = jnp.where(diff >= 0, jnp.exp(log_gamma[:, None, None] * jnp.maximum(diff, 0.0)), 0.0)
    inter = jnp.exp(log_gamma[:, None] * (j + 1.0))
    inter = jnp.broadcast_to(inter[:, :, None], (N_HEADS, L, LANES))
    wsb = jnp.exp(log_gamma[:, None] * (L - 1.0 - j))[:, None, :]
    gl = jnp.broadcast_to(jnp.exp(log_gamma * L)[:, None, None], (N_HEADS, 1, LANES))
    return dec, inter, wsb, gl


def _pad_lanes(v):
    return jnp.zeros((1, LANES), F32).at[0, :v.shape[0]].set(v.astype(F32))


def _run_group(p, kt, gates, bi_row, bf_row, tabs, ga, gb, state, n_streams, n_chunks, L, row_off):
    c0, n0, m0, s0 = state
    awe, c_row, m_out = _gate_prep(gates, bi_row, bf_row, m0, n_streams, n_chunks, L, row_off)
    ya, yb, c_out, n_out, s_out = _recurrence(p, kt, awe, c_row, tabs, ga, gb, c0, n0, s0,
                                              n_streams, n_chunks, L, row_off)
    return ya, yb, (c_out, n_out, m_out, s_out)


def kernel(x_prompt, x_sample, state_mlstm_C, state_mlstm_n, state_mlstm_m, state_ret_S,
           meta_tokens, g_norm1, w_in, b_igate, b_fgate, g_head_a, g_head_b, w_out, g_final):
    depth = w_in.shape[0]
    assert depth == 1 and x_prompt.shape[0] == 1
    d = x_prompt.shape[-1]
    seq = x_prompt.shape[1]
    n_dec, t_dec = x_sample.shape[0], x_sample.shape[1]
    n_meta = meta_tokens.shape[0]
    width = N_HEADS * D_HEAD

    w = w_in[0]
    w_bf = w.astype(BF16)
    wg = jnp.zeros((d, 2 * LANES), F32)
    wg = wg.at[:, :N_HEADS].set(w[:, 9 * width:9 * width + N_HEADS])
    wg = wg.at[:, LANES:LANES + N_HEADS].set(w[:, 9 * width + N_HEADS:])
    wg = wg.astype(BF16)
    w_out_bf = w_out[0].astype(BF16)
    g1 = g_norm1[0][None, :]
    bi_row = _pad_lanes(b_igate[0])
    bf_row = _pad_lanes(b_fgate[0])
    ga = g_head_a[0].reshape(N_HEADS, 1, D_HEAD)
    gb = g_head_b[0].reshape(N_HEADS, 1, D_HEAD)

    x_main = x_prompt[0]
    n_samp = n_dec * t_dec
    small_rows = n_samp + n_meta
    small_pad = -small_rows % 256
    x_small = jnp.concatenate([x_sample.reshape(n_samp, d), meta_tokens.astype(F32),
                               jnp.zeros((small_pad, d), F32)], axis=0)
    pos_main = n_meta + jnp.arange(seq, dtype=F32)
    pos_small = jnp.concatenate([jnp.tile(n_meta + PAST_LEN + jnp.arange(t_dec, dtype=F32), n_dec),
                                 jnp.arange(n_meta, dtype=F32), jnp.zeros((small_pad,), F32)])

    def project(x, pos, tm):
        xn, gates = _prenorm(x, g1, wg, 256)
        cos, sin = _rope_tables(pos)
        p, kt = _proj(xn, w_bf, cos, sin, cos.T, sin.T, tm)
        return p, kt, gates

    p_main, kt_main, gates_main = project(x_main, pos_main, 512)
    p_small, kt_small, gates_small = project(x_small, pos_small, 256)

    zeros_c = jnp.zeros((1, N_HEADS, D_HEAD, D_HEAD), F32)
    zero_state = (zeros_c, jnp.zeros((1, N_HEADS, D_HEAD, LANES), F32), jnp.zeros((1, 1, LANES), F32), zeros_c)
    kt_meta = kt_small[:, :, n_samp:n_samp + n_meta][None]
    _, _, st_meta = _run_group(p_small, kt_meta, gates_small, bi_row, bf_row, _retention_tables(n_meta),
                               ga, gb, zero_state, 1, 1, n_meta, n_samp // n_meta)

    ya_p, yb_p, st_p = _run_group(p_main, kt_main[None], gates_main, bi_row, bf_row,
                                  _retention_tables(MAIN_CHUNK), ga, gb, st_meta,
                                  1, seq // MAIN_CHUNK, MAIN_CHUNK, 0)

    kt_samp = kt_small[:, :, :n_samp].reshape(N_HEADS, 2 * D_HEAD, n_dec, t_dec).transpose(2, 0, 1, 3)
    n0 = jnp.broadcast_to(state_mlstm_n[0].astype(F32)[..., None], (n_dec, N_HEADS, D_HEAD, LANES))
    m0 = jnp.zeros((n_dec, 1, LANES), F32).at[:, 0, :N_HEADS].set(state_mlstm_m[0].astype(F32))
    st0 = (state_mlstm_C[0].astype(F32), n0, m0, state_ret_S[0].astype(F32))
    ya_s, yb_s, st_s = _run_group(p_small, kt_samp, gates_small, bi_row, bf_row, _retention_tables(t_dec),
                                  ga, gb, st0, n_dec, 1, t_dec, 0)

    gf = g_final[None, :]
    y_prompt = _out_proj(ya_p, yb_p, w_out_bf, x_main, gf, 512, 4)[None]
    y_sample = _out_proj(ya_s, yb_s, w_out_bf, x_sample.reshape(n_samp, d), gf, 512, 4)
    y_sample = y_sample.reshape(n_dec, t_dec, d)

    dp = x_prompt.dtype
    ds = state_mlstm_C.dtype

    def states(st, dt):
        c_out, n_out, m_out, s_out = st
        return (c_out[None].astype(dt), n_out[..., 0][None].astype(dt),
                m_out[:, 0, :N_HEADS][None].astype(dt), s_out[None].astype(dt))

    pc, pn, pm, ps = states(st_p, dp)
    sc, sn, sm, ss = states(st_s, ds)
    return (y_prompt, y_sample, pc, pn, pm, ps, sc, sn, sm, ss)
```

```python
import functools

import jax
import jax.numpy as jnp
from jax import lax
from jax.experimental import pallas as pl
from jax.experimental.pallas import tpu as pltpu

F32 = jnp.float32
BF16 = jnp.bfloat16

EPS = 1e-6
LOG2_E = 1.4426950408889634
ROPE_BASE = 10000.0
RET_DECAY_BASE = 5.0
PAST_LEN = 2048

N_HEADS = 8
D_HEAD = 256
HALF = D_HEAD // 2
LANES = 128
MAIN_CHUNK = 256
HEADS_PER_STEP = 2
NORM_ROWS = 256
PROJ_ROWS = 768
P_SECTIONS = 7
P_HEAD = P_SECTIONS * D_HEAD
Y_HEAD = 2 * D_HEAD

VMEM_LIMIT = 56 * 1024 * 1024


def _params(vmem=VMEM_LIMIT):
    return pltpu.CompilerParams(vmem_limit_bytes=vmem)


def _nt_dot(a, b):
    return lax.dot_general(a, b, (((1,), (1,)), ((), ())), preferred_element_type=F32)


def _prenorm_kernel(xa_ref, xb_ref, g_ref, wg_ref, xn_ref, gates_ref, *, tiles_a):
    def body(x_ref):
        x = x_ref[...]
        ms = jnp.mean(x * x, axis=-1, keepdims=True)
        xn = (x * lax.rsqrt(ms + EPS) * g_ref[...]).astype(BF16)
        xn_ref[...] = xn
        gates_ref[...] = _nt_dot(xn, wg_ref[...])

    @pl.when(pl.program_id(0) < tiles_a)
    def _():
        body(xa_ref)

    @pl.when(pl.program_id(0) >= tiles_a)
    def _():
        body(xb_ref)


def _prenorm(xa, xb, g_row, wg_t, tm):
    d = xa.shape[1]
    tiles_a = xa.shape[0] // tm
    tiles_b = xb.shape[0] // tm
    rows = xa.shape[0] + xb.shape[0]
    return pl.pallas_call(
        functools.partial(_prenorm_kernel, tiles_a=tiles_a),
        out_shape=(jax.ShapeDtypeStruct((rows, d), BF16),
                   jax.ShapeDtypeStruct((rows, 2 * LANES), F32)),
        grid=(tiles_a + tiles_b,),
        in_specs=[pl.BlockSpec((tm, d), lambda i: (jnp.minimum(i, tiles_a - 1), 0)),
                  pl.BlockSpec((tm, d), lambda i: (jnp.maximum(i - tiles_a, 0), 0)),
                  pl.BlockSpec((1, d), lambda i: (0, 0)),
                  pl.BlockSpec((2 * LANES, d), lambda i: (0, 0))],
        out_specs=(pl.BlockSpec((tm, d), lambda i: (i, 0)),
                   pl.BlockSpec((tm, 2 * LANES), lambda i: (i, 0))),
        compiler_params=_params(),
        name="prenorm",
    )(xa, xb, g_row, wg_t)


def _silu(z):
    return z * jax.nn.sigmoid(z)


def _proj_kernel(xn_ref, wqa, wva, woa, wza, wqb, wvb, wzb, wka, wkb,
                 cos_ref, sin_ref, cos_t_ref, sin_t_ref, p_ref, kt_ref):
    xn = xn_ref[...]

    def tok(w_ref):
        return _nt_dot(xn, w_ref[...])

    def put(sec, val):
        p_ref[:, sec * D_HEAD:(sec + 1) * D_HEAD] = val.astype(BF16)

    put(0, tok(wqa))
    put(1, tok(wva))
    put(2, jax.nn.sigmoid(tok(woa)))
    put(3, _silu(tok(wza)))
    q = tok(wqb)
    c = cos_ref[...]
    s = sin_ref[...]
    x1 = q[:, :HALF]
    x2 = q[:, HALF:]
    p_ref[:, 4 * D_HEAD:4 * D_HEAD + HALF] = (x1 * c - x2 * s).astype(BF16)
    p_ref[:, 4 * D_HEAD + HALF:5 * D_HEAD] = (x1 * s + x2 * c).astype(BF16)
    put(5, tok(wvb))
    put(6, _silu(tok(wzb)))

    scale = D_HEAD ** -0.5
    kt_ref[0, 0:D_HEAD, :] = (_nt_dot(wka[...], xn) * scale).astype(BF16)
    kb = _nt_dot(wkb[...], xn)
    ct = cos_t_ref[...] * scale
    st = sin_t_ref[...] * scale
    k1 = kb[:HALF]
    k2 = kb[HALF:]
    kt_ref[0, D_HEAD:D_HEAD + HALF, :] = (k1 * ct - k2 * st).astype(BF16)
    kt_ref[0, D_HEAD + HALF:2 * D_HEAD, :] = (k1 * st + k2 * ct).astype(BF16)


def _proj(xn, wt_bf, cos, sin, cos_t, sin_t, tm):
    rows, d = xn.shape
    def w_spec(wsec):
        return pl.BlockSpec((D_HEAD, d), lambda h, m: (wsec * N_HEADS + h, 0),
                            pipeline_mode=pl.Buffered(1))

    tab = pl.BlockSpec((tm, HALF), lambda h, m: (m, 0))
    tab_t = pl.BlockSpec((HALF, tm), lambda h, m: (0, m))
    return pl.pallas_call(
        _proj_kernel,
        out_shape=(jax.ShapeDtypeStruct((rows, N_HEADS * P_HEAD), BF16),
                   jax.ShapeDtypeStruct((N_HEADS, 2 * D_HEAD, rows), BF16)),
        grid=(N_HEADS, rows // tm),
        in_specs=[pl.BlockSpec((tm, d), lambda h, m: (m, 0)),
                  w_spec(0), w_spec(2), w_spec(3), w_spec(4), w_spec(5), w_spec(7), w_spec(8),
                  w_spec(1), w_spec(6), tab, tab, tab_t, tab_t],
        out_specs=(pl.BlockSpec((tm, P_HEAD), lambda h, m: (m, h)),
                   pl.BlockSpec((1, 2 * D_HEAD, tm), lambda h, m: (h, 0, m))),
        compiler_params=_params(),
        name="proj",
    )(xn, wt_bf, wt_bf, wt_bf, wt_bf, wt_bf, wt_bf, wt_bf, wt_bf, wt_bf, cos, sin, cos_t, sin_t)


def _gate_prep_kernel(g_ref, bi_ref, bf_ref, m0_ref, awe_ref, crow_ref, mo_ref, m_scr, *, L):
    @pl.when(pl.program_id(1) == 0)
    def _():
        m_scr[...] = m0_ref[0]

    g = g_ref[...]
    ig = g[:, :LANES] + bi_ref[...]
    z = g[:, LANES:] + bf_ref[...]
    lf = jnp.minimum(z, 0.0) - jnp.log1p(jnp.exp(-jnp.abs(z)))
    row = lax.broadcasted_iota(jnp.int32, (L, LANES), 0)
    b = lf
    d = 1
    while d < L:
        b = b + jnp.where(row >= d, pltpu.roll(b, d, 0), 0.0)
        d *= 2
    c = ig - b
    cm = c
    d = 1
    while d < L:
        cm = jnp.maximum(cm, jnp.where(row >= d, pltpu.roll(cm, d, 0), -jnp.inf))
        d *= 2
    m_prev = m_scr[...]
    mm = jnp.maximum(m_prev, cm)
    m_t = b + mm
    awe_ref[:, 0:LANES] = -mm * LOG2_E
    awe_ref[:, LANES:2 * LANES] = jnp.exp(m_prev - mm)
    awe_ref[:, 2 * LANES:3 * LANES] = jnp.exp(-m_t)
    c2 = c * LOG2_E
    if L < LANES:
        c_sq = jnp.concatenate([c2, jnp.zeros((LANES - L, LANES), F32)], axis=0)
    else:
        c_sq = c2
    c_t = c_sq.T
    for hh in range(N_HEADS):
        crow_ref[0, hh] = c_t[hh:hh + 1, :L]
    m_new = m_t[L - 1:L, :]
    m_scr[...] = m_new
    mo_ref[0] = m_new


def _gate_prep(gates, bi_row, bf_row, m0, n_streams, n_chunks, L, row_off):
    rows_out = n_streams * n_chunks * L
    return pl.pallas_call(
        functools.partial(_gate_prep_kernel, L=L),
        out_shape=(jax.ShapeDtypeStruct((rows_out, 3 * LANES), F32),
                   jax.ShapeDtypeStruct((n_streams, N_HEADS, 1, n_chunks * L), F32),
                   jax.ShapeDtypeStruct((n_streams, 1, LANES), F32)),
        grid=(n_streams, n_chunks),
        in_specs=[pl.BlockSpec((L, 2 * LANES), lambda s, c: (row_off + s * n_chunks + c, 0)),
                  pl.BlockSpec((1, LANES), lambda s, c: (0, 0)),
                  pl.BlockSpec((1, LANES), lambda s, c: (0, 0)),
                  pl.BlockSpec((1, 1, LANES), lambda s, c: (s, 0, 0))],
        out_specs=(pl.BlockSpec((L, 3 * LANES), lambda s, c: (s * n_chunks + c, 0)),
                   pl.BlockSpec((1, N_HEADS, 1, L), lambda s, c: (s, 0, 0, c)),
                   pl.BlockSpec((1, 1, LANES), lambda s, c: (s, 0, 0))),
        scratch_shapes=[pltpu.VMEM((1, LANES), F32)],
        compiler_params=_params(),
        name="gate_prep",
    )(gates, bi_row, bf_row, m0)


def _lane_tile(x, n):
    if n <= LANES:
        return x[:, :n]
    return jnp.concatenate([x] * (n // LANES), axis=1)


def _head_norm(h, g_row):
    ms = jnp.mean(h * h, axis=-1, keepdims=True)
    return h * lax.rsqrt(ms + EPS) * g_row


def _rec_kernel(p_ref, kt_ref, awe_ref, c_ref, dec_ref, inter_ref, wsb_ref, gl_ref, ga_ref, gb_ref,
                c0_ref, n0_ref, s0_ref,
                y_ref, co_ref, no_ref, so_ref,
                c_scr, n_scr, s_scr, *, L):
    ci = pl.program_id(2)

    @pl.when(ci == 0)
    def _():
        c_scr[...] = c0_ref[0]
        n_scr[...] = n0_ref[0]
        s_scr[...] = s0_ref[0]

    ones_l = jnp.ones((L, LANES), BF16)
    row = lax.broadcasted_iota(jnp.int32, (L, L), 0)
    col = lax.broadcasted_iota(jnp.int32, (L, L), 1)
    causal = col <= row
    lane = lax.broadcasted_iota(jnp.int32, (L, LANES), 1)
    last = ci == pl.num_programs(2) - 1

    for j in range(HEADS_PER_STEP):
        head = pl.program_id(1) * HEADS_PER_STEP + j
        mine = lane == head

        def sec(i, j=j):
            lo = j * P_HEAD + i * D_HEAD
            return p_ref[:, lo:lo + D_HEAD]

        def pick(x, mine=mine):
            return jnp.sum(jnp.where(mine, x, 0.0), axis=1, keepdims=True)

        qa = sec(0)
        va = sec(1)
        kat = kt_ref[0, j, 0:D_HEAD, :]
        a = pick(awe_ref[:, 0:LANES])
        wi = pick(awe_ref[:, LANES:2 * LANES])
        e = pick(awe_ref[:, 2 * LANES:3 * LANES])
        c = c_ref[0, j]
        logd = jnp.where(causal, a + c, -jnp.inf)
        s = jnp.dot(qa, kat, preferred_element_type=F32) * jnp.exp2(logd)
        s_bf = s.astype(BF16)
        cmat = c_scr[j]
        nrep = n_scr[j]
        sv = jnp.dot(s_bf, va, preferred_element_type=F32)
        rowsum = jnp.dot(s_bf, ones_l, preferred_element_type=F32)
        qc = jnp.dot(qa, cmat.astype(BF16), preferred_element_type=F32)
        qn = jnp.dot(qa, nrep.astype(BF16), preferred_element_type=F32)
        den = wi * qn + rowsum
        r = 1.0 / jnp.maximum(jnp.abs(den), e)
        h = (wi * qc + sv) * _lane_tile(r, D_HEAD)
        ya = _head_norm(sec(2).astype(F32) * h, ga_ref[j]) * sec(3).astype(F32)
        y_ref[:, j * Y_HEAD:j * Y_HEAD + D_HEAD] = ya.astype(BF16)

        w_state = jnp.exp2(a[L - 1:L, :] + c)
        kw = (kat.astype(F32) * w_state).astype(BF16)
        decay = wi[L - 1:L, :]
        c_new = decay * cmat + jnp.dot(kw, va, preferred_element_type=F32)
        n_new = decay * nrep + jnp.dot(kw, ones_l, preferred_element_type=F32)
        c_scr[j] = c_new
        n_scr[j] = n_new

        qb = sec(4)
        vb = sec(5)
        kbt = kt_ref[0, j, D_HEAD:2 * D_HEAD, :]
        smat = s_scr[j]
        sb = (jnp.dot(qb, kbt, preferred_element_type=F32) * dec_ref[j]).astype(BF16)
        o = (jnp.dot(sb, vb, preferred_element_type=F32)
             + _lane_tile(inter_ref[j], D_HEAD) * jnp.dot(qb, smat.astype(BF16), preferred_element_type=F32))
        yb = _head_norm(o, gb_ref[j]) * sec(6).astype(F32)
        y_ref[:, j * Y_HEAD + D_HEAD:(j + 1) * Y_HEAD] = yb.astype(BF16)
        kwb = (kbt.astype(F32) * wsb_ref[j]).astype(BF16)
        s_new = _lane_tile(gl_ref[j], D_HEAD) * smat + jnp.dot(kwb, vb, preferred_element_type=F32)
        s_scr[j] = s_new

        @pl.when(last)
        def _(j=j, c_new=c_new, n_new=n_new, s_new=s_new):
            co_ref[0, j] = c_new
            no_ref[0, j] = n_new
            so_ref[0, j] = s_new


def _recurrence(p, kt, awe, c_row, tabs, ga, gb, c0, n0, s0, n_streams, n_chunks, L, row_off):
    dec, inter, wsb, gl = tabs
    hb = HEADS_PER_STEP
    y_rows = n_streams * n_chunks * L

    def per_head(*tail):
        return pl.BlockSpec((hb,) + tail, lambda s, h, c: (h,) + (0,) * len(tail))

    state_spec = pl.BlockSpec((1, hb, D_HEAD, D_HEAD), lambda s, h, c: (s, h, 0, 0))
    nstate_spec = pl.BlockSpec((1, hb, D_HEAD, LANES), lambda s, h, c: (s, h, 0, 0))
    in_specs = [pl.BlockSpec((L, hb * P_HEAD), lambda s, h, c: (row_off + s * n_chunks + c, h)),
                pl.BlockSpec((1, hb, 2 * D_HEAD, L), lambda s, h, c: (s, h, 0, c)),
                pl.BlockSpec((L, 3 * LANES), lambda s, h, c: (s * n_chunks + c, 0)),
                pl.BlockSpec((1, hb, 1, L), lambda s, h, c: (s, h, 0, c)),
                per_head(L, L), per_head(L, LANES), per_head(1, L), per_head(1, LANES),
                per_head(1, D_HEAD), per_head(1, D_HEAD),
                state_spec, nstate_spec, state_spec]
    st = jax.ShapeDtypeStruct((n_streams, N_HEADS, D_HEAD, D_HEAD), F32)
    return pl.pallas_call(
        functools.partial(_rec_kernel, L=L),
        out_shape=(jax.ShapeDtypeStruct((y_rows, N_HEADS * Y_HEAD), BF16),
                   st, jax.ShapeDtypeStruct((n_streams, N_HEADS, D_HEAD, LANES), F32), st),
        grid=(n_streams, N_HEADS // hb, n_chunks),
        in_specs=in_specs,
        out_specs=(pl.BlockSpec((L, hb * Y_HEAD), lambda s, h, c: (s * n_chunks + c, h)),
                   state_spec, nstate_spec, state_spec),
        scratch_shapes=[pltpu.VMEM((hb, D_HEAD, D_HEAD), F32),
                        pltpu.VMEM((hb, D_HEAD, LANES), F32),
                        pltpu.VMEM((hb, D_HEAD, D_HEAD), F32)],
        compiler_params=_params(),
        name="recurrence",
    )(p, kt, awe, c_row, dec, inter, wsb, gl, ga, gb, c0, n0, s0)


def _out_kernel(y_ref, w_ref, x_ref, g_ref, o_ref, *, n_k, tx):
    k = pl.program_id(1)

    @pl.when(k == 0)
    def _():
        o_ref[...] = jnp.dot(y_ref[...], w_ref[...], preferred_element_type=F32)

    @pl.when(k > 0)
    def _():
        o_ref[...] += jnp.dot(y_ref[...], w_ref[...], preferred_element_type=F32)

    for kk in range(n_k):
        @pl.when(k == kk)
        def _(kk=kk):
            o_ref[:, kk * tx:(kk + 1) * tx] += x_ref[...]

    @pl.when(k == n_k - 1)
    def _():
        z = o_ref[...]
        ms = jnp.mean(z * z, axis=-1, keepdims=True)
        o_ref[...] = z * lax.rsqrt(ms + EPS) * g_ref[...]


def _out_proj(y, w_bf, x, g_row, tm, n_k):
    rows, d = x.shape
    tk = y.shape[1] // n_k
    tx = d // n_k
    return pl.pallas_call(
        functools.partial(_out_kernel, n_k=n_k, tx=tx),
        out_shape=jax.ShapeDtypeStruct((rows, d), F32),
        grid=(rows // tm, n_k),
        in_specs=[pl.BlockSpec((tm, tk), lambda i, k: (i, k)),
                  pl.BlockSpec((tk, d), lambda i, k: (k, 0)),
                  pl.BlockSpec((tm, tx), lambda i, k: (i, k)),
                  pl.BlockSpec((1, d), lambda i, k: (0, 0))],
        out_specs=pl.BlockSpec((tm, d), lambda i, k: (i, 0)),
        compiler_params=_params(),
        name="out_proj",
    )(y, w_bf, x, g_row)


def _rope_tables(base, off):
    inv = ROPE_BASE ** (-jnp.arange(0, D_HEAD, 2, dtype=F32) / D_HEAD)
    ab = base.astype(F32)[:, None] * inv[None, :]
    ao = off.astype(F32)[:, None] * inv[None, :]
    cb, sb = jnp.cos(ab)[:, None, :], jnp.sin(ab)[:, None, :]
    co, so = jnp.cos(ao)[None, :, :], jnp.sin(ao)[None, :, :]
    n = base.shape[0] * off.shape[0]
    return (cb * co - sb * so).reshape(n, HALF), (sb * co + cb * so).reshape(n, HALF)


def _retention_tables(L):
    log_gamma = jnp.log1p(-jnp.power(2.0, -RET_DECAY_BASE - jnp.arange(N_HEADS, dtype=F32)))
    j = jnp.arange(L, dtype=F32)
    diff = j[:, None] - j[None, :]
    dec = jnp.where(diff >= 0, jnp.exp(log_gamma[:, None, None] * jnp.maximum(diff, 0.0)), 0.0)
    inter = jnp.exp(log_gamma[:, None] * (j + 1.0))
    inter = jnp.broadcast_to(inter[:, :, None], (N_HEADS, L, LANES))
    wsb = jnp.exp(log_gamma[:, None] * (L - 1.0 - j))[:, None, :]
    gl = jnp.broadcast_to(jnp.exp(log_gamma * L)[:, None, None], (N_HEADS, 1, LANES))
    return dec, inter, wsb, gl


def _pad_lanes(v):
    return jnp.zeros((1, LANES), F32).at[0, :v.shape[0]].set(v.astype(F32))


def _run_group(p, kt, gates, bi_row, bf_row, tabs, ga, gb, state, n_streams, n_chunks, L, row_off):
    c0, n0, m0, s0 = state
    awe, c_row, m_out = _gate_prep(gates, bi_row, bf_row, m0, n_streams, n_chunks, L, row_off)
    y, c_out, n_out, s_out = _recurrence(p, kt, awe, c_row, tabs, ga, gb, c0, n0, s0,
                                         n_streams, n_chunks, L, row_off)
    return y, (c_out, n_out, m_out, s_out)


def kernel(x_prompt, x_sample, state_mlstm_C, state_mlstm_n, state_mlstm_m, state_ret_S,
           meta_tokens, g_norm1, w_in, b_igate, b_fgate, g_head_a, g_head_b, w_out, g_final):
    depth = w_in.shape[0]
    assert depth == 1 and x_prompt.shape[0] == 1
    d = x_prompt.shape[-1]
    seq = x_prompt.shape[1]
    n_dec, t_dec = x_sample.shape[0], x_sample.shape[1]
    n_meta = meta_tokens.shape[0]
    width = N_HEADS * D_HEAD
    assert seq % MAIN_CHUNK == 0 and MAIN_CHUNK % t_dec == 0 and t_dec % n_meta == 0

    wt = w_in[0].T
    wt_bf = wt.astype(BF16)
    wg_t = jnp.zeros((2 * LANES, d), F32)
    wg_t = wg_t.at[:N_HEADS].set(wt[9 * width:9 * width + N_HEADS])
    wg_t = wg_t.at[LANES:LANES + N_HEADS].set(wt[9 * width + N_HEADS:])
    wg_t = wg_t.astype(BF16)
    w_out_bf = (w_out[0].reshape(2, N_HEADS, D_HEAD, d).transpose(1, 0, 2, 3)
                .reshape(2 * width, d).astype(BF16))
    g1 = g_norm1[0][None, :]
    bi_row = _pad_lanes(b_igate[0])
    bf_row = _pad_lanes(b_fgate[0])
    ga = g_head_a[0].reshape(N_HEADS, 1, D_HEAD)
    gb = g_head_b[0].reshape(N_HEADS, 1, D_HEAD)

    x_main = x_prompt[0]
    n_samp = n_dec * t_dec
    meta_at = seq + n_samp
    small_pad = -(meta_at + n_meta) % PROJ_ROWS
    x_small = jnp.concatenate([x_sample.reshape(n_samp, d), meta_tokens.astype(F32),
                               jnp.zeros((small_pad, d), F32)], axis=0)
    n_rows = seq + x_small.shape[0]
    assert seq % NORM_ROWS == 0 and x_small.shape[0] % NORM_ROWS == 0
    blk = jnp.arange(n_rows // t_dec)
    base = jnp.where(blk < seq // t_dec, n_meta + t_dec * blk,
                     jnp.where(blk < meta_at // t_dec, n_meta + PAST_LEN, 0))
    cos, sin = _rope_tables(base, jnp.arange(t_dec))

    xn, gates = _prenorm(x_main, x_small, g1, wg_t, NORM_ROWS)
    p, kt = _proj(xn, wt_bf, cos, sin, cos.T, sin.T, PROJ_ROWS)

    zeros_c = jnp.zeros((1, N_HEADS, D_HEAD, D_HEAD), F32)
    zero_state = (zeros_c, jnp.zeros((1, N_HEADS, D_HEAD, LANES), F32), jnp.zeros((1, 1, LANES), F32), zeros_c)
    kt_meta = kt[:, :, meta_at:meta_at + n_meta][None]
    _, st_meta = _run_group(p, kt_meta, gates, bi_row, bf_row, _retention_tables(n_meta),
                            ga, gb, zero_state, 1, 1, n_meta, meta_at // n_meta)

    y_p, st_p = _run_group(p, kt[None], gates, bi_row, bf_row,
                           _retention_tables(MAIN_CHUNK), ga, gb, st_meta,
                           1, seq // MAIN_CHUNK, MAIN_CHUNK, 0)

    kt_samp = kt[:, :, seq:meta_at].reshape(N_HEADS, 2 * D_HEAD, n_dec, t_dec).transpose(2, 0, 1, 3)
    n0 = jnp.broadcast_to(state_mlstm_n[0].astype(F32)[..., None], (n_dec, N_HEADS, D_HEAD, LANES))
    m0 = jnp.zeros((n_dec, 1, LANES), F32).at[:, 0, :N_HEADS].set(state_mlstm_m[0].astype(F32))
    st0 = (state_mlstm_C[0].astype(F32), n0, m0, state_ret_S[0].astype(F32))
    y_s, st_s = _run_group(p, kt_samp, gates, bi_row, bf_row, _retention_tables(t_dec),
                           ga, gb, st0, n_dec, 1, t_dec, seq // t_dec)

    gf = g_final[None, :]
    y_prompt = _out_proj(y_p, w_out_bf, x_main, gf, 512, 4)[None]
    y_sample = _out_proj(y_s, w_out_bf, x_sample.reshape(n_samp, d), gf, 512, 4)
    y_sample = y_sample.reshape(n_dec, t_dec, d)

    dp = x_prompt.dtype
    ds = state_mlstm_C.dtype

    def states(st, dt):
        c_out, n_out, m_out, s_out = st
        return (c_out[None].astype(dt), n_out[..., 0][None].astype(dt),
                m_out[:, 0, :N_HEADS][None].astype(dt), s_out[None].astype(dt))

    pc, pn, pm, ps = states(st_p, dp)
    sc, sn, sm, ss = states(st_s, ds)
    return (y_prompt, y_sample, pc, pn, pm, ps, sc, sn, sm, ss)
```

```python
import functools

import jax
import jax.numpy as jnp
from jax import lax
from jax.experimental import pallas as pl
from jax.experimental.pallas import tpu as pltpu

F32 = jnp.float32
BF16 = jnp.bfloat16

EPS = 1e-6
LOG2_E = 1.4426950408889634
ROPE_BASE = 10000.0
RET_DECAY_BASE = 5.0
PAST_LEN = 2048

N_HEADS = 8
D_HEAD = 256
HALF = D_HEAD // 2
LANES = 128
MAIN_CHUNK = 256
HEADS_PER_STEP = 2
MIXER_SUB = 2
OUT_COL_SPLIT = 4
NORM_ROWS = 256
PROJ_ROWS = 768
P_SECTIONS = 7
P_HEAD = P_SECTIONS * D_HEAD
Y_HEAD = 2 * D_HEAD

VMEM_LIMIT = 56 * 1024 * 1024


def _params(vmem=VMEM_LIMIT):
    return pltpu.CompilerParams(vmem_limit_bytes=vmem)


def _nt_dot(a, b):
    return lax.dot_general(a, b, (((1,), (1,)), ((), ())), preferred_element_type=F32)


def _prenorm_kernel(xa_ref, xb_ref, g_ref, wg_ref, xn_ref, gates_ref, *, tiles_a):
    def body(x_ref):
        x = x_ref[...]
        ms = jnp.mean(x * x, axis=-1, keepdims=True)
        xn = (x * lax.rsqrt(ms + EPS) * g_ref[...]).astype(BF16)
        xn_ref[...] = xn
        gates_ref[...] = _nt_dot(xn, wg_ref[...])

    @pl.when(pl.program_id(0) < tiles_a)
    def _():
        body(xa_ref)

    @pl.when(pl.program_id(0) >= tiles_a)
    def _():
        body(xb_ref)


def _prenorm(xa, xb, g_row, wg_t, tm):
    d = xa.shape[1]
    tiles_a = xa.shape[0] // tm
    tiles_b = xb.shape[0] // tm
    rows = xa.shape[0] + xb.shape[0]
    return pl.pallas_call(
        functools.partial(_prenorm_kernel, tiles_a=tiles_a),
        out_shape=(jax.ShapeDtypeStruct((rows, d), BF16),
                   jax.ShapeDtypeStruct((rows, 2 * LANES), F32)),
        grid=(tiles_a + tiles_b,),
        in_specs=[pl.BlockSpec((tm, d), lambda i: (jnp.minimum(i, tiles_a - 1), 0)),
                  pl.BlockSpec((tm, d), lambda i: (jnp.maximum(i - tiles_a, 0), 0)),
                  pl.BlockSpec((1, d), lambda i: (0, 0)),
                  pl.BlockSpec((2 * LANES, d), lambda i: (0, 0))],
        out_specs=(pl.BlockSpec((tm, d), lambda i: (i, 0)),
                   pl.BlockSpec((tm, 2 * LANES), lambda i: (i, 0))),
        compiler_params=_params(),
        name="prenorm",
    )(xa, xb, g_row, wg_t)


def _silu(z):
    return z * jax.nn.sigmoid(z)


def _proj_kernel(xn_ref, wqa, wva, woa, wza, wqb, wvb, wzb, wka, wkb,
                 cos_ref, sin_ref, cos_t_ref, sin_t_ref, p_ref, kt_ref):
    xn = xn_ref[...]

    def tok(w_ref):
        return _nt_dot(xn, w_ref[...])

    def put(sec, val):
        p_ref[:, sec * D_HEAD:(sec + 1) * D_HEAD] = val.astype(BF16)

    put(0, tok(wqa))
    put(1, tok(wva))
    put(2, jax.nn.sigmoid(tok(woa)))
    put(3, _silu(tok(wza)))
    q = tok(wqb)
    c = cos_ref[...]
    s = sin_ref[...]
    x1 = q[:, :HALF]
    x2 = q[:, HALF:]
    p_ref[:, 4 * D_HEAD:4 * D_HEAD + HALF] = (x1 * c - x2 * s).astype(BF16)
    p_ref[:, 4 * D_HEAD + HALF:5 * D_HEAD] = (x1 * s + x2 * c).astype(BF16)
    put(5, tok(wvb))
    put(6, _silu(tok(wzb)))

    scale = D_HEAD ** -0.5
    kt_ref[0, 0:D_HEAD, :] = (_nt_dot(wka[...], xn) * scale).astype(BF16)
    kb = _nt_dot(wkb[...], xn)
    ct = cos_t_ref[...] * scale
    st = sin_t_ref[...] * scale
    k1 = kb[:HALF]
    k2 = kb[HALF:]
    kt_ref[0, D_HEAD:D_HEAD + HALF, :] = (k1 * ct - k2 * st).astype(BF16)
    kt_ref[0, D_HEAD + HALF:2 * D_HEAD, :] = (k1 * st + k2 * ct).astype(BF16)


def _proj(xn, wt_bf, cos, sin, cos_t, sin_t, tm):
    rows, d = xn.shape
    def w_spec(wsec):
        return pl.BlockSpec((D_HEAD, d), lambda h, m: (wsec * N_HEADS + h, 0),
                            pipeline_mode=pl.Buffered(1))

    tab = pl.BlockSpec((tm, HALF), lambda h, m: (m, 0))
    tab_t = pl.BlockSpec((HALF, tm), lambda h, m: (0, m))
    return pl.pallas_call(
        _proj_kernel,
        out_shape=(jax.ShapeDtypeStruct((rows, N_HEADS * P_HEAD), BF16),
                   jax.ShapeDtypeStruct((N_HEADS, 2 * D_HEAD, rows), BF16)),
        grid=(N_HEADS, rows // tm),
        in_specs=[pl.BlockSpec((tm, d), lambda h, m: (m, 0)),
                  w_spec(0), w_spec(2), w_spec(3), w_spec(4), w_spec(5), w_spec(7), w_spec(8),
                  w_spec(1), w_spec(6), tab, tab, tab_t, tab_t],
        out_specs=(pl.BlockSpec((tm, P_HEAD), lambda h, m: (m, h)),
                   pl.BlockSpec((1, 2 * D_HEAD, tm), lambda h, m: (h, 0, m))),
        compiler_params=_params(),
        name="proj",
    )(xn, wt_bf, wt_bf, wt_bf, wt_bf, wt_bf, wt_bf, wt_bf, wt_bf, wt_bf, cos, sin, cos_t, sin_t)


def _gate_prep_kernel(g_ref, bi_ref, bf_ref, m0_ref, awe_ref, crow_ref, mo_ref, m_scr, *, L):
    @pl.when(pl.program_id(1) == 0)
    def _():
        m_scr[...] = m0_ref[0]

    g = g_ref[...]
    ig = g[:, :LANES] + bi_ref[...]
    z = g[:, LANES:] + bf_ref[...]
    lf = jnp.minimum(z, 0.0) - jnp.log1p(jnp.exp(-jnp.abs(z)))
    row = lax.broadcasted_iota(jnp.int32, (L, LANES), 0)
    b = lf
    d = 1
    while d < L:
        b = b + jnp.where(row >= d, pltpu.roll(b, d, 0), 0.0)
        d *= 2
    c = ig - b
    cm = c
    d = 1
    while d < L:
        cm = jnp.maximum(cm, jnp.where(row >= d, pltpu.roll(cm, d, 0), -jnp.inf))
        d *= 2
    m_prev = m_scr[...]
    mm = jnp.maximum(m_prev, cm)
    m_t = b + mm
    awe_ref[:, 0:LANES] = -mm * LOG2_E
    awe_ref[:, LANES:2 * LANES] = jnp.exp(m_prev - mm)
    awe_ref[:, 2 * LANES:3 * LANES] = jnp.exp(-m_t)
    c2 = c * LOG2_E
    if L < LANES:
        c_sq = jnp.concatenate([c2, jnp.zeros((LANES - L, LANES), F32)], axis=0)
    else:
        c_sq = c2
    c_t = c_sq.T
    for hh in range(N_HEADS):
        crow_ref[0, hh] = c_t[hh:hh + 1, :L]
    m_new = m_t[L - 1:L, :]
    m_scr[...] = m_new
    mo_ref[0] = m_new


def _gate_prep(gates, bi_row, bf_row, m0, n_streams, n_chunks, L, row_off):
    rows_out = n_streams * n_chunks * L
    return pl.pallas_call(
        functools.partial(_gate_prep_kernel, L=L),
        out_shape=(jax.ShapeDtypeStruct((rows_out, 3 * LANES), F32),
                   jax.ShapeDtypeStruct((n_streams, N_HEADS, 1, n_chunks * L), F32),
                   jax.ShapeDtypeStruct((n_streams, 1, LANES), F32)),
        grid=(n_streams, n_chunks),
        in_specs=[pl.BlockSpec((L, 2 * LANES), lambda s, c: (row_off + s * n_chunks + c, 0)),
                  pl.BlockSpec((1, LANES), lambda s, c: (0, 0)),
                  pl.BlockSpec((1, LANES), lambda s, c: (0, 0)),
                  pl.BlockSpec((1, 1, LANES), lambda s, c: (s, 0, 0))],
        out_specs=(pl.BlockSpec((L, 3 * LANES), lambda s, c: (s * n_chunks + c, 0)),
                   pl.BlockSpec((1, N_HEADS, 1, L), lambda s, c: (s, 0, 0, c)),
                   pl.BlockSpec((1, 1, LANES), lambda s, c: (s, 0, 0))),
        scratch_shapes=[pltpu.VMEM((1, LANES), F32)],
        compiler_params=_params(),
        name="gate_prep",
    )(gates, bi_row, bf_row, m0)


def _lane_tile(x, n):
    if n <= LANES:
        return x[:, :n]
    return jnp.concatenate([x] * (n // LANES), axis=1)


def _head_norm(h, g_row):
    ms = jnp.mean(h * h, axis=-1, keepdims=True)
    return h * lax.rsqrt(ms + EPS) * g_row


def _rec_kernel(p_ref, kt_ref, awe_ref, c_ref, dec_ref, inter_ref, wsb_ref, gl_ref, ga_ref, gb_ref,
                c0_ref, n0_ref, s0_ref,
                y_ref, co_ref, no_ref, so_ref,
                c_scr, n_scr, s_scr, *, L):
    ci = pl.program_id(2)

    @pl.when(ci == 0)
    def _():
        c_scr[...] = c0_ref[0]
        n_scr[...] = n0_ref[0]
        s_scr[...] = s0_ref[0]

    ones_l = jnp.ones((L, LANES), BF16)
    row = lax.broadcasted_iota(jnp.int32, (L, L), 0)
    col = lax.broadcasted_iota(jnp.int32, (L, L), 1)
    causal = col <= row
    lane = lax.broadcasted_iota(jnp.int32, (L, LANES), 1)
    last = ci == pl.num_programs(2) - 1

    hb = range(HEADS_PER_STEP)

    def sec(j, i):
        lo = j * P_HEAD + i * D_HEAD
        return p_ref[:, lo:lo + D_HEAD]

    def pick(x, j):
        mine = lane == pl.program_id(1) * HEADS_PER_STEP + j
        return jnp.sum(jnp.where(mine, x, 0.0), axis=1, keepdims=True)

    def mm(x, y):
        return jnp.dot(x, y, preferred_element_type=F32)

    qa = [sec(j, 0) for j in hb]
    va = [sec(j, 1) for j in hb]
    qb = [sec(j, 4) for j in hb]
    vb = [sec(j, 5) for j in hb]
    kat = [kt_ref[0, j, 0:D_HEAD, :] for j in hb]
    kbt = [kt_ref[0, j, D_HEAD:2 * D_HEAD, :] for j in hb]
    cmat = [c_scr[j] for j in hb]
    nrep = [n_scr[j] for j in hb]
    smat = [s_scr[j] for j in hb]

    s_raw = [mm(qa[j], kat[j]) for j in hb]
    sb_raw = [mm(qb[j], kbt[j]) for j in hb]
    qc = [mm(qa[j], cmat[j].astype(BF16)) for j in hb]
    qn = [mm(qa[j], nrep[j].astype(BF16)) for j in hb]
    qs = [mm(qb[j], smat[j].astype(BF16)) for j in hb]

    a = [pick(awe_ref[:, 0:LANES], j) for j in hb]
    wi = [pick(awe_ref[:, LANES:2 * LANES], j) for j in hb]
    e = [pick(awe_ref[:, 2 * LANES:3 * LANES], j) for j in hb]
    c = [c_ref[0, j] for j in hb]
    s_bf = [(s_raw[j] * jnp.exp2(jnp.where(causal, a[j] + c[j], -jnp.inf))).astype(BF16) for j in hb]
    sb_bf = [(sb_raw[j] * dec_ref[j]).astype(BF16) for j in hb]

    w_state = [jnp.exp2(a[j][L - 1:L, :] + c[j]) for j in hb]
    kw = [kat[j] * w_state[j].astype(BF16) for j in hb]
    kwb = [kbt[j] * wsb_ref[j].astype(BF16) for j in hb]

    sv = [mm(s_bf[j], va[j]) for j in hb]
    rowsum = [mm(s_bf[j], ones_l) for j in hb]
    ob = [mm(sb_bf[j], vb[j]) for j in hb]
    c_upd = [mm(kw[j], va[j]) for j in hb]
    n_upd = [mm(kw[j], ones_l) for j in hb]
    s_upd = [mm(kwb[j], vb[j]) for j in hb]

    for j in hb:
        den = wi[j] * qn[j] + rowsum[j]
        r = 1.0 / jnp.maximum(jnp.abs(den), e[j])
        h = (wi[j] * qc[j] + sv[j]) * _lane_tile(r, D_HEAD)
        ya = _head_norm(sec(j, 2).astype(F32) * h, ga_ref[j]) * sec(j, 3).astype(F32)
        y_ref[:, j * Y_HEAD:j * Y_HEAD + D_HEAD] = ya.astype(BF16)
        o = ob[j] + _lane_tile(inter_ref[j], D_HEAD) * qs[j]
        yb = _head_norm(o, gb_ref[j]) * sec(j, 6).astype(F32)
        y_ref[:, j * Y_HEAD + D_HEAD:(j + 1) * Y_HEAD] = yb.astype(BF16)

    for j in hb:
        decay = wi[j][L - 1:L, :]
        c_new = decay * cmat[j] + c_upd[j]
        n_new = decay * nrep[j] + n_upd[j]
        s_new = _lane_tile(gl_ref[j], D_HEAD) * smat[j] + s_upd[j]
        c_scr[j] = c_new
        n_scr[j] = n_new
        s_scr[j] = s_new

        @pl.when(last)
        def _(j=j, c_new=c_new, n_new=n_new, s_new=s_new):
            co_ref[0, j] = c_new
            no_ref[0, j] = n_new
            so_ref[0, j] = s_new


def _recurrence(p, kt, awe, c_row, tabs, ga, gb, c0, n0, s0, n_streams, n_chunks, L, row_off):
    dec, inter, wsb, gl = tabs
    hb = HEADS_PER_STEP
    y_rows = n_streams * n_chunks * L

    def per_head(*tail):
        return pl.BlockSpec((hb,) + tail, lambda s, h, c: (h,) + (0,) * len(tail))

    state_spec = pl.BlockSpec((1, hb, D_HEAD, D_HEAD), lambda s, h, c: (s, h, 0, 0))
    nstate_spec = pl.BlockSpec((1, hb, D_HEAD, LANES), lambda s, h, c: (s, h, 0, 0))
    in_specs = [pl.BlockSpec((L, hb * P_HEAD), lambda s, h, c: (row_off + s * n_chunks + c, h)),
                pl.BlockSpec((1, hb, 2 * D_HEAD, L), lambda s, h, c: (s, h, 0, c)),
                pl.BlockSpec((L, 3 * LANES), lambda s, h, c: (s * n_chunks + c, 0)),
                pl.BlockSpec((1, hb, 1, L), lambda s, h, c: (s, h, 0, c)),
                per_head(L, L), per_head(L, LANES), per_head(1, L), per_head(1, LANES),
                per_head(1, D_HEAD), per_head(1, D_HEAD),
                state_spec, nstate_spec, state_spec]
    st = jax.ShapeDtypeStruct((n_streams, N_HEADS, D_HEAD, D_HEAD), F32)
    return pl.pallas_call(
        functools.partial(_rec_kernel, L=L),
        out_shape=(jax.ShapeDtypeStruct((y_rows, N_HEADS * Y_HEAD), BF16),
                   st, jax.ShapeDtypeStruct((n_streams, N_HEADS, D_HEAD, LANES), F32), st),
        grid=(n_streams, N_HEADS // hb, n_chunks),
        in_specs=in_specs,
        out_specs=(pl.BlockSpec((L, hb * Y_HEAD), lambda s, h, c: (s * n_chunks + c, h)),
                   state_spec, nstate_spec, state_spec),
        scratch_shapes=[pltpu.VMEM((hb, D_HEAD, D_HEAD), F32),
                        pltpu.VMEM((hb, D_HEAD, LANES), F32),
                        pltpu.VMEM((hb, D_HEAD, D_HEAD), F32)],
        compiler_params=_params(),
        name="recurrence",
    )(p, kt, awe, c_row, dec, inter, wsb, gl, ga, gb, c0, n0, s0)


def _mixer_kernel(p_ref, kt_ref, awe_ref, c_ref, dec_ref, inter_ref, wsb_ref, gl_ref, ga_ref, gb_ref,
                  c0_ref, n0_ref, s0_ref, w_ref, x_ref, gf_ref,
                  o_ref, co_ref, no_ref, so_ref,
                  c_scr, n_scr, s_scr, y_scr, *, L):
    tile = pl.program_id(0)
    step = pl.program_id(1)
    head = jnp.minimum(step, N_HEADS - 1)
    d = o_ref.shape[1]

    def mm(x, y):
        return jnp.dot(x, y, preferred_element_type=F32)

    def load_initial_state():
        @pl.when(tile == 0)
        def _():
            c_scr[head] = c0_ref[0, 0]
            n_scr[head] = n0_ref[0, 0]
            s_scr[head] = s0_ref[0, 0]

    def store_final_state(cmat, nrep, smat):
        @pl.when(tile == pl.num_programs(0) - 1)
        def _():
            co_ref[0, 0] = cmat
            no_ref[0, 0] = nrep
            so_ref[0, 0] = smat

    def recurrence_step():
        ones_l = jnp.ones((L, LANES), BF16)
        row = lax.broadcasted_iota(jnp.int32, (L, L), 0)
        col = lax.broadcasted_iota(jnp.int32, (L, L), 1)
        causal = col <= row
        mine = lax.broadcasted_iota(jnp.int32, (L, LANES), 1) == head

        def pick(x):
            return jnp.sum(jnp.where(mine, x, 0.0), axis=1, keepdims=True)

        subs = range(MIXER_SUB)
        rows = [slice(u * L, (u + 1) * L) for u in subs]

        def sec(u, i):
            return p_ref[rows[u], i * D_HEAD:(i + 1) * D_HEAD]

        dec = dec_ref[head]
        inter = _lane_tile(inter_ref[head], D_HEAD)
        wsb = wsb_ref[head].astype(BF16)
        gl = _lane_tile(gl_ref[head], D_HEAD)
        c_all = c_ref[0, head]
        y_half = y_scr.at[step % 2]

        qa = [sec(u, 0) for u in subs]
        va = [sec(u, 1) for u in subs]
        qb = [sec(u, 4) for u in subs]
        vb = [sec(u, 5) for u in subs]
        kat = [kt_ref[0, 0, 0:D_HEAD, rows[u]] for u in subs]
        kbt = [kt_ref[0, 0, D_HEAD:2 * D_HEAD, rows[u]] for u in subs]
        a = [pick(awe_ref[rows[u], 0:LANES]) for u in subs]
        wi = [pick(awe_ref[rows[u], LANES:2 * LANES]) for u in subs]
        e = [pick(awe_ref[rows[u], 2 * LANES:3 * LANES]) for u in subs]
        c = [c_all[:, rows[u]] for u in subs]
        s_bf = [(mm(qa[u], kat[u]) * jnp.exp2(jnp.where(causal, a[u] + c[u], -jnp.inf))).astype(BF16)
                for u in subs]
        sb_bf = [(mm(qb[u], kbt[u]) * dec).astype(BF16) for u in subs]
        kw = [kat[u] * jnp.exp2(a[u][L - 1:L, :] + c[u]).astype(BF16) for u in subs]
        kwb = [kbt[u] * wsb for u in subs]
        sv = [mm(s_bf[u], va[u]) for u in subs]
        rowsum = [mm(s_bf[u], ones_l) for u in subs]
        ob = [mm(sb_bf[u], vb[u]) for u in subs]
        c_upd = [mm(kw[u], va[u]) for u in subs]
        n_upd = [mm(kw[u], ones_l) for u in subs]
        s_upd = [mm(kwb[u], vb[u]) for u in subs]

        cmat = c_scr[head]
        nrep = n_scr[head]
        smat = s_scr[head]
        for u in subs:
            qc = mm(qa[u], cmat.astype(BF16))
            qn = mm(qa[u], nrep.astype(BF16))
            qs = mm(qb[u], smat.astype(BF16))
            den = wi[u] * qn + rowsum[u]
            r = 1.0 / jnp.maximum(jnp.abs(den), e[u])
            h = (wi[u] * qc + sv[u]) * _lane_tile(r, D_HEAD)
            ya = _head_norm(sec(u, 2).astype(F32) * h, ga_ref[head]) * sec(u, 3).astype(F32)
            y_half[rows[u], 0:D_HEAD] = ya.astype(BF16)
            yb = _head_norm(ob[u] + inter * qs, gb_ref[head]) * sec(u, 6).astype(F32)
            y_half[rows[u], D_HEAD:Y_HEAD] = yb.astype(BF16)
            decay = wi[u][L - 1:L, :]
            cmat = decay * cmat + c_upd[u]
            nrep = decay * nrep + n_upd[u]
            smat = gl * smat + s_upd[u]
        c_scr[head] = cmat
        n_scr[head] = nrep
        s_scr[head] = smat
        return cmat, nrep, smat

    def accumulate():
        y = y_scr[(step + 1) % 2]
        tq = d // OUT_COL_SPLIT
        for q in range(OUT_COL_SPLIT):
            o_ref[:, q * tq:(q + 1) * tq] += mm(y, w_ref[:, q * tq:(q + 1) * tq])

    def add_residual():
        tx = d // N_HEADS
        for hh in range(N_HEADS):
            @pl.when(step == hh + 1)
            def _(hh=hh):
                o_ref[:, hh * tx:(hh + 1) * tx] += x_ref[...]

    @pl.when(step == 0)
    def _():
        load_initial_state()
        o_ref[...] = jnp.zeros(o_ref.shape, F32)
        store_final_state(*recurrence_step())

    @pl.when((step > 0) & (step < N_HEADS))
    def _():
        load_initial_state()
        state = recurrence_step()
        accumulate()
        store_final_state(*state)
        add_residual()

    @pl.when(step == N_HEADS)
    def _():
        accumulate()
        add_residual()
        z = o_ref[...]
        ms = jnp.mean(z * z, axis=-1, keepdims=True)
        o_ref[...] = z * lax.rsqrt(ms + EPS) * gf_ref[...]


def _mixer(p, kt, awe, c_row, tabs, ga, gb, c0, n0, s0, w_bf, x, gf_row, L):
    dec, inter, wsb, gl = tabs
    seq, d = x.shape
    tile_rows = MIXER_SUB * L
    n_tiles = seq // tile_rows

    def resident(arr):
        nd = arr.ndim
        return pl.BlockSpec(arr.shape, lambda t, s: (0,) * nd, pipeline_mode=pl.Buffered(1))

    def rec_head(s):
        return jnp.minimum(s, N_HEADS - 1)

    def acc_head(s):
        return jnp.maximum(s - 1, 0)

    def first_tile(*tail):
        return pl.BlockSpec((1, 1) + tail,
                            lambda t, s: (0, jnp.where(t == 0, rec_head(s), N_HEADS - 1), 0, 0))

    def last_tile(*tail):
        return pl.BlockSpec((1, 1) + tail,
                            lambda t, s: (0, jnp.where(t == n_tiles - 1, rec_head(s), 0), 0, 0))

    in_specs = [pl.BlockSpec((tile_rows, P_HEAD), lambda t, s: (t, rec_head(s))),
                pl.BlockSpec((1, 1, 2 * D_HEAD, tile_rows), lambda t, s: (0, rec_head(s), 0, t)),
                pl.BlockSpec((tile_rows, 3 * LANES), lambda t, s: (t, 0)),
                pl.BlockSpec((1, N_HEADS, 1, tile_rows), lambda t, s: (0, 0, 0, t)),
                resident(dec), resident(inter), resident(wsb), resident(gl), resident(ga), resident(gb),
                first_tile(D_HEAD, D_HEAD), first_tile(D_HEAD, LANES), first_tile(D_HEAD, D_HEAD),
                pl.BlockSpec((Y_HEAD, d), lambda t, s: (acc_head(s), 0)),
                pl.BlockSpec((tile_rows, d // N_HEADS), lambda t, s: (t, acc_head(s))),
                resident(gf_row)]
    st = jax.ShapeDtypeStruct((1, N_HEADS, D_HEAD, D_HEAD), F32)
    return pl.pallas_call(
        functools.partial(_mixer_kernel, L=L),
        out_shape=(jax.ShapeDtypeStruct((seq, d), F32),
                   st, jax.ShapeDtypeStruct((1, N_HEADS, D_HEAD, LANES), F32), st),
        grid=(n_tiles, N_HEADS + 1),
        in_specs=in_specs,
        out_specs=(pl.BlockSpec((tile_rows, d), lambda t, s: (t, 0)),
                   last_tile(D_HEAD, D_HEAD), last_tile(D_HEAD, LANES), last_tile(D_HEAD, D_HEAD)),
        scratch_shapes=[pltpu.VMEM((N_HEADS, D_HEAD, D_HEAD), F32),
                        pltpu.VMEM((N_HEADS, D_HEAD, LANES), F32),
                        pltpu.VMEM((N_HEADS, D_HEAD, D_HEAD), F32),
                        pltpu.VMEM((2, tile_rows, Y_HEAD), BF16)],
        compiler_params=_params(),
        name="mixer",
    )(p, kt, awe, c_row, dec, inter, wsb, gl, ga, gb, c0, n0, s0, w_bf, x, gf_row)


def _out_kernel(y_ref, w_ref, x_ref, g_ref, o_ref, *, n_k, tx):
    k = pl.program_id(1)

    @pl.when(k == 0)
    def _():
        o_ref[...] = jnp.dot(y_ref[...], w_ref[...], preferred_element_type=F32)

    @pl.when(k > 0)
    def _():
        o_ref[...] += jnp.dot(y_ref[...], w_ref[...], preferred_element_type=F32)

    for kk in range(n_k):
        @pl.when(k == kk)
        def _(kk=kk):
            o_ref[:, kk * tx:(kk + 1) * tx] += x_ref[...]

    @pl.when(k == n_k - 1)
    def _():
        z = o_ref[...]
        ms = jnp.mean(z * z, axis=-1, keepdims=True)
        o_ref[...] = z * lax.rsqrt(ms + EPS) * g_ref[...]


def _out_proj(y, w_bf, x, g_row, tm, n_k):
    rows, d = x.shape
    tk = y.shape[1] // n_k
    tx = d // n_k
    return pl.pallas_call(
        functools.partial(_out_kernel, n_k=n_k, tx=tx),
        out_shape=jax.ShapeDtypeStruct((rows, d), F32),
        grid=(rows // tm, n_k),
        in_specs=[pl.BlockSpec((tm, tk), lambda i, k: (i, k)),
                  pl.BlockSpec((tk, d), lambda i, k: (k, 0)),
                  pl.BlockSpec((tm, tx), lambda i, k: (i, k)),
                  pl.BlockSpec((1, d), lambda i, k: (0, 0))],
        out_specs=pl.BlockSpec((tm, d), lambda i, k: (i, 0)),
        compiler_params=_params(),
        name="out_proj",
    )(y, w_bf, x, g_row)


def _rope_tables(base, off):
    inv = ROPE_BASE ** (-jnp.arange(0, D_HEAD, 2, dtype=F32) / D_HEAD)
    ab = base.astype(F32)[:, None] * inv[None, :]
    ao = off.astype(F32)[:, None] * inv[None, :]
    cb, sb = jnp.cos(ab)[:, None, :], jnp.sin(ab)[:, None, :]
    co, so = jnp.cos(ao)[None, :, :], jnp.sin(ao)[None, :, :]
    n = base.shape[0] * off.shape[0]
    return (cb * co - sb * so).reshape(n, HALF), (sb * co + cb * so).reshape(n, HALF)


def _retention_tables(L):
    log_gamma = jnp.log1p(-jnp.power(2.0, -RET_DECAY_BASE - jnp.arange(N_HEADS, dtype=F32)))
    j = jnp.arange(L, dtype=F32)
    diff = j[:, None] - j[None, :]
    dec = jnp.where(diff >= 0, jnp.exp(log_gamma[:, None, None] * jnp.maximum(diff, 0.0)), 0.0)
    inter = jnp.exp(log_gamma[:, None] * (j + 1.0))
    inter = jnp.broadcast_to(inter[:, :, None], (N_HEADS, L, LANES))
    wsb = jnp.exp(log_gamma[:, None] * (L - 1.0 - j))[:, None, :]
    gl = jnp.broadcast_to(jnp.exp(log_gamma * L)[:, None, None], (N_HEADS, 1, LANES))
    return dec, inter, wsb, gl


def _pad_lanes(v):
    return jnp.zeros((1, LANES), F32).at[0, :v.shape[0]].set(v.astype(F32))


def _run_group(p, kt, gates, bi_row, bf_row, tabs, ga, gb, state, n_streams, n_chunks, L, row_off):
    c0, n0, m0, s0 = state
    awe, c_row, m_out = _gate_prep(gates, bi_row, bf_row, m0, n_streams, n_chunks, L, row_off)
    y, c_out, n_out, s_out = _recurrence(p, kt, awe, c_row, tabs, ga, gb, c0, n0, s0,
                                         n_streams, n_chunks, L, row_off)
    return y, (c_out, n_out, m_out, s_out)


def kernel(x_prompt, x_sample, state_mlstm_C, state_mlstm_n, state_mlstm_m, state_ret_S,
           meta_tokens, g_norm1, w_in, b_igate, b_fgate, g_head_a, g_head_b, w_out, g_final):
    depth = w_in.shape[0]
    assert depth == 1 and x_prompt.shape[0] == 1
    d = x_prompt.shape[-1]
    seq = x_prompt.shape[1]
    n_dec, t_dec = x_sample.shape[0], x_sample.shape[1]
    n_meta = meta_tokens.shape[0]
    width = N_HEADS * D_HEAD
    assert seq % MAIN_CHUNK == 0 and MAIN_CHUNK % t_dec == 0 and t_dec % n_meta == 0

    wt = w_in[0].T
    wt_bf = wt.astype(BF16)
    wg_t = jnp.zeros((2 * LANES, d), F32)
    wg_t = wg_t.at[:N_HEADS].set(wt[9 * width:9 * width + N_HEADS])
    wg_t = wg_t.at[LANES:LANES + N_HEADS].set(wt[9 * width + N_HEADS:])
    wg_t = wg_t.astype(BF16)
    w_out_bf = (w_out[0].reshape(2, N_HEADS, D_HEAD, d).transpose(1, 0, 2, 3)
                .reshape(2 * width, d).astype(BF16))
    g1 = g_norm1[0][None, :]
    bi_row = _pad_lanes(b_igate[0])
    bf_row = _pad_lanes(b_fgate[0])
    ga = g_head_a[0].reshape(N_HEADS, 1, D_HEAD)
    gb = g_head_b[0].reshape(N_HEADS, 1, D_HEAD)

    x_main = x_prompt[0]
    n_samp = n_dec * t_dec
    meta_at = seq + n_samp
    small_pad = -(meta_at + n_meta) % PROJ_ROWS
    x_small = jnp.concatenate([x_sample.reshape(n_samp, d), meta_tokens.astype(F32),
                               jnp.zeros((small_pad, d), F32)], axis=0)
    n_rows = seq + x_small.shape[0]
    assert seq % NORM_ROWS == 0 and x_small.shape[0] % NORM_ROWS == 0
    blk = jnp.arange(n_rows // t_dec)
    base = jnp.where(blk < seq // t_dec, n_meta + t_dec * blk,
                     jnp.where(blk < meta_at // t_dec, n_meta + PAST_LEN, 0))
    cos, sin = _rope_tables(base, jnp.arange(t_dec))

    xn, gates = _prenorm(x_main, x_small, g1, wg_t, NORM_ROWS)
    p, kt = _proj(xn, wt_bf, cos, sin, cos.T, sin.T, PROJ_ROWS)

    zeros_c = jnp.zeros((1, N_HEADS, D_HEAD, D_HEAD), F32)
    zero_state = (zeros_c, jnp.zeros((1, N_HEADS, D_HEAD, LANES), F32), jnp.zeros((1, 1, LANES), F32), zeros_c)
    kt_meta = kt[:, :, meta_at:meta_at + n_meta][None]
    _, st_meta = _run_group(p, kt_meta, gates, bi_row, bf_row, _retention_tables(n_meta),
                            ga, gb, zero_state, 1, 1, n_meta, meta_at // n_meta)

    gf = g_final[None, :]
    c_meta, n_meta_st, m_meta, s_meta = st_meta
    awe_p, c_row_p, m_p = _gate_prep(gates, bi_row, bf_row, m_meta, 1, seq // MAIN_CHUNK, MAIN_CHUNK, 0)
    y_prompt, c_p, n_p, s_p = _mixer(p, kt[None], awe_p, c_row_p, _retention_tables(MAIN_CHUNK), ga, gb,
                                     c_meta, n_meta_st, s_meta, w_out_bf, x_main, gf, MAIN_CHUNK)
    st_p = (c_p, n_p, m_p, s_p)

    kt_samp = kt[:, :, seq:meta_at].reshape(N_HEADS, 2 * D_HEAD, n_dec, t_dec).transpose(2, 0, 1, 3)
    n0 = jnp.broadcast_to(state_mlstm_n[0].astype(F32)[..., None], (n_dec, N_HEADS, D_HEAD, LANES))
    m0 = jnp.zeros((n_dec, 1, LANES), F32).at[:, 0, :N_HEADS].set(state_mlstm_m[0].astype(F32))
    st0 = (state_mlstm_C[0].astype(F32), n0, m0, state_ret_S[0].astype(F32))
    y_s, st_s = _run_group(p, kt_samp, gates, bi_row, bf_row, _retention_tables(t_dec),
                           ga, gb, st0, n_dec, 1, t_dec, seq // t_dec)

    y_prompt = y_prompt[None]
    y_sample = _out_proj(y_s, w_out_bf, x_sample.reshape(n_samp, d), gf, 512, 4)
    y_sample = y_sample.reshape(n_dec, t_dec, d)

    dp = x_prompt.dtype
    ds = state_mlstm_C.dtype

    def states(st, dt):
        c_out, n_out, m_out, s_out = st
        return (c_out[None].astype(dt), n_out[..., 0][None].astype(dt),
                m_out[:, 0, :N_HEADS][None].astype(dt), s_out[None].astype(dt))

    pc, pn, pm, ps = states(st_p, dp)
    sc, sn, sm, ss = states(st_s, ds)
    return (y_prompt, y_sample, pc, pn, pm, ps, sc, sn, sm, ss)
```

```python
import functools

import jax
import jax.numpy as jnp
from jax import lax
from jax.experimental import pallas as pl
from jax.experimental.pallas import tpu as pltpu

F32 = jnp.float32
BF16 = jnp.bfloat16

EPS = 1e-6
LOG2_E = 1.4426950408889634
ROPE_BASE = 10000.0
RET_DECAY_BASE = 5.0
PAST_LEN = 2048

N_HEADS = 8
D_HEAD = 256
HALF = D_HEAD // 2
LANES = 128
MAIN_CHUNK = 256
HEADS_PER_STEP = 4
PREP_CHUNKS = 4
MIXER_SUB = 2
OUT_COL_SPLIT = 4
NORM_ROWS = 256
PROJ_ROWS = 768
P_SECTIONS = 7
P_HEAD = P_SECTIONS * D_HEAD
Y_HEAD = 2 * D_HEAD

VMEM_LIMIT = 56 * 1024 * 1024


def _params(vmem=VMEM_LIMIT):
    return pltpu.CompilerParams(vmem_limit_bytes=vmem)


def _nt_dot(a, b):
    return lax.dot_general(a, b, (((1,), (1,)), ((), ())), preferred_element_type=F32)


def _prenorm_kernel(xa_ref, xb_ref, g_ref, wg_ref, xn_ref, gates_ref, *, tiles_a):
    def body(x_ref):
        x = x_ref[...]
        ms = jnp.mean(x * x, axis=-1, keepdims=True)
        xn = (x * lax.rsqrt(ms + EPS) * g_ref[...]).astype(BF16)
        xn_ref[...] = xn
        gates_ref[...] = _nt_dot(xn, wg_ref[...])

    @pl.when(pl.program_id(0) < tiles_a)
    def _():
        body(xa_ref)

    @pl.when(pl.program_id(0) >= tiles_a)
    def _():
        body(xb_ref)


def _prenorm(xa, xb, g_row, wg_t, tm):
    d = xa.shape[1]
    tiles_a = xa.shape[0] // tm
    tiles_b = xb.shape[0] // tm
    rows = xa.shape[0] + xb.shape[0]
    return pl.pallas_call(
        functools.partial(_prenorm_kernel, tiles_a=tiles_a),
        out_shape=(jax.ShapeDtypeStruct((rows, d), BF16),
                   jax.ShapeDtypeStruct((rows, 2 * LANES), F32)),
        grid=(tiles_a + tiles_b,),
        in_specs=[pl.BlockSpec((tm, d), lambda i: (jnp.minimum(i, tiles_a - 1), 0)),
                  pl.BlockSpec((tm, d), lambda i: (jnp.maximum(i - tiles_a, 0), 0)),
                  pl.BlockSpec((1, d), lambda i: (0, 0)),
                  pl.BlockSpec((2 * LANES, d), lambda i: (0, 0))],
        out_specs=(pl.BlockSpec((tm, d), lambda i: (i, 0)),
                   pl.BlockSpec((tm, 2 * LANES), lambda i: (i, 0))),
        compiler_params=_params(),
        name="prenorm",
    )(xa, xb, g_row, wg_t)


def _silu(z):
    return z * jax.nn.sigmoid(z)


def _proj_kernel(xn_ref, wqa, wva, woa, wza, wqb, wvb, wzb, wka, wkb,
                 cos_ref, sin_ref, cos_t_ref, sin_t_ref, p_ref, kt_ref):
    xn = xn_ref[...]

    def tok(w_ref):
        return _nt_dot(xn, w_ref[...])

    def put(sec, val):
        p_ref[:, sec * D_HEAD:(sec + 1) * D_HEAD] = val.astype(BF16)

    put(0, tok(wqa))
    put(1, tok(wva))
    put(2, jax.nn.sigmoid(tok(woa)))
    put(3, _silu(tok(wza)))
    q = tok(wqb)
    c = cos_ref[...]
    s = sin_ref[...]
    x1 = q[:, :HALF]
    x2 = q[:, HALF:]
    p_ref[:, 4 * D_HEAD:4 * D_HEAD + HALF] = (x1 * c - x2 * s).astype(BF16)
    p_ref[:, 4 * D_HEAD + HALF:5 * D_HEAD] = (x1 * s + x2 * c).astype(BF16)
    put(5, tok(wvb))
    put(6, _silu(tok(wzb)))

    scale = D_HEAD ** -0.5
    kt_ref[0, 0:D_HEAD, :] = (_nt_dot(wka[...], xn) * scale).astype(BF16)
    kb = _nt_dot(wkb[...], xn)
    ct = cos_t_ref[...] * scale
    st = sin_t_ref[...] * scale
    k1 = kb[:HALF]
    k2 = kb[HALF:]
    kt_ref[0, D_HEAD:D_HEAD + HALF, :] = (k1 * ct - k2 * st).astype(BF16)
    kt_ref[0, D_HEAD + HALF:2 * D_HEAD, :] = (k1 * st + k2 * ct).astype(BF16)


def _proj(xn, wt_bf, cos, sin, cos_t, sin_t, tm):
    rows, d = xn.shape
    def w_spec(wsec):
        return pl.BlockSpec((D_HEAD, d), lambda h, m: (wsec * N_HEADS + h, 0),
                            pipeline_mode=pl.Buffered(1))

    tab = pl.BlockSpec((tm, HALF), lambda h, m: (m, 0))
    tab_t = pl.BlockSpec((HALF, tm), lambda h, m: (0, m))
    return pl.pallas_call(
        _proj_kernel,
        out_shape=(jax.ShapeDtypeStruct((rows, N_HEADS * P_HEAD), BF16),
                   jax.ShapeDtypeStruct((N_HEADS, 2 * D_HEAD, rows), BF16)),
        grid=(N_HEADS, rows // tm),
        in_specs=[pl.BlockSpec((tm, d), lambda h, m: (m, 0)),
                  w_spec(0), w_spec(2), w_spec(3), w_spec(4), w_spec(5), w_spec(7), w_spec(8),
                  w_spec(1), w_spec(6), tab, tab, tab_t, tab_t],
        out_specs=(pl.BlockSpec((tm, P_HEAD), lambda h, m: (m, h)),
                   pl.BlockSpec((1, 2 * D_HEAD, tm), lambda h, m: (h, 0, m))),
        compiler_params=_params(),
        name="proj",
    )(xn, wt_bf, wt_bf, wt_bf, wt_bf, wt_bf, wt_bf, wt_bf, wt_bf, wt_bf, cos, sin, cos_t, sin_t)


def _gate_prep_kernel(g_ref, bi_ref, bf_ref, m0_ref, awe_ref, crow_ref, mo_ref, m_scr, *, L):
    @pl.when(pl.program_id(1) == 0)
    def _():
        m_scr[...] = m0_ref[0]

    rows = g_ref.shape[0]
    g = g_ref[...]
    ig = g[:, :LANES] + bi_ref[...]
    z = g[:, LANES:] + bf_ref[...]
    lf = jnp.minimum(z, 0.0) - jnp.log1p(jnp.exp(-jnp.abs(z)))
    assert L & (L - 1) == 0
    pos = lax.broadcasted_iota(jnp.int32, (rows, LANES), 0) & (L - 1)
    b = lf
    d = 1
    while d < L:
        b = b + jnp.where(pos >= d, pltpu.roll(b, d, 0), 0.0)
        d *= 2
    c = ig - b
    cm = c
    d = 1
    while d < L:
        cm = jnp.maximum(cm, jnp.where(pos >= d, pltpu.roll(cm, d, 0), -jnp.inf))
        d *= 2
    m_prev = m_scr[...]
    for k in range(rows // L):
        sl = slice(k * L, (k + 1) * L)
        mm = jnp.maximum(m_prev, cm[sl])
        m_t = b[sl] + mm
        awe_ref[sl, 0:LANES] = -mm * LOG2_E
        awe_ref[sl, LANES:2 * LANES] = jnp.exp(m_prev - mm)
        awe_ref[sl, 2 * LANES:3 * LANES] = jnp.exp(-m_t)
        m_prev = m_t[L - 1:L, :]
    c2 = c * LOG2_E
    if rows < LANES:
        c_sq = jnp.concatenate([c2, jnp.zeros((LANES - rows, LANES), F32)], axis=0)
    else:
        c_sq = c2
    c_t = c_sq.T
    for hh in range(N_HEADS):
        crow_ref[0, hh] = c_t[hh:hh + 1, :rows]
    m_scr[...] = m_prev
    mo_ref[0] = m_prev


def _gate_prep(gates, bi_row, bf_row, m0, n_streams, n_chunks, L, row_off):
    rows_out = n_streams * n_chunks * L
    per_step = PREP_CHUNKS if n_chunks % PREP_CHUNKS == 0 else 1
    n_steps = n_chunks // per_step
    rb = per_step * L
    assert (row_off * L) % rb == 0
    blk_off = row_off * L // rb
    return pl.pallas_call(
        functools.partial(_gate_prep_kernel, L=L),
        out_shape=(jax.ShapeDtypeStruct((rows_out, 3 * LANES), F32),
                   jax.ShapeDtypeStruct((n_streams, N_HEADS, 1, n_chunks * L), F32),
                   jax.ShapeDtypeStruct((n_streams, 1, LANES), F32)),
        grid=(n_streams, n_steps),
        in_specs=[pl.BlockSpec((rb, 2 * LANES), lambda s, c: (blk_off + s * n_steps + c, 0)),
                  pl.BlockSpec((1, LANES), lambda s, c: (0, 0)),
                  pl.BlockSpec((1, LANES), lambda s, c: (0, 0)),
                  pl.BlockSpec((1, 1, LANES), lambda s, c: (s, 0, 0))],
        out_specs=(pl.BlockSpec((rb, 3 * LANES), lambda s, c: (s * n_steps + c, 0)),
                   pl.BlockSpec((1, N_HEADS, 1, rb), lambda s, c: (s, 0, 0, c)),
                   pl.BlockSpec((1, 1, LANES), lambda s, c: (s, 0, 0))),
        scratch_shapes=[pltpu.VMEM((1, LANES), F32)],
        compiler_params=_params(),
        name="gate_prep",
    )(gates, bi_row, bf_row, m0)


def _lane_tile(x, n):
    if n <= LANES:
        return x[:, :n]
    return jnp.concatenate([x] * (n // LANES), axis=1)


def _head_norm(h, g_row):
    ms = jnp.mean(h * h, axis=-1, keepdims=True)
    return h * lax.rsqrt(ms + EPS) * g_row


def _rec_kernel(p_ref, kt_ref, awe_ref, c_ref, dec_ref, inter_ref, wsb_ref, gl_ref, ga_ref, gb_ref,
                c0_ref, n0_ref, s0_ref,
                y_ref, co_ref, no_ref, so_ref,
                c_scr, n_scr, s_scr, *, L):
    ci = pl.program_id(2)

    @pl.when(ci == 0)
    def _():
        c_scr[...] = c0_ref[0]
        n_scr[...] = n0_ref[0]
        s_scr[...] = s0_ref[0]

    ones_l = jnp.ones((L, LANES), BF16)
    row = lax.broadcasted_iota(jnp.int32, (L, L), 0)
    col = lax.broadcasted_iota(jnp.int32, (L, L), 1)
    causal = col <= row
    lane = lax.broadcasted_iota(jnp.int32, (L, LANES), 1)
    last = ci == pl.num_programs(2) - 1

    hb = range(HEADS_PER_STEP)

    def sec(j, i):
        lo = j * P_HEAD + i * D_HEAD
        return p_ref[:, lo:lo + D_HEAD]

    def pick(x, j):
        mine = lane == pl.program_id(1) * HEADS_PER_STEP + j
        return jnp.sum(jnp.where(mine, x, 0.0), axis=1, keepdims=True)

    def mm(x, y):
        return jnp.dot(x, y, preferred_element_type=F32)

    qa = [sec(j, 0) for j in hb]
    va = [sec(j, 1) for j in hb]
    qb = [sec(j, 4) for j in hb]
    vb = [sec(j, 5) for j in hb]
    kat = [kt_ref[0, j, 0:D_HEAD, :] for j in hb]
    kbt = [kt_ref[0, j, D_HEAD:2 * D_HEAD, :] for j in hb]
    cmat = [c_scr[j] for j in hb]
    nrep = [n_scr[j] for j in hb]
    smat = [s_scr[j] for j in hb]

    s_raw = [mm(qa[j], kat[j]) for j in hb]
    sb_raw = [mm(qb[j], kbt[j]) for j in hb]
    qc = [mm(qa[j], cmat[j].astype(BF16)) for j in hb]
    qn = [mm(qa[j], nrep[j].astype(BF16)) for j in hb]
    qs = [mm(qb[j], smat[j].astype(BF16)) for j in hb]

    a = [pick(awe_ref[:, 0:LANES], j) for j in hb]
    wi = [pick(awe_ref[:, LANES:2 * LANES], j) for j in hb]
    e = [pick(awe_ref[:, 2 * LANES:3 * LANES], j) for j in hb]
    c = [c_ref[0, j] for j in hb]
    s_bf = [(s_raw[j] * jnp.exp2(jnp.where(causal, a[j] + c[j], -jnp.inf))).astype(BF16) for j in hb]
    sb_bf = [(sb_raw[j] * dec_ref[j]).astype(BF16) for j in hb]

    w_state = [jnp.exp2(a[j][L - 1:L, :] + c[j]) for j in hb]
    kw = [kat[j] * w_state[j].astype(BF16) for j in hb]
    kwb = [kbt[j] * wsb_ref[j].astype(BF16) for j in hb]

    sv = [mm(s_bf[j], va[j]) for j in hb]
    rowsum = [mm(s_bf[j], ones_l) for j in hb]
    ob = [mm(sb_bf[j], vb[j]) for j in hb]
    c_upd = [mm(kw[j], va[j]) for j in hb]
    n_upd = [mm(kw[j], ones_l) for j in hb]
    s_upd = [mm(kwb[j], vb[j]) for j in hb]

    for j in hb:
        den = wi[j] * qn[j] + rowsum[j]
        r = 1.0 / jnp.maximum(jnp.abs(den), e[j])
        h = (wi[j] * qc[j] + sv[j]) * _lane_tile(r, D_HEAD)
        ya = _head_norm(sec(j, 2).astype(F32) * h, ga_ref[j]) * sec(j, 3).astype(F32)
        y_ref[:, j * Y_HEAD:j * Y_HEAD + D_HEAD] = ya.astype(BF16)
        o = ob[j] + _lane_tile(inter_ref[j], D_HEAD) * qs[j]
        yb = _head_norm(o, gb_ref[j]) * sec(j, 6).astype(F32)
        y_ref[:, j * Y_HEAD + D_HEAD:(j + 1) * Y_HEAD] = yb.astype(BF16)

    for j in hb:
        decay = wi[j][L - 1:L, :]
        c_new = decay * cmat[j] + c_upd[j]
        n_new = decay * nrep[j] + n_upd[j]
        s_new = _lane_tile(gl_ref[j], D_HEAD) * smat[j] + s_upd[j]
        c_scr[j] = c_new
        n_scr[j] = n_new
        s_scr[j] = s_new

        @pl.when(last)
        def _(j=j, c_new=c_new, n_new=n_new, s_new=s_new):
            co_ref[0, j] = c_new
            no_ref[0, j] = n_new
            so_ref[0, j] = s_new


def _recurrence(p, kt, awe, c_row, tabs, ga, gb, c0, n0, s0, n_streams, n_chunks, L, row_off):
    dec, inter, wsb, gl = tabs
    hb = HEADS_PER_STEP
    y_rows = n_streams * n_chunks * L

    def per_head(*tail):
        return pl.BlockSpec((hb,) + tail, lambda s, h, c: (h,) + (0,) * len(tail))

    state_spec = pl.BlockSpec((1, hb, D_HEAD, D_HEAD), lambda s, h, c: (s, h, 0, 0))
    nstate_spec = pl.BlockSpec((1, hb, D_HEAD, LANES), lambda s, h, c: (s, h, 0, 0))
    in_specs = [pl.BlockSpec((L, hb * P_HEAD), lambda s, h, c: (row_off + s * n_chunks + c, h)),
                pl.BlockSpec((1, hb, 2 * D_HEAD, L), lambda s, h, c: (s, h, 0, c)),
                pl.BlockSpec((L, 3 * LANES), lambda s, h, c: (s * n_chunks + c, 0)),
                pl.BlockSpec((1, hb, 1, L), lambda s, h, c: (s, h, 0, c)),
                per_head(L, L), per_head(L, LANES), per_head(1, L), per_head(1, LANES),
                per_head(1, D_HEAD), per_head(1, D_HEAD),
                state_spec, nstate_spec, state_spec]
    st = jax.ShapeDtypeStruct((n_streams, N_HEADS, D_HEAD, D_HEAD), F32)
    return pl.pallas_call(
        functools.partial(_rec_kernel, L=L),
        out_shape=(jax.ShapeDtypeStruct((y_rows, N_HEADS * Y_HEAD), BF16),
                   st, jax.ShapeDtypeStruct((n_streams, N_HEADS, D_HEAD, LANES), F32), st),
        grid=(n_streams, N_HEADS // hb, n_chunks),
        in_specs=in_specs,
        out_specs=(pl.BlockSpec((L, hb * Y_HEAD), lambda s, h, c: (s * n_chunks + c, h)),
                   state_spec, nstate_spec, state_spec),
        scratch_shapes=[pltpu.VMEM((hb, D_HEAD, D_HEAD), F32),
                        pltpu.VMEM((hb, D_HEAD, LANES), F32),
                        pltpu.VMEM((hb, D_HEAD, D_HEAD), F32)],
        compiler_params=_params(),
        name="recurrence",
    )(p, kt, awe, c_row, dec, inter, wsb, gl, ga, gb, c0, n0, s0)


def _mixer_kernel(p_ref, kt_ref, awe_ref, c_ref, dec_ref, inter_ref, wsb_ref, gl_ref, ga_ref, gb_ref,
                  c0_ref, n0_ref, s0_ref, w_ref, x_ref, gf_ref,
                  o_ref, co_ref, no_ref, so_ref,
                  c_scr, n_scr, s_scr, y_scr, *, L):
    tile = pl.program_id(0)
    step = pl.program_id(1)
    head = jnp.minimum(step, N_HEADS - 1)
    d = o_ref.shape[1]

    def mm(x, y):
        return jnp.dot(x, y, preferred_element_type=F32)

    def load_initial_state():
        @pl.when(tile == 0)
        def _():
            c_scr[head] = c0_ref[0, 0]
            n_scr[head] = n0_ref[0, 0]
            s_scr[head] = s0_ref[0, 0]

    def store_final_state(cmat, nrep, smat):
        @pl.when(tile == pl.num_programs(0) - 1)
        def _():
            co_ref[0, 0] = cmat
            no_ref[0, 0] = nrep
            so_ref[0, 0] = smat

    def recurrence_step():
        ones_l = jnp.ones((L, LANES), BF16)
        row = lax.broadcasted_iota(jnp.int32, (L, L), 0)
        col = lax.broadcasted_iota(jnp.int32, (L, L), 1)
        causal = col <= row
        mine = lax.broadcasted_iota(jnp.int32, (L, LANES), 1) == head

        def pick(x):
            return jnp.sum(jnp.where(mine, x, 0.0), axis=1, keepdims=True)

        subs = range(MIXER_SUB)
        rows = [slice(u * L, (u + 1) * L) for u in subs]

        def sec(u, i):
            return p_ref[rows[u], i * D_HEAD:(i + 1) * D_HEAD]

        dec = dec_ref[head]
        inter = _lane_tile(inter_ref[head], D_HEAD)
        wsb = wsb_ref[head].astype(BF16)
        gl = _lane_tile(gl_ref[head], D_HEAD)
        c_all = c_ref[0, head]
        y_half = y_scr.at[step % 2]

        qa = [sec(u, 0) for u in subs]
        va = [sec(u, 1) for u in subs]
        qb = [sec(u, 4) for u in subs]
        vb = [sec(u, 5) for u in subs]
        kat = [kt_ref[0, 0, 0:D_HEAD, rows[u]] for u in subs]
        kbt = [kt_ref[0, 0, D_HEAD:2 * D_HEAD, rows[u]] for u in subs]
        a = [pick(awe_ref[rows[u], 0:LANES]) for u in subs]
        wi = [pick(awe_ref[rows[u], LANES:2 * LANES]) for u in subs]
        e = [pick(awe_ref[rows[u], 2 * LANES:3 * LANES]) for u in subs]
        c = [c_all[:, rows[u]] for u in subs]
        s_bf = [(mm(qa[u], kat[u]) * jnp.exp2(jnp.where(causal, a[u] + c[u], -jnp.inf))).astype(BF16)
                for u in subs]
        sb_bf = [(mm(qb[u], kbt[u]) * dec).astype(BF16) for u in subs]
        kw = [kat[u] * jnp.exp2(a[u][L - 1:L, :] + c[u]).astype(BF16) for u in subs]
        kwb = [kbt[u] * wsb for u in subs]
        sv = [mm(s_bf[u], va[u]) for u in subs]
        rowsum = [mm(s_bf[u], ones_l) for u in subs]
        ob = [mm(sb_bf[u], vb[u]) for u in subs]
        c_upd = [mm(kw[u], va[u]) for u in subs]
        n_upd = [mm(kw[u], ones_l) for u in subs]
        s_upd = [mm(kwb[u], vb[u]) for u in subs]

        cmat = c_scr[head]
        nrep = n_scr[head]
        smat = s_scr[head]
        for u in subs:
            qc = mm(qa[u], cmat.astype(BF16))
            qn = mm(qa[u], nrep.astype(BF16))
            qs = mm(qb[u], smat.astype(BF16))
            den = wi[u] * qn + rowsum[u]
            r = 1.0 / jnp.maximum(jnp.abs(den), e[u])
            h = (wi[u] * qc + sv[u]) * _lane_tile(r, D_HEAD)
            ya = _head_norm(sec(u, 2).astype(F32) * h, ga_ref[head]) * sec(u, 3).astype(F32)
            y_half[rows[u], 0:D_HEAD] = ya.astype(BF16)
            yb = _head_norm(ob[u] + inter * qs, gb_ref[head]) * sec(u, 6).astype(F32)
            y_half[rows[u], D_HEAD:Y_HEAD] = yb.astype(BF16)
            decay = wi[u][L - 1:L, :]
            cmat = decay * cmat + c_upd[u]
            nrep = decay * nrep + n_upd[u]
            smat = gl * smat + s_upd[u]
        c_scr[head] = cmat
        n_scr[head] = nrep
        s_scr[head] = smat
        return cmat, nrep, smat

    def accumulate():
        y = y_scr[(step + 1) % 2]
        tq = d // OUT_COL_SPLIT
        for q in range(OUT_COL_SPLIT):
            o_ref[:, q * tq:(q + 1) * tq] += mm(y, w_ref[:, q * tq:(q + 1) * tq])

    def add_residual():
        tx = d // N_HEADS
        for hh in range(N_HEADS):
            @pl.when(step == hh + 1)
            def _(hh=hh):
                o_ref[:, hh * tx:(hh + 1) * tx] += x_ref[...]

    @pl.when(step == 0)
    def _():
        load_initial_state()
        o_ref[...] = jnp.zeros(o_ref.shape, F32)
        store_final_state(*recurrence_step())

    @pl.when((step > 0) & (step < N_HEADS))
    def _():
        load_initial_state()
        accumulate()
        state = recurrence_step()
        store_final_state(*state)
        add_residual()

    @pl.when(step == N_HEADS)
    def _():
        tx = d // N_HEADS
        lo = (N_HEADS - 1) * tx
        halves = [slice(u * L, (u + 1) * L) for u in range(MIXER_SUB)]
        y = y_scr[(step + 1) % 2]
        zs = [o_ref[r, :] + mm(y[r], w_ref[...]) for r in halves]
        for r, z in zip(halves, zs):
            z = jnp.concatenate([z[:, :lo], z[:, lo:] + x_ref[r, :]], axis=1)
            ms = jnp.mean(z * z, axis=-1, keepdims=True)
            o_ref[r, :] = z * lax.rsqrt(ms + EPS) * gf_ref[...]


def _mixer(p, kt, awe, c_row, tabs, ga, gb, c0, n0, s0, w_bf, x, gf_row, L):
    dec, inter, wsb, gl = tabs
    seq, d = x.shape
    tile_rows = MIXER_SUB * L
    n_tiles = seq // tile_rows

    def resident(arr):
        nd = arr.ndim
        return pl.BlockSpec(arr.shape, lambda t, s: (0,) * nd, pipeline_mode=pl.Buffered(1))

    def rec_head(s):
        return jnp.minimum(s, N_HEADS - 1)

    def acc_head(s):
        return jnp.maximum(s - 1, 0)

    def first_tile(*tail):
        return pl.BlockSpec((1, 1) + tail,
                            lambda t, s: (0, jnp.where(t == 0, rec_head(s), N_HEADS - 1), 0, 0))

    def last_tile(*tail):
        return pl.BlockSpec((1, 1) + tail,
                            lambda t, s: (0, jnp.where(t == n_tiles - 1, rec_head(s), 0), 0, 0))

    in_specs = [pl.BlockSpec((tile_rows, P_HEAD), lambda t, s: (t, rec_head(s))),
                pl.BlockSpec((1, 1, 2 * D_HEAD, tile_rows), lambda t, s: (0, rec_head(s), 0, t)),
                pl.BlockSpec((tile_rows, 3 * LANES), lambda t, s: (t, 0)),
                pl.BlockSpec((1, N_HEADS, 1, tile_rows), lambda t, s: (0, 0, 0, t)),
                resident(dec), resident(inter), resident(wsb), resident(gl), resident(ga), resident(gb),
                first_tile(D_HEAD, D_HEAD), first_tile(D_HEAD, LANES), first_tile(D_HEAD, D_HEAD),
                pl.BlockSpec((Y_HEAD, d), lambda t, s: (acc_head(s), 0)),
                pl.BlockSpec((tile_rows, d // N_HEADS), lambda t, s: (t, acc_head(s))),
                resident(gf_row)]
    st = jax.ShapeDtypeStruct((1, N_HEADS, D_HEAD, D_HEAD), F32)
    return pl.pallas_call(
        functools.partial(_mixer_kernel, L=L),
        out_shape=(jax.ShapeDtypeStruct((seq, d), F32),
                   st, jax.ShapeDtypeStruct((1, N_HEADS, D_HEAD, LANES), F32), st),
        grid=(n_tiles, N_HEADS + 1),
        in_specs=in_specs,
        out_specs=(pl.BlockSpec((tile_rows, d), lambda t, s: (t, 0)),
                   last_tile(D_HEAD, D_HEAD), last_tile(D_HEAD, LANES), last_tile(D_HEAD, D_HEAD)),
        scratch_shapes=[pltpu.VMEM((N_HEADS, D_HEAD, D_HEAD), F32),
                        pltpu.VMEM((N_HEADS, D_HEAD, LANES), F32),
                        pltpu.VMEM((N_HEADS, D_HEAD, D_HEAD), F32),
                        pltpu.VMEM((2, tile_rows, Y_HEAD), BF16)],
        compiler_params=_params(),
        name="mixer",
    )(p, kt, awe, c_row, dec, inter, wsb, gl, ga, gb, c0, n0, s0, w_bf, x, gf_row)


def _out_kernel(y_ref, w_ref, x_ref, g_ref, o_ref, *, n_k, tx):
    k = pl.program_id(1)

    @pl.when(k == 0)
    def _():
        o_ref[...] = jnp.dot(y_ref[...], w_ref[...], preferred_element_type=F32)

    @pl.when(k > 0)
    def _():
        o_ref[...] += jnp.dot(y_ref[...], w_ref[...], preferred_element_type=F32)

    for kk in range(n_k):
        @pl.when(k == kk)
        def _(kk=kk):
            o_ref[:, kk * tx:(kk + 1) * tx] += x_ref[...]

    @pl.when(k == n_k - 1)
    def _():
        z = o_ref[...]
        ms = jnp.mean(z * z, axis=-1, keepdims=True)
        o_ref[...] = z * lax.rsqrt(ms + EPS) * g_ref[...]


def _out_proj(y, w_bf, x, g_row, tm, n_k):
    rows, d = x.shape
    tk = y.shape[1] // n_k
    tx = d // n_k
    return pl.pallas_call(
        functools.partial(_out_kernel, n_k=n_k, tx=tx),
        out_shape=jax.ShapeDtypeStruct((rows, d), F32),
        grid=(rows // tm, n_k),
        in_specs=[pl.BlockSpec((tm, tk), lambda i, k: (i, k)),
                  pl.BlockSpec((tk, d), lambda i, k: (k, 0)),
                  pl.BlockSpec((tm, tx), lambda i, k: (i, k)),
                  pl.BlockSpec((1, d), lambda i, k: (0, 0))],
        out_specs=pl.BlockSpec((tm, d), lambda i, k: (i, 0)),
        compiler_params=_params(),
        name="out_proj",
    )(y, w_bf, x, g_row)


def _rope_tables(base, off):
    inv = ROPE_BASE ** (-jnp.arange(0, D_HEAD, 2, dtype=F32) / D_HEAD)
    ab = base.astype(F32)[:, None] * inv[None, :]
    ao = off.astype(F32)[:, None] * inv[None, :]
    cb, sb = jnp.cos(ab)[:, None, :], jnp.sin(ab)[:, None, :]
    co, so = jnp.cos(ao)[None, :, :], jnp.sin(ao)[None, :, :]
    n = base.shape[0] * off.shape[0]
    return (cb * co - sb * so).reshape(n, HALF), (sb * co + cb * so).reshape(n, HALF)


def _retention_tables(L):
    log_gamma = jnp.log1p(-jnp.power(2.0, -RET_DECAY_BASE - jnp.arange(N_HEADS, dtype=F32)))
    j = jnp.arange(L, dtype=F32)
    diff = j[:, None] - j[None, :]
    dec = jnp.where(diff >= 0, jnp.exp(log_gamma[:, None, None] * jnp.maximum(diff, 0.0)), 0.0)
    inter = jnp.exp(log_gamma[:, None] * (j + 1.0))
    inter = jnp.broadcast_to(inter[:, :, None], (N_HEADS, L, LANES))
    wsb = jnp.exp(log_gamma[:, None] * (L - 1.0 - j))[:, None, :]
    gl = jnp.broadcast_to(jnp.exp(log_gamma * L)[:, None, None], (N_HEADS, 1, LANES))
    return dec, inter, wsb, gl


def _pad_lanes(v):
    return jnp.zeros((1, LANES), F32).at[0, :v.shape[0]].set(v.astype(F32))


def _run_group(p, kt, gates, bi_row, bf_row, tabs, ga, gb, state, n_streams, n_chunks, L, row_off):
    c0, n0, m0, s0 = state
    awe, c_row, m_out = _gate_prep(gates, bi_row, bf_row, m0, n_streams, n_chunks, L, row_off)
    y, c_out, n_out, s_out = _recurrence(p, kt, awe, c_row, tabs, ga, gb, c0, n0, s0,
                                         n_streams, n_chunks, L, row_off)
    return y, (c_out, n_out, m_out, s_out)


def kernel(x_prompt, x_sample, state_mlstm_C, state_mlstm_n, state_mlstm_m, state_ret_S,
           meta_tokens, g_norm1, w_in, b_igate, b_fgate, g_head_a, g_head_b, w_out, g_final):
    depth = w_in.shape[0]
    assert depth == 1 and x_prompt.shape[0] == 1
    d = x_prompt.shape[-1]
    seq = x_prompt.shape[1]
    n_dec, t_dec = x_sample.shape[0], x_sample.shape[1]
    n_meta = meta_tokens.shape[0]
    width = N_HEADS * D_HEAD
    assert seq % MAIN_CHUNK == 0 and MAIN_CHUNK % t_dec == 0 and t_dec % n_meta == 0

    wt = w_in[0].T
    wt_bf = wt.astype(BF16)
    wg_t = jnp.zeros((2 * LANES, d), F32)
    wg_t = wg_t.at[:N_HEADS].set(wt[9 * width:9 * width + N_HEADS])
    wg_t = wg_t.at[LANES:LANES + N_HEADS].set(wt[9 * width + N_HEADS:])
    wg_t = wg_t.astype(BF16)
    w_out_bf = (w_out[0].reshape(2, N_HEADS, D_HEAD, d).transpose(1, 0, 2, 3)
                .reshape(2 * width, d).astype(BF16))
    g1 = g_norm1[0][None, :]
    bi_row = _pad_lanes(b_igate[0])
    bf_row = _pad_lanes(b_fgate[0])
    ga = g_head_a[0].reshape(N_HEADS, 1, D_HEAD)
    gb = g_head_b[0].reshape(N_HEADS, 1, D_HEAD)

    x_main = x_prompt[0]
    n_samp = n_dec * t_dec
    meta_at = seq + n_samp
    small_pad = -(meta_at + n_meta) % PROJ_ROWS
    x_small = jnp.concatenate([x_sample.reshape(n_samp, d), meta_tokens.astype(F32),
                               jnp.zeros((small_pad, d), F32)], axis=0)
    n_rows = seq + x_small.shape[0]
    assert seq % NORM_ROWS == 0 and x_small.shape[0] % NORM_ROWS == 0
    blk = jnp.arange(n_rows // t_dec)
    base = jnp.where(blk < seq // t_dec, n_meta + t_dec * blk,
                     jnp.where(blk < meta_at // t_dec, n_meta + PAST_LEN, 0))
    cos, sin = _rope_tables(base, jnp.arange(t_dec))

    xn, gates = _prenorm(x_main, x_small, g1, wg_t, NORM_ROWS)
    p, kt = _proj(xn, wt_bf, cos, sin, cos.T, sin.T, PROJ_ROWS)

    zeros_c = jnp.zeros((1, N_HEADS, D_HEAD, D_HEAD), F32)
    zero_state = (zeros_c, jnp.zeros((1, N_HEADS, D_HEAD, LANES), F32), jnp.zeros((1, 1, LANES), F32), zeros_c)
    kt_meta = kt[:, :, meta_at:meta_at + n_meta][None]
    _, st_meta = _run_group(p, kt_meta, gates, bi_row, bf_row, _retention_tables(n_meta),
                            ga, gb, zero_state, 1, 1, n_meta, meta_at // n_meta)

    gf = g_final[None, :]
    c_meta, n_meta_st, m_meta, s_meta = st_meta
    awe_p, c_row_p, m_p = _gate_prep(gates, bi_row, bf_row, m_meta, 1, seq // MAIN_CHUNK, MAIN_CHUNK, 0)
    y_prompt, c_p, n_p, s_p = _mixer(p, kt[None], awe_p, c_row_p, _retention_tables(MAIN_CHUNK), ga, gb,
                                     c_meta, n_meta_st, s_meta, w_out_bf, x_main, gf, MAIN_CHUNK)
    st_p = (c_p, n_p, m_p, s_p)

    kt_samp = kt[:, :, seq:meta_at].reshape(N_HEADS, 2 * D_HEAD, n_dec, t_dec).transpose(2, 0, 1, 3)
    n0 = jnp.broadcast_to(state_mlstm_n[0].astype(F32)[..., None], (n_dec, N_HEADS, D_HEAD, LANES))
    m0 = jnp.zeros((n_dec, 1, LANES), F32).at[:, 0, :N_HEADS].set(state_mlstm_m[0].astype(F32))
    st0 = (state_mlstm_C[0].astype(F32), n0, m0, state_ret_S[0].astype(F32))
    y_s, st_s = _run_group(p, kt_samp, gates, bi_row, bf_row, _retention_tables(t_dec),
                           ga, gb, st0, n_dec, 1, t_dec, seq // t_dec)

    y_prompt = y_prompt[None]
    y_sample = _out_proj(y_s, w_out_bf, x_sample.reshape(n_samp, d), gf, 512, 4)
    y_sample = y_sample.reshape(n_dec, t_dec, d)

    dp = x_prompt.dtype
    ds = state_mlstm_C.dtype

    def states(st, dt):
        c_out, n_out, m_out, s_out = st
        return (c_out[None].astype(dt), n_out[..., 0][None].astype(dt),
                m_out[:, 0, :N_HEADS][None].astype(dt), s_out[None].astype(dt))

    pc, pn, pm, ps = states(st_p, dp)
    sc, sn, sm, ss = states(st_s, ds)
    return (y_prompt, y_sample, pc, pn, pm, ps, sc, sn, sm, ss)
```

```python
import functools

import jax
import jax.numpy as jnp
from jax import lax
from jax.experimental import pallas as pl
from jax.experimental.pallas import tpu as pltpu

F32 = jnp.float32
BF16 = jnp.bfloat16

EPS = 1e-6
LOG2_E = 1.4426950408889634
ROPE_BASE = 10000.0
RET_DECAY_BASE = 5.0
PAST_LEN = 2048

N_HEADS = 8
D_HEAD = 256
HALF = D_HEAD // 2
LANES = 128
MAIN_CHUNK = 256
HEADS_PER_STEP = 4
PREP_CHUNKS = 4
MIXER_SUB = 2
OUT_COL_SPLIT = 4
NORM_ROWS = 256
PROJ_ROWS = 768
PROJ_PREFETCHED = 3
P_SECTIONS = 7
P_HEAD = P_SECTIONS * D_HEAD
Y_HEAD = 2 * D_HEAD

VMEM_LIMIT = 56 * 1024 * 1024


def _params(vmem=VMEM_LIMIT):
    return pltpu.CompilerParams(vmem_limit_bytes=vmem)


def _nt_dot(a, b):
    return lax.dot_general(a, b, (((1,), (1,)), ((), ())), preferred_element_type=F32)


def _prenorm_kernel(*refs, starts):
    n_in = len(starts) - 1
    x_refs = refs[:n_in]
    g_ref, wg_ref, xn_ref, gates_ref = refs[n_in:]
    i = pl.program_id(0)
    for k, x_ref in enumerate(x_refs):
        @pl.when((i >= starts[k]) & (i < starts[k + 1]))
        def _(x_ref=x_ref):
            x = x_ref[...]
            ms = jnp.mean(x * x, axis=-1, keepdims=True)
            xn = (x * lax.rsqrt(ms + EPS) * g_ref[...]).astype(BF16)
            xn_ref[...] = xn
            gates_ref[...] = _nt_dot(xn, wg_ref[...])


def _prenorm(xs, g_row, wg_t, tm):
    d = xs[0].shape[1]
    starts = [0]
    for x in xs:
        assert x.shape[0] % tm == 0
        starts.append(starts[-1] + x.shape[0] // tm)

    def x_spec(k):
        return pl.BlockSpec((tm, d), lambda i: (jnp.clip(i - starts[k], 0, starts[k + 1] - starts[k] - 1), 0))

    rows = starts[-1] * tm
    return pl.pallas_call(
        functools.partial(_prenorm_kernel, starts=tuple(starts)),
        out_shape=(jax.ShapeDtypeStruct((rows, d), BF16),
                   jax.ShapeDtypeStruct((rows, 2 * LANES), F32)),
        grid=(starts[-1],),
        in_specs=[x_spec(k) for k in range(len(xs))] + [
            pl.BlockSpec((1, d), lambda i: (0, 0)),
            pl.BlockSpec((2 * LANES, d), lambda i: (0, 0))],
        out_specs=(pl.BlockSpec((tm, d), lambda i: (i, 0)),
                   pl.BlockSpec((tm, 2 * LANES), lambda i: (i, 0))),
        compiler_params=_params(),
        name="prenorm",
    )(*xs, g_row, wg_t)


def _silu(z):
    return z * jax.nn.sigmoid(z)


def _proj_kernel(xn_ref, wqa, wva, woa, wza, wqb, wvb, wzb, wka, wkb,
                 cos_ref, sin_ref, cos_t_ref, sin_t_ref, p_ref, kt_ref):
    xn = xn_ref[...]

    def tok(w_ref):
        return _nt_dot(xn, w_ref[...])

    def put(sec, val):
        p_ref[:, sec * D_HEAD:(sec + 1) * D_HEAD] = val.astype(BF16)

    put(0, tok(wqa))
    put(1, tok(wva))
    put(2, jax.nn.sigmoid(tok(woa)))
    put(3, _silu(tok(wza)))
    q = tok(wqb)
    c = cos_ref[...]
    s = sin_ref[...]
    x1 = q[:, :HALF]
    x2 = q[:, HALF:]
    p_ref[:, 4 * D_HEAD:4 * D_HEAD + HALF] = (x1 * c - x2 * s).astype(BF16)
    p_ref[:, 4 * D_HEAD + HALF:5 * D_HEAD] = (x1 * s + x2 * c).astype(BF16)
    put(5, tok(wvb))
    put(6, _silu(tok(wzb)))

    scale = D_HEAD ** -0.5
    kt_ref[0, 0:D_HEAD, :] = (_nt_dot(wka[...], xn) * scale).astype(BF16)
    kb = _nt_dot(wkb[...], xn)
    ct = cos_t_ref[...] * scale
    st = sin_t_ref[...] * scale
    k1 = kb[:HALF]
    k2 = kb[HALF:]
    kt_ref[0, D_HEAD:D_HEAD + HALF, :] = (k1 * ct - k2 * st).astype(BF16)
    kt_ref[0, D_HEAD + HALF:2 * D_HEAD, :] = (k1 * st + k2 * ct).astype(BF16)


def _proj(xn, wt_bf, cos, sin, cos_t, sin_t, tm):
    rows, d = xn.shape
    def w_spec(wsec, buffers):
        return pl.BlockSpec((D_HEAD, d), lambda h, m: (wsec * N_HEADS + h, 0),
                            pipeline_mode=pl.Buffered(buffers))

    order = (0, 2, 3, 4, 5, 7, 8, 1, 6)
    w_specs = [w_spec(wsec, 2 if i < PROJ_PREFETCHED else 1) for i, wsec in enumerate(order)]
    tab = pl.BlockSpec((tm, HALF), lambda h, m: (m, 0))
    tab_t = pl.BlockSpec((HALF, tm), lambda h, m: (0, m))
    return pl.pallas_call(
        _proj_kernel,
        out_shape=(jax.ShapeDtypeStruct((rows, N_HEADS * P_HEAD), BF16),
                   jax.ShapeDtypeStruct((N_HEADS, 2 * D_HEAD, rows), BF16)),
        grid=(N_HEADS, rows // tm),
        in_specs=[pl.BlockSpec((tm, d), lambda h, m: (m, 0))] + w_specs + [tab, tab, tab_t, tab_t],
        out_specs=(pl.BlockSpec((tm, P_HEAD), lambda h, m: (m, h)),
                   pl.BlockSpec((1, 2 * D_HEAD, tm), lambda h, m: (h, 0, m))),
        compiler_params=_params(),
        name="proj",
    )(xn, wt_bf, wt_bf, wt_bf, wt_bf, wt_bf, wt_bf, wt_bf, wt_bf, wt_bf, cos, sin, cos_t, sin_t)


def _gate_prep_kernel(g_ref, bi_ref, bf_ref, m0_ref, awe_ref, crow_ref, mo_ref, m_scr, *, L):
    @pl.when(pl.program_id(1) == 0)
    def _():
        m_scr[...] = m0_ref[0]

    rows = g_ref.shape[0]
    g = g_ref[...]
    ig = g[:, :LANES] + bi_ref[...]
    z = g[:, LANES:] + bf_ref[...]
    lf = jnp.minimum(z, 0.0) - jnp.log1p(jnp.exp(-jnp.abs(z)))
    assert L & (L - 1) == 0
    pos = lax.broadcasted_iota(jnp.int32, (rows, LANES), 0) & (L - 1)
    b = lf
    d = 1
    while d < L:
        b = b + jnp.where(pos >= d, pltpu.roll(b, d, 0), 0.0)
        d *= 2
    c = ig - b
    cm = c
    d = 1
    while d < L:
        cm = jnp.maximum(cm, jnp.where(pos >= d, pltpu.roll(cm, d, 0), -jnp.inf))
        d *= 2
    m_prev = m_scr[...]
    for k in range(rows // L):
        sl = slice(k * L, (k + 1) * L)
        mm = jnp.maximum(m_prev, cm[sl])
        m_t = b[sl] + mm
        awe_ref[sl, 0:LANES] = -mm * LOG2_E
        awe_ref[sl, LANES:2 * LANES] = jnp.exp(m_prev - mm)
        awe_ref[sl, 2 * LANES:3 * LANES] = jnp.exp(-m_t)
        m_prev = m_t[L - 1:L, :]
    c2 = c * LOG2_E
    if rows < LANES:
        c_sq = jnp.concatenate([c2, jnp.zeros((LANES - rows, LANES), F32)], axis=0)
    else:
        c_sq = c2
    c_t = c_sq.T
    for hh in range(N_HEADS):
        crow_ref[0, hh] = c_t[hh:hh + 1, :rows]
    m_scr[...] = m_prev
    mo_ref[0] = m_prev


def _gate_prep(gates, bi_row, bf_row, m0, n_streams, n_chunks, L, row_off):
    rows_out = n_streams * n_chunks * L
    per_step = PREP_CHUNKS if n_chunks % PREP_CHUNKS == 0 else 1
    n_steps = n_chunks // per_step
    rb = per_step * L
    assert (row_off * L) % rb == 0
    blk_off = row_off * L // rb
    return pl.pallas_call(
        functools.partial(_gate_prep_kernel, L=L),
        out_shape=(jax.ShapeDtypeStruct((rows_out, 3 * LANES), F32),
                   jax.ShapeDtypeStruct((n_streams, N_HEADS, 1, n_chunks * L), F32),
                   jax.ShapeDtypeStruct((n_streams, 1, LANES), F32)),
        grid=(n_streams, n_steps),
        in_specs=[pl.BlockSpec((rb, 2 * LANES), lambda s, c: (blk_off + s * n_steps + c, 0)),
                  pl.BlockSpec((1, LANES), lambda s, c: (0, 0)),
                  pl.BlockSpec((1, LANES), lambda s, c: (0, 0)),
                  pl.BlockSpec((1, 1, LANES), lambda s, c: (s, 0, 0))],
        out_specs=(pl.BlockSpec((rb, 3 * LANES), lambda s, c: (s * n_steps + c, 0)),
                   pl.BlockSpec((1, N_HEADS, 1, rb), lambda s, c: (s, 0, 0, c)),
                   pl.BlockSpec((1, 1, LANES), lambda s, c: (s, 0, 0))),
        scratch_shapes=[pltpu.VMEM((1, LANES), F32)],
        compiler_params=_params(),
        name="gate_prep",
    )(gates, bi_row, bf_row, m0)


def _lane_tile(x, n):
    if n <= LANES:
        return x[:, :n]
    return jnp.concatenate([x] * (n // LANES), axis=1)


def _head_norm(h, g_row):
    ms = jnp.mean(h * h, axis=-1, keepdims=True)
    return h * lax.rsqrt(ms + EPS) * g_row


def _rec_kernel(p_ref, kt_ref, awe_ref, c_ref, dec_ref, inter_ref, wsb_ref, gl_ref, ga_ref, gb_ref,
                c0_ref, n0_ref, s0_ref,
                y_ref, co_ref, no_ref, so_ref,
                c_scr, n_scr, s_scr, *, L):
    ci = pl.program_id(2)

    @pl.when(ci == 0)
    def _():
        c_scr[...] = c0_ref[0]
        n_scr[...] = n0_ref[0]
        s_scr[...] = s0_ref[0]

    ones_l = jnp.ones((L, LANES), BF16)
    row = lax.broadcasted_iota(jnp.int32, (L, L), 0)
    col = lax.broadcasted_iota(jnp.int32, (L, L), 1)
    causal = col <= row
    lane = lax.broadcasted_iota(jnp.int32, (L, LANES), 1)
    last = ci == pl.num_programs(2) - 1

    hb = range(HEADS_PER_STEP)

    def sec(j, i):
        lo = j * P_HEAD + i * D_HEAD
        return p_ref[:, lo:lo + D_HEAD]

    def pick(x, j):
        mine = lane == pl.program_id(1) * HEADS_PER_STEP + j
        return jnp.sum(jnp.where(mine, x, 0.0), axis=1, keepdims=True)

    def mm(x, y):
        return jnp.dot(x, y, preferred_element_type=F32)

    qa = [sec(j, 0) for j in hb]
    va = [sec(j, 1) for j in hb]
    qb = [sec(j, 4) for j in hb]
    vb = [sec(j, 5) for j in hb]
    kat = [kt_ref[0, j, 0:D_HEAD, :] for j in hb]
    kbt = [kt_ref[0, j, D_HEAD:2 * D_HEAD, :] for j in hb]
    cmat = [c_scr[j] for j in hb]
    nrep = [n_scr[j] for j in hb]
    smat = [s_scr[j] for j in hb]

    s_raw = [mm(qa[j], kat[j]) for j in hb]
    sb_raw = [mm(qb[j], kbt[j]) for j in hb]
    qc = [mm(qa[j], cmat[j].astype(BF16)) for j in hb]
    qn = [mm(qa[j], nrep[j].astype(BF16)) for j in hb]
    qs = [mm(qb[j], smat[j].astype(BF16)) for j in hb]

    a = [pick(awe_ref[:, 0:LANES], j) for j in hb]
    wi = [pick(awe_ref[:, LANES:2 * LANES], j) for j in hb]
    e = [pick(awe_ref[:, 2 * LANES:3 * LANES], j) for j in hb]
    c = [c_ref[0, j] for j in hb]
    s_bf = [(s_raw[j] * jnp.exp2(jnp.where(causal, a[j] + c[j], -jnp.inf))).astype(BF16) for j in hb]
    sb_bf = [(sb_raw[j] * dec_ref[j]).astype(BF16) for j in hb]

    w_state = [jnp.exp2(a[j][L - 1:L, :] + c[j]) for j in hb]
    kw = [kat[j] * w_state[j].astype(BF16) for j in hb]
    kwb = [kbt[j] * wsb_ref[j].astype(BF16) for j in hb]

    sv = [mm(s_bf[j], va[j]) for j in hb]
    rowsum = [mm(s_bf[j], ones_l) for j in hb]
    ob = [mm(sb_bf[j], vb[j]) for j in hb]
    c_upd = [mm(kw[j], va[j]) for j in hb]
    n_upd = [mm(kw[j], ones_l) for j in hb]
    s_upd = [mm(kwb[j], vb[j]) for j in hb]

    for j in hb:
        den = wi[j] * qn[j] + rowsum[j]
        r = 1.0 / jnp.maximum(jnp.abs(den), e[j])
        h = (wi[j] * qc[j] + sv[j]) * _lane_tile(r, D_HEAD)
        ya = _head_norm(sec(j, 2).astype(F32) * h, ga_ref[j]) * sec(j, 3).astype(F32)
        y_ref[:, j * Y_HEAD:j * Y_HEAD + D_HEAD] = ya.astype(BF16)
        o = ob[j] + _lane_tile(inter_ref[j], D_HEAD) * qs[j]
        yb = _head_norm(o, gb_ref[j]) * sec(j, 6).astype(F32)
        y_ref[:, j * Y_HEAD + D_HEAD:(j + 1) * Y_HEAD] = yb.astype(BF16)

    for j in hb:
        decay = wi[j][L - 1:L, :]
        c_new = decay * cmat[j] + c_upd[j]
        n_new = decay * nrep[j] + n_upd[j]
        s_new = _lane_tile(gl_ref[j], D_HEAD) * smat[j] + s_upd[j]
        c_scr[j] = c_new
        n_scr[j] = n_new
        s_scr[j] = s_new

        @pl.when(last)
        def _(j=j, c_new=c_new, n_new=n_new, s_new=s_new):
            co_ref[0, j] = c_new
            no_ref[0, j] = n_new
            so_ref[0, j] = s_new


def _recurrence(p, kt, awe, c_row, tabs, ga, gb, c0, n0, s0, n_streams, n_chunks, L, row_off):
    dec, inter, wsb, gl = tabs
    hb = HEADS_PER_STEP
    y_rows = n_streams * n_chunks * L

    def per_head(*tail):
        return pl.BlockSpec((hb,) + tail, lambda s, h, c: (h,) + (0,) * len(tail))

    state_spec = pl.BlockSpec((1, hb, D_HEAD, D_HEAD), lambda s, h, c: (s, h, 0, 0))
    nstate_spec = pl.BlockSpec((1, hb, D_HEAD, LANES), lambda s, h, c: (s, h, 0, 0))
    in_specs = [pl.BlockSpec((L, hb * P_HEAD), lambda s, h, c: (row_off + s * n_chunks + c, h)),
                pl.BlockSpec((1, hb, 2 * D_HEAD, L), lambda s, h, c: (s, h, 0, c)),
                pl.BlockSpec((L, 3 * LANES), lambda s, h, c: (s * n_chunks + c, 0)),
                pl.BlockSpec((1, hb, 1, L), lambda s, h, c: (s, h, 0, c)),
                per_head(L, L), per_head(L, LANES), per_head(1, L), per_head(1, LANES),
                per_head(1, D_HEAD), per_head(1, D_HEAD),
                state_spec, nstate_spec, state_spec]
    st = jax.ShapeDtypeStruct((n_streams, N_HEADS, D_HEAD, D_HEAD), F32)
    return pl.pallas_call(
        functools.partial(_rec_kernel, L=L),
        out_shape=(jax.ShapeDtypeStruct((y_rows, N_HEADS * Y_HEAD), BF16),
                   st, jax.ShapeDtypeStruct((n_streams, N_HEADS, D_HEAD, LANES), F32), st),
        grid=(n_streams, N_HEADS // hb, n_chunks),
        in_specs=in_specs,
        out_specs=(pl.BlockSpec((L, hb * Y_HEAD), lambda s, h, c: (s * n_chunks + c, h)),
                   state_spec, nstate_spec, state_spec),
        scratch_shapes=[pltpu.VMEM((hb, D_HEAD, D_HEAD), F32),
                        pltpu.VMEM((hb, D_HEAD, LANES), F32),
                        pltpu.VMEM((hb, D_HEAD, D_HEAD), F32)],
        compiler_params=_params(),
        name="recurrence",
    )(p, kt, awe, c_row, dec, inter, wsb, gl, ga, gb, c0, n0, s0)


def _mixer_kernel(p_ref, kt_ref, awe_ref, c_ref, dec_ref, inter_ref, wsb_ref, gl_ref, ga_ref, gb_ref,
                  c0_ref, n0_ref, s0_ref, w_ref, x_ref, gf_ref,
                  o_ref, co_ref, no_ref, so_ref,
                  c_scr, n_scr, s_scr, y_scr, *, L):
    tile = pl.program_id(0)
    step = pl.program_id(1)
    head = jnp.minimum(step, N_HEADS - 1)
    d = o_ref.shape[1]

    def mm(x, y):
        return jnp.dot(x, y, preferred_element_type=F32)

    def load_initial_state():
        @pl.when(tile == 0)
        def _():
            c_scr[head] = c0_ref[0, 0]
            n_scr[head] = n0_ref[0, 0]
            s_scr[head] = s0_ref[0, 0]

    def store_final_state(cmat, nrep, smat):
        @pl.when(tile == pl.num_programs(0) - 1)
        def _():
            co_ref[0, 0] = cmat
            no_ref[0, 0] = nrep
            so_ref[0, 0] = smat

    def recurrence_step():
        ones_l = jnp.ones((L, LANES), BF16)
        row = lax.broadcasted_iota(jnp.int32, (L, L), 0)
        col = lax.broadcasted_iota(jnp.int32, (L, L), 1)
        causal = col <= row
        mine = lax.broadcasted_iota(jnp.int32, (L, LANES), 1) == head

        def pick(x):
            return jnp.sum(jnp.where(mine, x, 0.0), axis=1, keepdims=True)

        subs = range(MIXER_SUB)
        rows = [slice(u * L, (u + 1) * L) for u in subs]

        def sec(u, i):
            return p_ref[rows[u], i * D_HEAD:(i + 1) * D_HEAD]

        dec = dec_ref[head]
        inter = _lane_tile(inter_ref[head], D_HEAD)
        wsb = wsb_ref[head].astype(BF16)
        gl = _lane_tile(gl_ref[head], D_HEAD)
        c_all = c_ref[0, head]
        y_half = y_scr.at[step % 2]

        qa = [sec(u, 0) for u in subs]
        va = [sec(u, 1) for u in subs]
        qb = [sec(u, 4) for u in subs]
        vb = [sec(u, 5) for u in subs]
        kat = [kt_ref[0, 0, 0:D_HEAD, rows[u]] for u in subs]
        kbt = [kt_ref[0, 0, D_HEAD:2 * D_HEAD, rows[u]] for u in subs]
        a = [pick(awe_ref[rows[u], 0:LANES]) for u in subs]
        wi = [pick(awe_ref[rows[u], LANES:2 * LANES]) for u in subs]
        e = [pick(awe_ref[rows[u], 2 * LANES:3 * LANES]) for u in subs]
        c = [c_all[:, rows[u]] for u in subs]
        s_bf = [(mm(qa[u], kat[u]) * jnp.exp2(jnp.where(causal, a[u] + c[u], -jnp.inf))).astype(BF16)
                for u in subs]
        sb_bf = [(mm(qb[u], kbt[u]) * dec).astype(BF16) for u in subs]
        kw = [kat[u] * jnp.exp2(a[u][L - 1:L, :] + c[u]).astype(BF16) for u in subs]
        kwb = [kbt[u] * wsb for u in subs]
        sv = [mm(s_bf[u], va[u]) for u in subs]
        rowsum = [mm(s_bf[u], ones_l) for u in subs]
        ob = [mm(sb_bf[u], vb[u]) for u in subs]
        c_upd = [mm(kw[u], va[u]) for u in subs]
        n_upd = [mm(kw[u], ones_l) for u in subs]
        s_upd = [mm(kwb[u], vb[u]) for u in subs]

        cmat = c_scr[head]
        nrep = n_scr[head]
        smat = s_scr[head]
        for u in subs:
            qc = mm(qa[u], cmat.astype(BF16))
            qn = mm(qa[u], nrep.astype(BF16))
            qs = mm(qb[u], smat.astype(BF16))
            den = wi[u] * qn + rowsum[u]
            r = 1.0 / jnp.maximum(jnp.abs(den), e[u])
            h = (wi[u] * qc + sv[u]) * _lane_tile(r, D_HEAD)
            ya = _head_norm(sec(u, 2).astype(F32) * h, ga_ref[head]) * sec(u, 3).astype(F32)
            y_half[rows[u], 0:D_HEAD] = ya.astype(BF16)
            yb = _head_norm(ob[u] + inter * qs, gb_ref[head]) * sec(u, 6).astype(F32)
            y_half[rows[u], D_HEAD:Y_HEAD] = yb.astype(BF16)
            decay = wi[u][L - 1:L, :]
            cmat = decay * cmat + c_upd[u]
            nrep = decay * nrep + n_upd[u]
            smat = gl * smat + s_upd[u]
        c_scr[head] = cmat
        n_scr[head] = nrep
        s_scr[head] = smat
        return cmat, nrep, smat

    def accumulate():
        y = y_scr[(step + 1) % 2]
        tq = d // OUT_COL_SPLIT
        for q in range(OUT_COL_SPLIT):
            o_ref[:, q * tq:(q + 1) * tq] += mm(y, w_ref[:, q * tq:(q + 1) * tq])

    def add_residual():
        tx = d // N_HEADS
        for hh in range(N_HEADS):
            @pl.when(step == hh + 1)
            def _(hh=hh):
                o_ref[:, hh * tx:(hh + 1) * tx] += x_ref[...]

    @pl.when(step == 0)
    def _():
        load_initial_state()
        o_ref[...] = jnp.zeros(o_ref.shape, F32)
        store_final_state(*recurrence_step())

    @pl.when((step > 0) & (step < N_HEADS))
    def _():
        load_initial_state()
        accumulate()
        state = recurrence_step()
        store_final_state(*state)
        add_residual()

    @pl.when(step == N_HEADS)
    def _():
        tx = d // N_HEADS
        lo = (N_HEADS - 1) * tx
        halves = [slice(u * L, (u + 1) * L) for u in range(MIXER_SUB)]
        y = y_scr[(step + 1) % 2]
        zs = [o_ref[r, :] + mm(y[r], w_ref[...]) for r in halves]
        for r, z in zip(halves, zs):
            z = jnp.concatenate([z[:, :lo], z[:, lo:] + x_ref[r, :]], axis=1)
            ms = jnp.mean(z * z, axis=-1, keepdims=True)
            o_ref[r, :] = z * lax.rsqrt(ms + EPS) * gf_ref[...]


def _mixer(p, kt, awe, c_row, tabs, ga, gb, c0, n0, s0, w_bf, x, gf_row, L):
    dec, inter, wsb, gl = tabs
    seq, d = x.shape
    tile_rows = MIXER_SUB * L
    n_tiles = seq // tile_rows

    def resident(arr):
        nd = arr.ndim
        return pl.BlockSpec(arr.shape, lambda t, s: (0,) * nd, pipeline_mode=pl.Buffered(1))

    def rec_head(s):
        return jnp.minimum(s, N_HEADS - 1)

    def acc_head(s):
        return jnp.maximum(s - 1, 0)

    def first_tile(*tail):
        return pl.BlockSpec((1, 1) + tail,
                            lambda t, s: (0, jnp.where(t == 0, rec_head(s), N_HEADS - 1), 0, 0))

    def last_tile(*tail):
        return pl.BlockSpec((1, 1) + tail,
                            lambda t, s: (0, jnp.where(t == n_tiles - 1, rec_head(s), 0), 0, 0))

    in_specs = [pl.BlockSpec((tile_rows, P_HEAD), lambda t, s: (t, rec_head(s))),
                pl.BlockSpec((1, 1, 2 * D_HEAD, tile_rows), lambda t, s: (0, rec_head(s), 0, t)),
                pl.BlockSpec((tile_rows, 3 * LANES), lambda t, s: (t, 0)),
                pl.BlockSpec((1, N_HEADS, 1, tile_rows), lambda t, s: (0, 0, 0, t)),
                resident(dec), resident(inter), resident(wsb), resident(gl), resident(ga), resident(gb),
                first_tile(D_HEAD, D_HEAD), first_tile(D_HEAD, LANES), first_tile(D_HEAD, D_HEAD),
                pl.BlockSpec((Y_HEAD, d), lambda t, s: (acc_head(s), 0)),
                pl.BlockSpec((tile_rows, d // N_HEADS), lambda t, s: (t, acc_head(s))),
                resident(gf_row)]
    st = jax.ShapeDtypeStruct((1, N_HEADS, D_HEAD, D_HEAD), F32)
    return pl.pallas_call(
        functools.partial(_mixer_kernel, L=L),
        out_shape=(jax.ShapeDtypeStruct((seq, d), F32),
                   st, jax.ShapeDtypeStruct((1, N_HEADS, D_HEAD, LANES), F32), st),
        grid=(n_tiles, N_HEADS + 1),
        in_specs=in_specs,
        out_specs=(pl.BlockSpec((tile_rows, d), lambda t, s: (t, 0)),
                   last_tile(D_HEAD, D_HEAD), last_tile(D_HEAD, LANES), last_tile(D_HEAD, D_HEAD)),
        scratch_shapes=[pltpu.VMEM((N_HEADS, D_HEAD, D_HEAD), F32),
                        pltpu.VMEM((N_HEADS, D_HEAD, LANES), F32),
                        pltpu.VMEM((N_HEADS, D_HEAD, D_HEAD), F32),
                        pltpu.VMEM((2, tile_rows, Y_HEAD), BF16)],
        compiler_params=_params(),
        name="mixer",
    )(p, kt, awe, c_row, dec, inter, wsb, gl, ga, gb, c0, n0, s0, w_bf, x, gf_row)


def _out_kernel(y_ref, w_ref, x_ref, g_ref, o_ref, *, n_k, tx):
    k = pl.program_id(1)

    @pl.when(k == 0)
    def _():
        o_ref[...] = jnp.dot(y_ref[...], w_ref[...], preferred_element_type=F32)

    @pl.when(k > 0)
    def _():
        o_ref[...] += jnp.dot(y_ref[...], w_ref[...], preferred_element_type=F32)

    for kk in range(n_k):
        @pl.when(k == kk)
        def _(kk=kk):
            o_ref[:, kk * tx:(kk + 1) * tx] += x_ref[...]

    @pl.when(k == n_k - 1)
    def _():
        z = o_ref[...]
        ms = jnp.mean(z * z, axis=-1, keepdims=True)
        o_ref[...] = z * lax.rsqrt(ms + EPS) * g_ref[...]


def _out_proj(y, w_bf, x, g_row, tm, n_k):
    rows, d = x.shape
    tk = y.shape[1] // n_k
    tx = d // n_k
    return pl.pallas_call(
        functools.partial(_out_kernel, n_k=n_k, tx=tx),
        out_shape=jax.ShapeDtypeStruct((rows, d), F32),
        grid=(rows // tm, n_k),
        in_specs=[pl.BlockSpec((tm, tk), lambda i, k: (i, k)),
                  pl.BlockSpec((tk, d), lambda i, k: (k, 0)),
                  pl.BlockSpec((tm, tx), lambda i, k: (i, k)),
                  pl.BlockSpec((1, d), lambda i, k: (0, 0))],
        out_specs=pl.BlockSpec((tm, d), lambda i, k: (i, 0)),
        compiler_params=_params(),
        name="out_proj",
    )(y, w_bf, x, g_row)


def _rope_tables(base, off):
    inv = ROPE_BASE ** (-jnp.arange(0, D_HEAD, 2, dtype=F32) / D_HEAD)
    ab = base.astype(F32)[:, None] * inv[None, :]
    ao = off.astype(F32)[:, None] * inv[None, :]
    cb, sb = jnp.cos(ab)[:, None, :], jnp.sin(ab)[:, None, :]
    co, so = jnp.cos(ao)[None, :, :], jnp.sin(ao)[None, :, :]
    n = base.shape[0] * off.shape[0]
    return (cb * co - sb * so).reshape(n, HALF), (sb * co + cb * so).reshape(n, HALF)


def _retention_tables(L):
    log_gamma = jnp.log1p(-jnp.power(2.0, -RET_DECAY_BASE - jnp.arange(N_HEADS, dtype=F32)))
    j = jnp.arange(L, dtype=F32)
    diff = j[:, None] - j[None, :]
    dec = jnp.where(diff >= 0, jnp.exp(log_gamma[:, None, None] * jnp.maximum(diff, 0.0)), 0.0)
    inter = jnp.exp(log_gamma[:, None] * (j + 1.0))
    inter = jnp.broadcast_to(inter[:, :, None], (N_HEADS, L, LANES))
    wsb = jnp.exp(log_gamma[:, None] * (L - 1.0 - j))[:, None, :]
    gl = jnp.broadcast_to(jnp.exp(log_gamma * L)[:, None, None], (N_HEADS, 1, LANES))
    return dec, inter, wsb, gl


def _pad_lanes(v):
    return jnp.zeros((1, LANES), F32).at[0, :v.shape[0]].set(v.astype(F32))


def _run_group(p, kt, gates, bi_row, bf_row, tabs, ga, gb, state, n_streams, n_chunks, L, row_off):
    c0, n0, m0, s0 = state
    awe, c_row, m_out = _gate_prep(gates, bi_row, bf_row, m0, n_streams, n_chunks, L, row_off)
    y, c_out, n_out, s_out = _recurrence(p, kt, awe, c_row, tabs, ga, gb, c0, n0, s0,
                                         n_streams, n_chunks, L, row_off)
    return y, (c_out, n_out, m_out, s_out)


def kernel(x_prompt, x_sample, state_mlstm_C, state_mlstm_n, state_mlstm_m, state_ret_S,
           meta_tokens, g_norm1, w_in, b_igate, b_fgate, g_head_a, g_head_b, w_out, g_final):
    depth = w_in.shape[0]
    assert depth == 1 and x_prompt.shape[0] == 1
    d = x_prompt.shape[-1]
    seq = x_prompt.shape[1]
    n_dec, t_dec = x_sample.shape[0], x_sample.shape[1]
    n_meta = meta_tokens.shape[0]
    width = N_HEADS * D_HEAD
    assert seq % MAIN_CHUNK == 0 and MAIN_CHUNK % t_dec == 0 and t_dec % n_meta == 0

    wt = w_in[0].T
    wt_bf = wt.astype(BF16)
    wg_t = jnp.zeros((2 * LANES, d), F32)
    wg_t = wg_t.at[:N_HEADS].set(wt[9 * width:9 * width + N_HEADS])
    wg_t = wg_t.at[LANES:LANES + N_HEADS].set(wt[9 * width + N_HEADS:])
    wg_t = wg_t.astype(BF16)
    w_out_bf = (w_out[0].reshape(2, N_HEADS, D_HEAD, d).transpose(1, 0, 2, 3)
                .reshape(2 * width, d).astype(BF16))
    g1 = g_norm1[0][None, :]
    bi_row = _pad_lanes(b_igate[0])
    bf_row = _pad_lanes(b_fgate[0])
    ga = g_head_a[0].reshape(N_HEADS, 1, D_HEAD)
    gb = g_head_b[0].reshape(N_HEADS, 1, D_HEAD)

    x_main = x_prompt[0]
    n_samp = n_dec * t_dec
    meta_at = seq + n_samp
    meta_pad = -(meta_at + n_meta) % PROJ_ROWS
    x_samp = x_sample.reshape(n_samp, d)
    x_meta = jnp.pad(meta_tokens.astype(F32), ((0, meta_pad), (0, 0)))
    n_rows = meta_at + x_meta.shape[0]
    blk = jnp.arange(n_rows // t_dec)
    base = jnp.where(blk < seq // t_dec, n_meta + t_dec * blk,
                     jnp.where(blk < meta_at // t_dec, n_meta + PAST_LEN, 0))
    cos, sin = _rope_tables(base, jnp.arange(t_dec))

    xn, gates = _prenorm([x_main, x_samp, x_meta], g1, wg_t, NORM_ROWS)
    p, kt = _proj(xn, wt_bf, cos, sin, cos.T, sin.T, PROJ_ROWS)

    zeros_c = jnp.zeros((1, N_HEADS, D_HEAD, D_HEAD), F32)
    zero_state = (zeros_c, jnp.zeros((1, N_HEADS, D_HEAD, LANES), F32), jnp.zeros((1, 1, LANES), F32), zeros_c)
    kt_meta = kt[:, :, meta_at:meta_at + n_meta][None]
    _, st_meta = _run_group(p, kt_meta, gates, bi_row, bf_row, _retention_tables(n_meta),
                            ga, gb, zero_state, 1, 1, n_meta, meta_at // n_meta)

    gf = g_final[None, :]
    c_meta, n_meta_st, m_meta, s_meta = st_meta
    awe_p, c_row_p, m_p = _gate_prep(gates, bi_row, bf_row, m_meta, 1, seq // MAIN_CHUNK, MAIN_CHUNK, 0)
    y_prompt, c_p, n_p, s_p = _mixer(p, kt[None], awe_p, c_row_p, _retention_tables(MAIN_CHUNK), ga, gb,
                                     c_meta, n_meta_st, s_meta, w_out_bf, x_main, gf, MAIN_CHUNK)
    st_p = (c_p, n_p, m_p, s_p)

    kt_samp = kt[:, :, seq:meta_at].reshape(N_HEADS, 2 * D_HEAD, n_dec, t_dec).transpose(2, 0, 1, 3)
    n0 = jnp.broadcast_to(state_mlstm_n[0].astype(F32)[..., None], (n_dec, N_HEADS, D_HEAD, LANES))
    m0 = jnp.zeros((n_dec, 1, LANES), F32).at[:, 0, :N_HEADS].set(state_mlstm_m[0].astype(F32))
    st0 = (state_mlstm_C[0].astype(F32), n0, m0, state_ret_S[0].astype(F32))
    y_s, st_s = _run_group(p, kt_samp, gates, bi_row, bf_row, _retention_tables(t_dec),
                           ga, gb, st0, n_dec, 1, t_dec, seq // t_dec)

    y_prompt = y_prompt[None]
    y_sample = _out_proj(y_s, w_out_bf, x_sample.reshape(n_samp, d), gf, 512, 4)
    y_sample = y_sample.reshape(n_dec, t_dec, d)

    dp = x_prompt.dtype
    ds = state_mlstm_C.dtype

    def states(st, dt):
        c_out, n_out, m_out, s_out = st
        return (c_out[None].astype(dt), n_out[..., 0][None].astype(dt),
                m_out[:, 0, :N_HEADS][None].astype(dt), s_out[None].astype(dt))

    pc, pn, pm, ps = states(st_p, dp)
    sc, sn, sm, ss = states(st_s, ds)
    return (y_prompt, y_sample, pc, pn, pm, ps, sc, sn, sm, ss)
```

```python
import functools

import jax
import jax.numpy as jnp
from jax import lax
from jax.experimental import pallas as pl
from jax.experimental.pallas import tpu as pltpu

F32 = jnp.float32
BF16 = jnp.bfloat16

EPS = 1e-6
LOG2_E = 1.4426950408889634
ROPE_BASE = 10000.0
RET_DECAY_BASE = 5.0
PAST_LEN = 2048

N_HEADS = 8
D_HEAD = 256
HALF = D_HEAD // 2
LANES = 128
MAIN_CHUNK = 256
HEADS_PER_STEP = 4
PREP_CHUNKS = 4
MIXER_SUB = 2
OUT_COL_SPLIT = 4
W_SLOTS = 3
NORM_ROWS = 256
PROJ_ROWS = 768
PROJ_PREFETCHED = 3
P_SECTIONS = 7
P_HEAD = P_SECTIONS * D_HEAD
Y_HEAD = 2 * D_HEAD

VMEM_LIMIT = 56 * 1024 * 1024


def _params(vmem=VMEM_LIMIT):
    return pltpu.CompilerParams(vmem_limit_bytes=vmem)


def _nt_dot(a, b):
    return lax.dot_general(a, b, (((1,), (1,)), ((), ())), preferred_element_type=F32)


def _prenorm_kernel(*refs, starts):
    n_in = len(starts) - 1
    x_refs = refs[:n_in]
    g_ref, wg_ref, xn_ref, gates_ref = refs[n_in:]
    i = pl.program_id(0)
    for k, x_ref in enumerate(x_refs):
        @pl.when((i >= starts[k]) & (i < starts[k + 1]))
        def _(x_ref=x_ref):
            x = x_ref[...]
            ms = jnp.mean(x * x, axis=-1, keepdims=True)
            xn = (x * lax.rsqrt(ms + EPS) * g_ref[...]).astype(BF16)
            xn_ref[...] = xn
            gates_ref[...] = _nt_dot(xn, wg_ref[...])


def _prenorm(xs, g_row, wg_t, tm):
    d = xs[0].shape[1]
    starts = [0]
    for x in xs:
        assert x.shape[0] % tm == 0
        starts.append(starts[-1] + x.shape[0] // tm)

    def x_spec(k):
        return pl.BlockSpec((tm, d), lambda i: (jnp.clip(i - starts[k], 0, starts[k + 1] - starts[k] - 1), 0))

    rows = starts[-1] * tm
    return pl.pallas_call(
        functools.partial(_prenorm_kernel, starts=tuple(starts)),
        out_shape=(jax.ShapeDtypeStruct((rows, d), BF16),
                   jax.ShapeDtypeStruct((rows, 2 * LANES), F32)),
        grid=(starts[-1],),
        in_specs=[x_spec(k) for k in range(len(xs))] + [
            pl.BlockSpec((1, d), lambda i: (0, 0)),
            pl.BlockSpec((2 * LANES, d), lambda i: (0, 0))],
        out_specs=(pl.BlockSpec((tm, d), lambda i: (i, 0)),
                   pl.BlockSpec((tm, 2 * LANES), lambda i: (i, 0))),
        compiler_params=_params(),
        name="prenorm",
    )(*xs, g_row, wg_t)


def _silu(z):
    return z * jax.nn.sigmoid(z)


def _proj_kernel(xn_ref, wqa, wva, woa, wza, wqb, wvb, wzb, wka, wkb,
                 cos_ref, sin_ref, cos_t_ref, sin_t_ref, p_ref, kt_ref):
    xn = xn_ref[...]

    def tok(w_ref):
        return _nt_dot(xn, w_ref[...])

    def put(sec, val):
        p_ref[:, sec * D_HEAD:(sec + 1) * D_HEAD] = val.astype(BF16)

    put(0, tok(wqa))
    put(1, tok(wva))
    put(2, jax.nn.sigmoid(tok(woa)))
    put(3, _silu(tok(wza)))
    q = tok(wqb)
    c = cos_ref[...]
    s = sin_ref[...]
    x1 = q[:, :HALF]
    x2 = q[:, HALF:]
    p_ref[:, 4 * D_HEAD:4 * D_HEAD + HALF] = (x1 * c - x2 * s).astype(BF16)
    p_ref[:, 4 * D_HEAD + HALF:5 * D_HEAD] = (x1 * s + x2 * c).astype(BF16)
    put(5, tok(wvb))
    put(6, _silu(tok(wzb)))

    scale = D_HEAD ** -0.5
    kt_ref[0, 0:D_HEAD, :] = (_nt_dot(wka[...], xn) * scale).astype(BF16)
    kb = _nt_dot(wkb[...], xn)
    ct = cos_t_ref[...] * scale
    st = sin_t_ref[...] * scale
    k1 = kb[:HALF]
    k2 = kb[HALF:]
    kt_ref[0, D_HEAD:D_HEAD + HALF, :] = (k1 * ct - k2 * st).astype(BF16)
    kt_ref[0, D_HEAD + HALF:2 * D_HEAD, :] = (k1 * st + k2 * ct).astype(BF16)


def _proj(xn, wt_bf, cos, sin, cos_t, sin_t, tm):
    rows, d = xn.shape
    def w_spec(wsec, buffers):
        return pl.BlockSpec((D_HEAD, d), lambda h, m: (wsec * N_HEADS + h, 0),
                            pipeline_mode=pl.Buffered(buffers))

    order = (0, 2, 3, 4, 5, 7, 8, 1, 6)
    w_specs = [w_spec(wsec, 2 if i < PROJ_PREFETCHED else 1) for i, wsec in enumerate(order)]
    tab = pl.BlockSpec((tm, HALF), lambda h, m: (m, 0))
    tab_t = pl.BlockSpec((HALF, tm), lambda h, m: (0, m))
    return pl.pallas_call(
        _proj_kernel,
        out_shape=(jax.ShapeDtypeStruct((rows, N_HEADS * P_HEAD), BF16),
                   jax.ShapeDtypeStruct((N_HEADS, 2 * D_HEAD, rows), BF16)),
        grid=(N_HEADS, rows // tm),
        in_specs=[pl.BlockSpec((tm, d), lambda h, m: (m, 0))] + w_specs + [tab, tab, tab_t, tab_t],
        out_specs=(pl.BlockSpec((tm, P_HEAD), lambda h, m: (m, h)),
                   pl.BlockSpec((1, 2 * D_HEAD, tm), lambda h, m: (h, 0, m))),
        compiler_params=_params(),
        name="proj",
    )(xn, wt_bf, wt_bf, wt_bf, wt_bf, wt_bf, wt_bf, wt_bf, wt_bf, wt_bf, cos, sin, cos_t, sin_t)


def _gate_prep_kernel(g_ref, bi_ref, bf_ref, m0_ref, awe_ref, crow_ref, mo_ref, m_scr, *, L):
    @pl.when(pl.program_id(1) == 0)
    def _():
        m_scr[...] = m0_ref[0]

    rows = g_ref.shape[0]
    g = g_ref[...]
    ig = g[:, :LANES] + bi_ref[...]
    z = g[:, LANES:] + bf_ref[...]
    lf = jnp.minimum(z, 0.0) - jnp.log1p(jnp.exp(-jnp.abs(z)))
    assert L & (L - 1) == 0
    pos = lax.broadcasted_iota(jnp.int32, (rows, LANES), 0) & (L - 1)
    b = lf
    d = 1
    while d < L:
        b = b + jnp.where(pos >= d, pltpu.roll(b, d, 0), 0.0)
        d *= 2
    c = ig - b
    cm = c
    d = 1
    while d < L:
        cm = jnp.maximum(cm, jnp.where(pos >= d, pltpu.roll(cm, d, 0), -jnp.inf))
        d *= 2
    m_prev = m_scr[...]
    for k in range(rows // L):
        sl = slice(k * L, (k + 1) * L)
        mm = jnp.maximum(m_prev, cm[sl])
        m_t = b[sl] + mm
        awe_ref[sl, 0:LANES] = -mm * LOG2_E
        awe_ref[sl, LANES:2 * LANES] = jnp.exp(m_prev - mm)
        awe_ref[sl, 2 * LANES:3 * LANES] = jnp.exp(-m_t)
        m_prev = m_t[L - 1:L, :]
    c2 = c * LOG2_E
    if rows < LANES:
        c_sq = jnp.concatenate([c2, jnp.zeros((LANES - rows, LANES), F32)], axis=0)
    else:
        c_sq = c2
    c_t = c_sq.T
    for hh in range(N_HEADS):
        crow_ref[0, hh] = c_t[hh:hh + 1, :rows]
    m_scr[...] = m_prev
    mo_ref[0] = m_prev


def _gate_prep(gates, bi_row, bf_row, m0, n_streams, n_chunks, L, row_off):
    rows_out = n_streams * n_chunks * L
    per_step = PREP_CHUNKS if n_chunks % PREP_CHUNKS == 0 else 1
    n_steps = n_chunks // per_step
    rb = per_step * L
    assert (row_off * L) % rb == 0
    blk_off = row_off * L // rb
    return pl.pallas_call(
        functools.partial(_gate_prep_kernel, L=L),
        out_shape=(jax.ShapeDtypeStruct((rows_out, 3 * LANES), F32),
                   jax.ShapeDtypeStruct((n_streams, N_HEADS, 1, n_chunks * L), F32),
                   jax.ShapeDtypeStruct((n_streams, 1, LANES), F32)),
        grid=(n_streams, n_steps),
        in_specs=[pl.BlockSpec((rb, 2 * LANES), lambda s, c: (blk_off + s * n_steps + c, 0)),
                  pl.BlockSpec((1, LANES), lambda s, c: (0, 0)),
                  pl.BlockSpec((1, LANES), lambda s, c: (0, 0)),
                  pl.BlockSpec((1, 1, LANES), lambda s, c: (s, 0, 0))],
        out_specs=(pl.BlockSpec((rb, 3 * LANES), lambda s, c: (s * n_steps + c, 0)),
                   pl.BlockSpec((1, N_HEADS, 1, rb), lambda s, c: (s, 0, 0, c)),
                   pl.BlockSpec((1, 1, LANES), lambda s, c: (s, 0, 0))),
        scratch_shapes=[pltpu.VMEM((1, LANES), F32)],
        compiler_params=_params(),
        name="gate_prep",
    )(gates, bi_row, bf_row, m0)


def _lane_tile(x, n):
    if n <= LANES:
        return x[:, :n]
    return jnp.concatenate([x] * (n // LANES), axis=1)


def _head_norm(h, g_row):
    ms = jnp.mean(h * h, axis=-1, keepdims=True)
    return h * lax.rsqrt(ms + EPS) * g_row


def _rec_kernel(p_ref, kt_ref, awe_ref, c_ref, dec_ref, inter_ref, wsb_ref, gl_ref, ga_ref, gb_ref,
                c0_ref, n0_ref, s0_ref,
                y_ref, co_ref, no_ref, so_ref,
                c_scr, n_scr, s_scr, *, L):
    ci = pl.program_id(2)

    @pl.when(ci == 0)
    def _():
        c_scr[...] = c0_ref[0]
        n_scr[...] = n0_ref[0]
        s_scr[...] = s0_ref[0]

    ones_l = jnp.ones((L, LANES), BF16)
    row = lax.broadcasted_iota(jnp.int32, (L, L), 0)
    col = lax.broadcasted_iota(jnp.int32, (L, L), 1)
    causal = col <= row
    lane = lax.broadcasted_iota(jnp.int32, (L, LANES), 1)
    last = ci == pl.num_programs(2) - 1

    hb = range(HEADS_PER_STEP)

    def sec(j, i):
        lo = j * P_HEAD + i * D_HEAD
        return p_ref[:, lo:lo + D_HEAD]

    def pick(x, j):
        mine = lane == pl.program_id(1) * HEADS_PER_STEP + j
        return jnp.sum(jnp.where(mine, x, 0.0), axis=1, keepdims=True)

    def mm(x, y):
        return jnp.dot(x, y, preferred_element_type=F32)

    qa = [sec(j, 0) for j in hb]
    va = [sec(j, 1) for j in hb]
    qb = [sec(j, 4) for j in hb]
    vb = [sec(j, 5) for j in hb]
    kat = [kt_ref[0, j, 0:D_HEAD, :] for j in hb]
    kbt = [kt_ref[0, j, D_HEAD:2 * D_HEAD, :] for j in hb]
    cmat = [c_scr[j] for j in hb]
    nrep = [n_scr[j] for j in hb]
    smat = [s_scr[j] for j in hb]

    s_raw = [mm(qa[j], kat[j]) for j in hb]
    sb_raw = [mm(qb[j], kbt[j]) for j in hb]
    qc = [mm(qa[j], cmat[j].astype(BF16)) for j in hb]
    qn = [mm(qa[j], nrep[j].astype(BF16)) for j in hb]
    qs = [mm(qb[j], smat[j].astype(BF16)) for j in hb]

    a = [pick(awe_ref[:, 0:LANES], j) for j in hb]
    wi = [pick(awe_ref[:, LANES:2 * LANES], j) for j in hb]
    e = [pick(awe_ref[:, 2 * LANES:3 * LANES], j) for j in hb]
    c = [c_ref[0, j] for j in hb]
    s_bf = [(s_raw[j] * jnp.exp2(jnp.where(causal, a[j] + c[j], -jnp.inf))).astype(BF16) for j in hb]
    sb_bf = [(sb_raw[j] * dec_ref[j]).astype(BF16) for j in hb]

    w_state = [jnp.exp2(a[j][L - 1:L, :] + c[j]) for j in hb]
    kw = [kat[j] * w_state[j].astype(BF16) for j in hb]
    kwb = [kbt[j] * wsb_ref[j].astype(BF16) for j in hb]

    sv = [mm(s_bf[j], va[j]) for j in hb]
    rowsum = [mm(s_bf[j], ones_l) for j in hb]
    ob = [mm(sb_bf[j], vb[j]) for j in hb]
    c_upd = [mm(kw[j], va[j]) for j in hb]
    n_upd = [mm(kw[j], ones_l) for j in hb]
    s_upd = [mm(kwb[j], vb[j]) for j in hb]

    for j in hb:
        den = wi[j] * qn[j] + rowsum[j]
        r = 1.0 / jnp.maximum(jnp.abs(den), e[j])
        h = (wi[j] * qc[j] + sv[j]) * _lane_tile(r, D_HEAD)
        ya = _head_norm(sec(j, 2).astype(F32) * h, ga_ref[j]) * sec(j, 3).astype(F32)
        y_ref[:, j * Y_HEAD:j * Y_HEAD + D_HEAD] = ya.astype(BF16)
        o = ob[j] + _lane_tile(inter_ref[j], D_HEAD) * qs[j]
        yb = _head_norm(o, gb_ref[j]) * sec(j, 6).astype(F32)
        y_ref[:, j * Y_HEAD + D_HEAD:(j + 1) * Y_HEAD] = yb.astype(BF16)

    for j in hb:
        decay = wi[j][L - 1:L, :]
        c_new = decay * cmat[j] + c_upd[j]
        n_new = decay * nrep[j] + n_upd[j]
        s_new = _lane_tile(gl_ref[j], D_HEAD) * smat[j] + s_upd[j]
        c_scr[j] = c_new
        n_scr[j] = n_new
        s_scr[j] = s_new

        @pl.when(last)
        def _(j=j, c_new=c_new, n_new=n_new, s_new=s_new):
            co_ref[0, j] = c_new
            no_ref[0, j] = n_new
            so_ref[0, j] = s_new


def _recurrence(p, kt, awe, c_row, tabs, ga, gb, c0, n0, s0, n_streams, n_chunks, L, row_off):
    dec, inter, wsb, gl = tabs
    hb = HEADS_PER_STEP
    y_rows = n_streams * n_chunks * L

    def per_head(*tail):
        return pl.BlockSpec((hb,) + tail, lambda s, h, c: (h,) + (0,) * len(tail))

    state_spec = pl.BlockSpec((1, hb, D_HEAD, D_HEAD), lambda s, h, c: (s, h, 0, 0))
    nstate_spec = pl.BlockSpec((1, hb, D_HEAD, LANES), lambda s, h, c: (s, h, 0, 0))
    in_specs = [pl.BlockSpec((L, hb * P_HEAD), lambda s, h, c: (row_off + s * n_chunks + c, h)),
                pl.BlockSpec((1, hb, 2 * D_HEAD, L), lambda s, h, c: (s, h, 0, c)),
                pl.BlockSpec((L, 3 * LANES), lambda s, h, c: (s * n_chunks + c, 0)),
                pl.BlockSpec((1, hb, 1, L), lambda s, h, c: (s, h, 0, c)),
                per_head(L, L), per_head(L, LANES), per_head(1, L), per_head(1, LANES),
                per_head(1, D_HEAD), per_head(1, D_HEAD),
                state_spec, nstate_spec, state_spec]
    st = jax.ShapeDtypeStruct((n_streams, N_HEADS, D_HEAD, D_HEAD), F32)
    return pl.pallas_call(
        functools.partial(_rec_kernel, L=L),
        out_shape=(jax.ShapeDtypeStruct((y_rows, N_HEADS * Y_HEAD), BF16),
                   st, jax.ShapeDtypeStruct((n_streams, N_HEADS, D_HEAD, LANES), F32), st),
        grid=(n_streams, N_HEADS // hb, n_chunks),
        in_specs=in_specs,
        out_specs=(pl.BlockSpec((L, hb * Y_HEAD), lambda s, h, c: (s * n_chunks + c, h)),
                   state_spec, nstate_spec, state_spec),
        scratch_shapes=[pltpu.VMEM((hb, D_HEAD, D_HEAD), F32),
                        pltpu.VMEM((hb, D_HEAD, LANES), F32),
                        pltpu.VMEM((hb, D_HEAD, D_HEAD), F32)],
        compiler_params=_params(),
        name="recurrence",
    )(p, kt, awe, c_row, dec, inter, wsb, gl, ga, gb, c0, n0, s0)


def _mixer_kernel(p_ref, kt_ref, awe_ref, c_ref, dec_ref, inter_ref, wsb_ref, gl_ref, ga_ref, gb_ref,
                  c0_ref, n0_ref, s0_ref, w_hbm, x_ref, gf_ref,
                  o_ref, co_ref, no_ref, so_ref,
                  c_scr, n_scr, s_scr, y_scr, w_ring, w_sem, *, L):
    tile = pl.program_id(0)
    step = pl.program_id(1)
    head = jnp.minimum(step, N_HEADS - 1)
    d = o_ref.shape[1]

    def mm(x, y):
        return jnp.dot(x, y, preferred_element_type=F32)

    def load_initial_state():
        @pl.when(tile == 0)
        def _():
            c_scr[head] = c0_ref[0, 0]
            n_scr[head] = n0_ref[0, 0]
            s_scr[head] = s0_ref[0, 0]

    def store_final_state(cmat, nrep, smat):
        @pl.when(tile == pl.num_programs(0) - 1)
        def _():
            co_ref[0, 0] = cmat
            no_ref[0, 0] = nrep
            so_ref[0, 0] = smat

    def recurrence_step():
        ones_l = jnp.ones((L, LANES), BF16)
        row = lax.broadcasted_iota(jnp.int32, (L, L), 0)
        col = lax.broadcasted_iota(jnp.int32, (L, L), 1)
        causal = col <= row
        mine = lax.broadcasted_iota(jnp.int32, (L, LANES), 1) == head

        def pick(x):
            return jnp.sum(jnp.where(mine, x, 0.0), axis=1, keepdims=True)

        subs = range(MIXER_SUB)
        rows = [slice(u * L, (u + 1) * L) for u in subs]

        def sec(u, i):
            return p_ref[rows[u], i * D_HEAD:(i + 1) * D_HEAD]

        dec = dec_ref[head]
        inter = _lane_tile(inter_ref[head], D_HEAD)
        wsb = wsb_ref[head].astype(BF16)
        gl = _lane_tile(gl_ref[head], D_HEAD)
        c_all = c_ref[0, head]
        y_half = y_scr.at[step % 2]

        qa = [sec(u, 0) for u in subs]
        va = [sec(u, 1) for u in subs]
        qb = [sec(u, 4) for u in subs]
        vb = [sec(u, 5) for u in subs]
        kat = [kt_ref[0, 0, 0:D_HEAD, rows[u]] for u in subs]
        kbt = [kt_ref[0, 0, D_HEAD:2 * D_HEAD, rows[u]] for u in subs]
        a = [pick(awe_ref[rows[u], 0:LANES]) for u in subs]
        wi = [pick(awe_ref[rows[u], LANES:2 * LANES]) for u in subs]
        e = [pick(awe_ref[rows[u], 2 * LANES:3 * LANES]) for u in subs]
        c = [c_all[:, rows[u]] for u in subs]
        s_bf = [(mm(qa[u], kat[u]) * jnp.exp2(jnp.where(causal, a[u] + c[u], -jnp.inf))).astype(BF16)
                for u in subs]
        sb_bf = [(mm(qb[u], kbt[u]) * dec).astype(BF16) for u in subs]
        kw = [kat[u] * jnp.exp2(a[u][L - 1:L, :] + c[u]).astype(BF16) for u in subs]
        kwb = [kbt[u] * wsb for u in subs]
        sv = [mm(s_bf[u], va[u]) for u in subs]
        rowsum = [mm(s_bf[u], ones_l) for u in subs]
        ob = [mm(sb_bf[u], vb[u]) for u in subs]
        c_upd = [mm(kw[u], va[u]) for u in subs]
        n_upd = [mm(kw[u], ones_l) for u in subs]
        s_upd = [mm(kwb[u], vb[u]) for u in subs]

        cmat = c_scr[head]
        nrep = n_scr[head]
        smat = s_scr[head]
        for u in subs:
            qc = mm(qa[u], cmat.astype(BF16))
            qn = mm(qa[u], nrep.astype(BF16))
            qs = mm(qb[u], smat.astype(BF16))
            den = wi[u] * qn + rowsum[u]
            r = 1.0 / jnp.maximum(jnp.abs(den), e[u])
            h = (wi[u] * qc + sv[u]) * _lane_tile(r, D_HEAD)
            ya = _head_norm(sec(u, 2).astype(F32) * h, ga_ref[head]) * sec(u, 3).astype(F32)
            y_half[rows[u], 0:D_HEAD] = ya.astype(BF16)
            yb = _head_norm(ob[u] + inter * qs, gb_ref[head]) * sec(u, 6).astype(F32)
            y_half[rows[u], D_HEAD:Y_HEAD] = yb.astype(BF16)
            decay = wi[u][L - 1:L, :]
            cmat = decay * cmat + c_upd[u]
            nrep = decay * nrep + n_upd[u]
            smat = gl * smat + s_upd[u]
        c_scr[head] = cmat
        n_scr[head] = nrep
        s_scr[head] = smat
        return cmat, nrep, smat

    n_events = pl.num_programs(0) * N_HEADS
    event = tile * N_HEADS + step - 1

    def w_copy(k):
        rows_k = pl.ds((k % N_HEADS) * Y_HEAD, Y_HEAD)
        slot = k % W_SLOTS
        return pltpu.make_async_copy(w_hbm.at[rows_k, :], w_ring.at[slot], w_sem.at[slot])

    def w_advance():
        w_copy(event).wait()

        @pl.when(event + W_SLOTS - 1 < n_events)
        def _():
            w_copy(event + W_SLOTS - 1).start()

        return w_ring.at[event % W_SLOTS]

    def accumulate(w_ref):
        y = y_scr[(step + 1) % 2]
        tq = d // OUT_COL_SPLIT
        for q in range(OUT_COL_SPLIT):
            o_ref[:, q * tq:(q + 1) * tq] += mm(y, w_ref[:, q * tq:(q + 1) * tq])

    def add_residual():
        tx = d // N_HEADS
        for hh in range(N_HEADS):
            @pl.when(step == hh + 1)
            def _(hh=hh):
                o_ref[:, hh * tx:(hh + 1) * tx] += x_ref[...]

    @pl.when(step == 0)
    def _():
        @pl.when(tile == 0)
        def _():
            for k in range(W_SLOTS - 1):
                w_copy(k).start()

        load_initial_state()
        o_ref[...] = jnp.zeros(o_ref.shape, F32)
        store_final_state(*recurrence_step())

    @pl.when((step > 0) & (step < N_HEADS))
    def _():
        w_ref = w_advance()
        load_initial_state()
        accumulate(w_ref)
        state = recurrence_step()
        store_final_state(*state)
        add_residual()

    @pl.when(step == N_HEADS)
    def _():
        w_ref = w_advance()
        tx = d // N_HEADS
        lo = (N_HEADS - 1) * tx
        halves = [slice(u * L, (u + 1) * L) for u in range(MIXER_SUB)]
        y = y_scr[(step + 1) % 2]
        zs = [o_ref[r, :] + mm(y[r], w_ref[...]) for r in halves]
        for r, z in zip(halves, zs):
            z = jnp.concatenate([z[:, :lo], z[:, lo:] + x_ref[r, :]], axis=1)
            ms = jnp.mean(z * z, axis=-1, keepdims=True)
            o_ref[r, :] = z * lax.rsqrt(ms + EPS) * gf_ref[...]


def _mixer(p, kt, awe, c_row, tabs, ga, gb, c0, n0, s0, w_bf, x, gf_row, L):
    dec, inter, wsb, gl = tabs
    seq, d = x.shape
    tile_rows = MIXER_SUB * L
    n_tiles = seq // tile_rows

    def resident(arr):
        nd = arr.ndim
        return pl.BlockSpec(arr.shape, lambda t, s: (0,) * nd, pipeline_mode=pl.Buffered(1))

    def rec_head(s):
        return jnp.minimum(s, N_HEADS - 1)

    def acc_head(s):
        return jnp.maximum(s - 1, 0)

    def first_tile(*tail):
        return pl.BlockSpec((1, 1) + tail,
                            lambda t, s: (0, jnp.where(t == 0, rec_head(s), N_HEADS - 1), 0, 0))

    def last_tile(*tail):
        return pl.BlockSpec((1, 1) + tail,
                            lambda t, s: (0, jnp.where(t == n_tiles - 1, rec_head(s), 0), 0, 0))

    in_specs = [pl.BlockSpec((tile_rows, P_HEAD), lambda t, s: (t, rec_head(s))),
                pl.BlockSpec((1, 1, 2 * D_HEAD, tile_rows), lambda t, s: (0, rec_head(s), 0, t)),
                pl.BlockSpec((tile_rows, 3 * LANES), lambda t, s: (t, 0)),
                pl.BlockSpec((1, N_HEADS, 1, tile_rows), lambda t, s: (0, 0, 0, t)),
                resident(dec), resident(inter), resident(wsb), resident(gl), resident(ga), resident(gb),
                first_tile(D_HEAD, D_HEAD), first_tile(D_HEAD, LANES), first_tile(D_HEAD, D_HEAD),
                pl.BlockSpec(memory_space=pl.ANY),
                pl.BlockSpec((tile_rows, d // N_HEADS), lambda t, s: (t, acc_head(s))),
                resident(gf_row)]
    st = jax.ShapeDtypeStruct((1, N_HEADS, D_HEAD, D_HEAD), F32)
    return pl.pallas_call(
        functools.partial(_mixer_kernel, L=L),
        out_shape=(jax.ShapeDtypeStruct((seq, d), F32),
                   st, jax.ShapeDtypeStruct((1, N_HEADS, D_HEAD, LANES), F32), st),
        grid=(n_tiles, N_HEADS + 1),
        in_specs=in_specs,
        out_specs=(pl.BlockSpec((tile_rows, d), lambda t, s: (t, 0)),
                   last_tile(D_HEAD, D_HEAD), last_tile(D_HEAD, LANES), last_tile(D_HEAD, D_HEAD)),
        scratch_shapes=[pltpu.VMEM((N_HEADS, D_HEAD, D_HEAD), F32),
                        pltpu.VMEM((N_HEADS, D_HEAD, LANES), F32),
                        pltpu.VMEM((N_HEADS, D_HEAD, D_HEAD), F32),
                        pltpu.VMEM((2, tile_rows, Y_HEAD), BF16),
                        pltpu.VMEM((W_SLOTS, Y_HEAD, d), BF16),
                        pltpu.SemaphoreType.DMA((W_SLOTS,))],
        compiler_params=pltpu.CompilerParams(dimension_semantics=("arbitrary", "arbitrary"),
                                             vmem_limit_bytes=VMEM_LIMIT),
        name="mixer",
    )(p, kt, awe, c_row, dec, inter, wsb, gl, ga, gb, c0, n0, s0, w_bf, x, gf_row)


def _out_kernel(y_ref, w_ref, x_ref, g_ref, o_ref, *, n_k, tx):
    k = pl.program_id(1)

    @pl.when(k == 0)
    def _():
        o_ref[...] = jnp.dot(y_ref[...], w_ref[...], preferred_element_type=F32)

    @pl.when(k > 0)
    def _():
        o_ref[...] += jnp.dot(y_ref[...], w_ref[...], preferred_element_type=F32)

    for kk in range(n_k):
        @pl.when(k == kk)
        def _(kk=kk):
            o_ref[:, kk * tx:(kk + 1) * tx] += x_ref[...]

    @pl.when(k == n_k - 1)
    def _():
        z = o_ref[...]
        ms = jnp.mean(z * z, axis=-1, keepdims=True)
        o_ref[...] = z * lax.rsqrt(ms + EPS) * g_ref[...]


def _out_proj(y, w_bf, x, g_row, tm, n_k):
    rows, d = x.shape
    tk = y.shape[1] // n_k
    tx = d // n_k
    return pl.pallas_call(
        functools.partial(_out_kernel, n_k=n_k, tx=tx),
        out_shape=jax.ShapeDtypeStruct((rows, d), F32),
        grid=(rows // tm, n_k),
        in_specs=[pl.BlockSpec((tm, tk), lambda i, k: (i, k)),
                  pl.BlockSpec((tk, d), lambda i, k: (k, 0)),
                  pl.BlockSpec((tm, tx), lambda i, k: (i, k)),
                  pl.BlockSpec((1, d), lambda i, k: (0, 0))],
        out_specs=pl.BlockSpec((tm, d), lambda i, k: (i, 0)),
        compiler_params=_params(),
        name="out_proj",
    )(y, w_bf, x, g_row)


def _rope_tables(base, off):
    inv = ROPE_BASE ** (-jnp.arange(0, D_HEAD, 2, dtype=F32) / D_HEAD)
    ab = base.astype(F32)[:, None] * inv[None, :]
    ao = off.astype(F32)[:, None] * inv[None, :]
    cb, sb = jnp.cos(ab)[:, None, :], jnp.sin(ab)[:, None, :]
    co, so = jnp.cos(ao)[None, :, :], jnp.sin(ao)[None, :, :]
    n = base.shape[0] * off.shape[0]
    return (cb * co - sb * so).reshape(n, HALF), (sb * co + cb * so).reshape(n, HALF)


def _retention_tables(L):
    log_gamma = jnp.log1p(-jnp.power(2.0, -RET_DECAY_BASE - jnp.arange(N_HEADS, dtype=F32)))
    j = jnp.arange(L, dtype=F32)
    diff = j[:, None] - j[None, :]
    dec = jnp.where(diff >= 0, jnp.exp(log_gamma[:, None, None] * jnp.maximum(diff, 0.0)), 0.0)
    inter = jnp.exp(log_gamma[:, None] * (j + 1.0))
    inter = jnp.broadcast_to(inter[:, :, None], (N_HEADS, L, LANES))
    wsb = jnp.exp(log_gamma[:, None] * (L - 1.0 - j))[:, None, :]
    gl = jnp.broadcast_to(jnp.exp(log_gamma * L)[:, None, None], (N_HEADS, 1, LANES))
    return dec, inter, wsb, gl


def _pad_lanes(v):
    return jnp.zeros((1, LANES), F32).at[0, :v.shape[0]].set(v.astype(F32))


def _run_group(p, kt, gates, bi_row, bf_row, tabs, ga, gb, state, n_streams, n_chunks, L, row_off):
    c0, n0, m0, s0 = state
    awe, c_row, m_out = _gate_prep(gates, bi_row, bf_row, m0, n_streams, n_chunks, L, row_off)
    y, c_out, n_out, s_out = _recurrence(p, kt, awe, c_row, tabs, ga, gb, c0, n0, s0,
                                         n_streams, n_chunks, L, row_off)
    return y, (c_out, n_out, m_out, s_out)


def kernel(x_prompt, x_sample, state_mlstm_C, state_mlstm_n, state_mlstm_m, state_ret_S,
           meta_tokens, g_norm1, w_in, b_igate, b_fgate, g_head_a, g_head_b, w_out, g_final):
    depth = w_in.shape[0]
    assert depth == 1 and x_prompt.shape[0] == 1
    d = x_prompt.shape[-1]
    seq = x_prompt.shape[1]
    n_dec, t_dec = x_sample.shape[0], x_sample.shape[1]
    n_meta = meta_tokens.shape[0]
    width = N_HEADS * D_HEAD
    assert seq % MAIN_CHUNK == 0 and MAIN_CHUNK % t_dec == 0 and t_dec % n_meta == 0

    wt = w_in[0].T
    wt_bf = wt.astype(BF16)
    wg_t = jnp.zeros((2 * LANES, d), F32)
    wg_t = wg_t.at[:N_HEADS].set(wt[9 * width:9 * width + N_HEADS])
    wg_t = wg_t.at[LANES:LANES + N_HEADS].set(wt[9 * width + N_HEADS:])
    wg_t = wg_t.astype(BF16)
    w_out_bf = (w_out[0].reshape(2, N_HEADS, D_HEAD, d).transpose(1, 0, 2, 3)
                .reshape(2 * width, d).astype(BF16))
    g1 = g_norm1[0][None, :]
    bi_row = _pad_lanes(b_igate[0])
    bf_row = _pad_lanes(b_fgate[0])
    ga = g_head_a[0].reshape(N_HEADS, 1, D_HEAD)
    gb = g_head_b[0].reshape(N_HEADS, 1, D_HEAD)

    x_main = x_prompt[0]
    n_samp = n_dec * t_dec
    meta_at = seq + n_samp
    meta_pad = -(meta_at + n_meta) % PROJ_ROWS
    x_samp = x_sample.reshape(n_samp, d)
    x_meta = jnp.pad(meta_tokens.astype(F32), ((0, meta_pad), (0, 0)))
    n_rows = meta_at + x_meta.shape[0]
    blk = jnp.arange(n_rows // t_dec)
    base = jnp.where(blk < seq // t_dec, n_meta + t_dec * blk,
                     jnp.where(blk < meta_at // t_dec, n_meta + PAST_LEN, 0))
    cos, sin = _rope_tables(base, jnp.arange(t_dec))

    xn, gates = _prenorm([x_main, x_samp, x_meta], g1, wg_t, NORM_ROWS)
    p, kt = _proj(xn, wt_bf, cos, sin, cos.T, sin.T, PROJ_ROWS)

    zeros_c = jnp.zeros((1, N_HEADS, D_HEAD, D_HEAD), F32)
    zero_state = (zeros_c, jnp.zeros((1, N_HEADS, D_HEAD, LANES), F32), jnp.zeros((1, 1, LANES), F32), zeros_c)
    kt_meta = kt[:, :, meta_at:meta_at + n_meta][None]
    _, st_meta = _run_group(p, kt_meta, gates, bi_row, bf_row, _retention_tables(n_meta),
                            ga, gb, zero_state, 1, 1, n_meta, meta_at // n_meta)

    gf = g_final[None, :]
    c_meta, n_meta_st, m_meta, s_meta = st_meta
    awe_p, c_row_p, m_p = _gate_prep(gates, bi_row, bf_row, m_meta, 1, seq // MAIN_CHUNK, MAIN_CHUNK, 0)
    y_prompt, c_p, n_p, s_p = _mixer(p, kt[None], awe_p, c_row_p, _retention_tables(MAIN_CHUNK), ga, gb,
                                     c_meta, n_meta_st, s_meta, w_out_bf, x_main, gf, MAIN_CHUNK)
    st_p = (c_p, n_p, m_p, s_p)

    kt_samp = kt[:, :, seq:meta_at].reshape(N_HEADS, 2 * D_HEAD, n_dec, t_dec).transpose(2, 0, 1, 3)
    n0 = jnp.broadcast_to(state_mlstm_n[0].astype(F32)[..., None], (n_dec, N_HEADS, D_HEAD, LANES))
    m0 = jnp.zeros((n_dec, 1, LANES), F32).at[:, 0, :N_HEADS].set(state_mlstm_m[0].astype(F32))
    st0 = (state_mlstm_C[0].astype(F32), n0, m0, state_ret_S[0].astype(F32))
    y_s, st_s = _run_group(p, kt_samp, gates, bi_row, bf_row, _retention_tables(t_dec),
                           ga, gb, st0, n_dec, 1, t_dec, seq // t_dec)

    y_prompt = y_prompt[None]
    y_sample = _out_proj(y_s, w_out_bf, x_sample.reshape(n_samp, d), gf, 512, 4)
    y_sample = y_sample.reshape(n_dec, t_dec, d)

    dp = x_prompt.dtype
    ds = state_mlstm_C.dtype

    def states(st, dt):
        c_out, n_out, m_out, s_out = st
        return (c_out[None].astype(dt), n_out[..., 0][None].astype(dt),
                m_out[:, 0, :N_HEADS][None].astype(dt), s_out[None].astype(dt))

    pc, pn, pm, ps = states(st_p, dp)
    sc, sn, sm, ss = states(st_s, ds)
    return (y_prompt, y_sample, pc, pn, pm, ps, sc, sn, sm, ss)
```

```python
import functools

import jax
import jax.numpy as jnp
from jax import lax
from jax.experimental import pallas as pl
from jax.experimental.pallas import tpu as pltpu

F32 = jnp.float32
BF16 = jnp.bfloat16

EPS = 1e-6
LOG2_E = 1.4426950408889634
ROPE_BASE = 10000.0
RET_DECAY_BASE = 5.0
PAST_LEN = 2048

N_HEADS = 8
D_HEAD = 256
HALF = D_HEAD // 2
LANES = 128
MAIN_CHUNK = 256
HEADS_PER_STEP = 4
PREP_CHUNKS = 4
MIXER_SUB = 2
OUT_COL_SPLIT = 4
NORM_ROWS = 512
NORM_FEW_TILES = 4
PROJ_ROWS = 768
PROJ_PREFETCHED = 3
P_SECTIONS = 7
P_HEAD = P_SECTIONS * D_HEAD
Y_HEAD = 2 * D_HEAD

VMEM_LIMIT = 56 * 1024 * 1024


def _params(vmem=VMEM_LIMIT):
    return pltpu.CompilerParams(vmem_limit_bytes=vmem)


def _nt_dot(a, b):
    return lax.dot_general(a, b, (((1,), (1,)), ((), ())), preferred_element_type=F32)


def _prenorm_kernel(*refs, starts):
    n_in = len(starts) - 1
    x_refs = refs[:n_in]
    g_ref, wg_ref, xn_ref, gates_ref = refs[n_in:]
    i = pl.program_id(0)
    for k, x_ref in enumerate(x_refs):
        @pl.when((i >= starts[k]) & (i < starts[k + 1]))
        def _(x_ref=x_ref):
            x = x_ref[...]
            ms = jnp.mean(x * x, axis=-1, keepdims=True)
            xn = (x * lax.rsqrt(ms + EPS) * g_ref[...]).astype(BF16)
            xn_ref[...] = xn
            gates_ref[...] = _nt_dot(xn, wg_ref[...])


def _prenorm(xs, g_row, wg_t, tm):
    d = xs[0].shape[1]
    starts = [0]
    for x in xs:
        assert x.shape[0] % tm == 0
        starts.append(starts[-1] + x.shape[0] // tm)

    def x_spec(k):
        n_k = starts[k + 1] - starts[k]
        return pl.BlockSpec((tm, d), lambda i: (jnp.clip(i - starts[k], 0, n_k - 1), 0),
                            pipeline_mode=pl.Buffered(2 if n_k > NORM_FEW_TILES else 1))

    rows = starts[-1] * tm
    return pl.pallas_call(
        functools.partial(_prenorm_kernel, starts=tuple(starts)),
        out_shape=(jax.ShapeDtypeStruct((rows, d), BF16),
                   jax.ShapeDtypeStruct((rows, 2 * LANES), F32)),
        grid=(starts[-1],),
        in_specs=[x_spec(k) for k in range(len(xs))] + [
            pl.BlockSpec((1, d), lambda i: (0, 0)),
            pl.BlockSpec((2 * LANES, d), lambda i: (0, 0))],
        out_specs=(pl.BlockSpec((tm, d), lambda i: (i, 0)),
                   pl.BlockSpec((tm, 2 * LANES), lambda i: (i, 0))),
        compiler_params=_params(),
        name="prenorm",
    )(*xs, g_row, wg_t)


def _silu(z):
    return z * jax.nn.sigmoid(z)


def _proj_kernel(xn_ref, wqa, wva, woa, wza, wqb, wvb, wzb, wka, wkb,
                 cos_ref, sin_ref, cos_t_ref, sin_t_ref, p_ref, kt_ref):
    xn = xn_ref[...]

    def tok(w_ref):
        return _nt_dot(xn, w_ref[...])

    def put(sec, val):
        p_ref[:, sec * D_HEAD:(sec + 1) * D_HEAD] = val.astype(BF16)

    put(0, tok(wqa))
    put(1, tok(wva))
    put(2, jax.nn.sigmoid(tok(woa)))
    put(3, _silu(tok(wza)))
    q = tok(wqb)
    c = cos_ref[...]
    s = sin_ref[...]
    x1 = q[:, :HALF]
    x2 = q[:, HALF:]
    p_ref[:, 4 * D_HEAD:4 * D_HEAD + HALF] = (x1 * c - x2 * s).astype(BF16)
    p_ref[:, 4 * D_HEAD + HALF:5 * D_HEAD] = (x1 * s + x2 * c).astype(BF16)
    put(5, tok(wvb))
    put(6, _silu(tok(wzb)))

    scale = D_HEAD ** -0.5
    kt_ref[0, 0:D_HEAD, :] = (_nt_dot(wka[...], xn) * scale).astype(BF16)
    kb = _nt_dot(wkb[...], xn)
    ct = cos_t_ref[...] * scale
    st = sin_t_ref[...] * scale
    k1 = kb[:HALF]
    k2 = kb[HALF:]
    kt_ref[0, D_HEAD:D_HEAD + HALF, :] = (k1 * ct - k2 * st).astype(BF16)
    kt_ref[0, D_HEAD + HALF:2 * D_HEAD, :] = (k1 * st + k2 * ct).astype(BF16)


def _proj(xn, wt_bf, cos, sin, cos_t, sin_t, tm):
    rows, d = cos.shape[0], xn.shape[1]
    assert rows % tm == 0 and rows <= xn.shape[0]
    def w_spec(wsec, buffers):
        return pl.BlockSpec((D_HEAD, d), lambda h, m: (wsec * N_HEADS + h, 0),
                            pipeline_mode=pl.Buffered(buffers))

    order = (0, 2, 3, 4, 5, 7, 8, 1, 6)
    w_specs = [w_spec(wsec, 2 if i < PROJ_PREFETCHED else 1) for i, wsec in enumerate(order)]
    tab = pl.BlockSpec((tm, HALF), lambda h, m: (m, 0))
    tab_t = pl.BlockSpec((HALF, tm), lambda h, m: (0, m))
    return pl.pallas_call(
        _proj_kernel,
        out_shape=(jax.ShapeDtypeStruct((rows, N_HEADS * P_HEAD), BF16),
                   jax.ShapeDtypeStruct((N_HEADS, 2 * D_HEAD, rows), BF16)),
        grid=(N_HEADS, rows // tm),
        in_specs=[pl.BlockSpec((tm, d), lambda h, m: (m, 0))] + w_specs + [tab, tab, tab_t, tab_t],
        out_specs=(pl.BlockSpec((tm, P_HEAD), lambda h, m: (m, h)),
                   pl.BlockSpec((1, 2 * D_HEAD, tm), lambda h, m: (h, 0, m))),
        compiler_params=_params(),
        name="proj",
    )(xn, wt_bf, wt_bf, wt_bf, wt_bf, wt_bf, wt_bf, wt_bf, wt_bf, wt_bf, cos, sin, cos_t, sin_t)


def _gate_prep_kernel(g_ref, bi_ref, bf_ref, m0_ref, awe_ref, crow_ref, mo_ref, m_scr, *, L):
    @pl.when(pl.program_id(1) == 0)
    def _():
        m_scr[...] = m0_ref[0]

    rows = g_ref.shape[0]
    g = g_ref[...]
    ig = g[:, :LANES] + bi_ref[...]
    z = g[:, LANES:] + bf_ref[...]
    lf = jnp.minimum(z, 0.0) - jnp.log1p(jnp.exp(-jnp.abs(z)))
    assert L & (L - 1) == 0
    pos = lax.broadcasted_iota(jnp.int32, (rows, LANES), 0) & (L - 1)
    b = lf
    d = 1
    while d < L:
        b = b + jnp.where(pos >= d, pltpu.roll(b, d, 0), 0.0)
        d *= 2
    c = ig - b
    cm = c
    d = 1
    while d < L:
        cm = jnp.maximum(cm, jnp.where(pos >= d, pltpu.roll(cm, d, 0), -jnp.inf))
        d *= 2
    m_prev = m_scr[...]
    for k in range(rows // L):
        sl = slice(k * L, (k + 1) * L)
        mm = jnp.maximum(m_prev, cm[sl])
        m_t = b[sl] + mm
        awe_ref[sl, 0:LANES] = -mm * LOG2_E
        awe_ref[sl, LANES:2 * LANES] = jnp.exp(m_prev - mm)
        awe_ref[sl, 2 * LANES:3 * LANES] = jnp.exp(-m_t)
        m_prev = m_t[L - 1:L, :]
    c2 = c * LOG2_E
    if rows < LANES:
        c_sq = jnp.concatenate([c2, jnp.zeros((LANES - rows, LANES), F32)], axis=0)
    else:
        c_sq = c2
    c_t = c_sq.T
    for hh in range(N_HEADS):
        crow_ref[0, hh] = c_t[hh:hh + 1, :rows]
    m_scr[...] = m_prev
    mo_ref[0] = m_prev


def _gate_prep(gates, bi_row, bf_row, m0, n_streams, n_chunks, L, row_off):
    rows_out = n_streams * n_chunks * L
    per_step = PREP_CHUNKS if n_chunks % PREP_CHUNKS == 0 else 1
    n_steps = n_chunks // per_step
    rb = per_step * L
    assert (row_off * L) % rb == 0
    blk_off = row_off * L // rb
    return pl.pallas_call(
        functools.partial(_gate_prep_kernel, L=L),
        out_shape=(jax.ShapeDtypeStruct((rows_out, 3 * LANES), F32),
                   jax.ShapeDtypeStruct((n_streams, N_HEADS, 1, n_chunks * L), F32),
                   jax.ShapeDtypeStruct((n_streams, 1, LANES), F32)),
        grid=(n_streams, n_steps),
        in_specs=[pl.BlockSpec((rb, 2 * LANES), lambda s, c: (blk_off + s * n_steps + c, 0)),
                  pl.BlockSpec((1, LANES), lambda s, c: (0, 0)),
                  pl.BlockSpec((1, LANES), lambda s, c: (0, 0)),
                  pl.BlockSpec((1, 1, LANES), lambda s, c: (s, 0, 0))],
        out_specs=(pl.BlockSpec((rb, 3 * LANES), lambda s, c: (s * n_steps + c, 0)),
                   pl.BlockSpec((1, N_HEADS, 1, rb), lambda s, c: (s, 0, 0, c)),
                   pl.BlockSpec((1, 1, LANES), lambda s, c: (s, 0, 0))),
        scratch_shapes=[pltpu.VMEM((1, LANES), F32)],
        compiler_params=_params(),
        name="gate_prep",
    )(gates, bi_row, bf_row, m0)


def _lane_tile(x, n):
    if n <= LANES:
        return x[:, :n]
    return jnp.concatenate([x] * (n // LANES), axis=1)


def _head_norm(h, g_row):
    ms = jnp.mean(h * h, axis=-1, keepdims=True)
    return h * lax.rsqrt(ms + EPS) * g_row


def _rec_kernel(p_ref, kt_ref, awe_ref, c_ref, dec_ref, inter_ref, wsb_ref, gl_ref, ga_ref, gb_ref,
                c0_ref, n0_ref, s0_ref,
                y_ref, co_ref, no_ref, so_ref,
                c_scr, n_scr, s_scr, *, L):
    ci = pl.program_id(2)

    @pl.when(ci == 0)
    def _():
        c_scr[...] = c0_ref[0]
        n_scr[...] = n0_ref[0]
        s_scr[...] = s0_ref[0]

    ones_l = jnp.ones((L, LANES), BF16)
    row = lax.broadcasted_iota(jnp.int32, (L, L), 0)
    col = lax.broadcasted_iota(jnp.int32, (L, L), 1)
    causal = col <= row
    lane = lax.broadcasted_iota(jnp.int32, (L, LANES), 1)
    last = ci == pl.num_programs(2) - 1

    hb = range(HEADS_PER_STEP)

    def sec(j, i):
        lo = j * P_HEAD + i * D_HEAD
        return p_ref[:, lo:lo + D_HEAD]

    def pick(x, j):
        mine = lane == pl.program_id(1) * HEADS_PER_STEP + j
        return jnp.sum(jnp.where(mine, x, 0.0), axis=1, keepdims=True)

    def mm(x, y):
        return jnp.dot(x, y, preferred_element_type=F32)

    qa = [sec(j, 0) for j in hb]
    va = [sec(j, 1) for j in hb]
    qb = [sec(j, 4) for j in hb]
    vb = [sec(j, 5) for j in hb]
    kat = [kt_ref[0, j, 0:D_HEAD, :] for j in hb]
    kbt = [kt_ref[0, j, D_HEAD:2 * D_HEAD, :] for j in hb]
    cmat = [c_scr[j] for j in hb]
    nrep = [n_scr[j] for j in hb]
    smat = [s_scr[j] for j in hb]

    s_raw = [mm(qa[j], kat[j]) for j in hb]
    sb_raw = [mm(qb[j], kbt[j]) for j in hb]
    qc = [mm(qa[j], cmat[j].astype(BF16)) for j in hb]
    qn = [mm(qa[j], nrep[j].astype(BF16)) for j in hb]
    qs = [mm(qb[j], smat[j].astype(BF16)) for j in hb]

    a = [pick(awe_ref[:, 0:LANES], j) for j in hb]
    wi = [pick(awe_ref[:, LANES:2 * LANES], j) for j in hb]
    e = [pick(awe_ref[:, 2 * LANES:3 * LANES], j) for j in hb]
    c = [c_ref[0, j] for j in hb]
    s_bf = [(s_raw[j] * jnp.exp2(jnp.where(causal, a[j] + c[j], -jnp.inf))).astype(BF16) for j in hb]
    sb_bf = [(sb_raw[j] * dec_ref[j]).astype(BF16) for j in hb]

    w_state = [jnp.exp2(a[j][L - 1:L, :] + c[j]) for j in hb]
    kw = [kat[j] * w_state[j].astype(BF16) for j in hb]
    kwb = [kbt[j] * wsb_ref[j].astype(BF16) for j in hb]

    sv = [mm(s_bf[j], va[j]) for j in hb]
    rowsum = [mm(s_bf[j], ones_l) for j in hb]
    ob = [mm(sb_bf[j], vb[j]) for j in hb]
    c_upd = [mm(kw[j], va[j]) for j in hb]
    n_upd = [mm(kw[j], ones_l) for j in hb]
    s_upd = [mm(kwb[j], vb[j]) for j in hb]

    for j in hb:
        den = wi[j] * qn[j] + rowsum[j]
        r = 1.0 / jnp.maximum(jnp.abs(den), e[j])
        h = (wi[j] * qc[j] + sv[j]) * _lane_tile(r, D_HEAD)
        ya = _head_norm(sec(j, 2).astype(F32) * h, ga_ref[j]) * sec(j, 3).astype(F32)
        y_ref[:, j * Y_HEAD:j * Y_HEAD + D_HEAD] = ya.astype(BF16)
        o = ob[j] + _lane_tile(inter_ref[j], D_HEAD) * qs[j]
        yb = _head_norm(o, gb_ref[j]) * sec(j, 6).astype(F32)
        y_ref[:, j * Y_HEAD + D_HEAD:(j + 1) * Y_HEAD] = yb.astype(BF16)

    for j in hb:
        decay = wi[j][L - 1:L, :]
        c_new = decay * cmat[j] + c_upd[j]
        n_new = decay * nrep[j] + n_upd[j]
        s_new = _lane_tile(gl_ref[j], D_HEAD) * smat[j] + s_upd[j]
        c_scr[j] = c_new
        n_scr[j] = n_new
        s_scr[j] = s_new

        @pl.when(last)
        def _(j=j, c_new=c_new, n_new=n_new, s_new=s_new):
            co_ref[0, j] = c_new
            no_ref[0, j] = n_new
            so_ref[0, j] = s_new


def _recurrence(p, kt, awe, c_row, tabs, ga, gb, c0, n0, s0, n_streams, n_chunks, L, row_off):
    dec, inter, wsb, gl = tabs
    hb = HEADS_PER_STEP
    y_rows = n_streams * n_chunks * L

    def per_head(*tail):
        return pl.BlockSpec((hb,) + tail, lambda s, h, c: (h,) + (0,) * len(tail))

    state_spec = pl.BlockSpec((1, hb, D_HEAD, D_HEAD), lambda s, h, c: (s, h, 0, 0))
    nstate_spec = pl.BlockSpec((1, hb, D_HEAD, LANES), lambda s, h, c: (s, h, 0, 0))
    in_specs = [pl.BlockSpec((L, hb * P_HEAD), lambda s, h, c: (row_off + s * n_chunks + c, h)),
                pl.BlockSpec((1, hb, 2 * D_HEAD, L), lambda s, h, c: (s, h, 0, c)),
                pl.BlockSpec((L, 3 * LANES), lambda s, h, c: (s * n_chunks + c, 0)),
                pl.BlockSpec((1, hb, 1, L), lambda s, h, c: (s, h, 0, c)),
                per_head(L, L), per_head(L, LANES), per_head(1, L), per_head(1, LANES),
                per_head(1, D_HEAD), per_head(1, D_HEAD),
                state_spec, nstate_spec, state_spec]
    st = jax.ShapeDtypeStruct((n_streams, N_HEADS, D_HEAD, D_HEAD), F32)
    return pl.pallas_call(
        functools.partial(_rec_kernel, L=L),
        out_shape=(jax.ShapeDtypeStruct((y_rows, N_HEADS * Y_HEAD), BF16),
                   st, jax.ShapeDtypeStruct((n_streams, N_HEADS, D_HEAD, LANES), F32), st),
        grid=(n_streams, N_HEADS // hb, n_chunks),
        in_specs=in_specs,
        out_specs=(pl.BlockSpec((L, hb * Y_HEAD), lambda s, h, c: (s * n_chunks + c, h)),
                   state_spec, nstate_spec, state_spec),
        scratch_shapes=[pltpu.VMEM((hb, D_HEAD, D_HEAD), F32),
                        pltpu.VMEM((hb, D_HEAD, LANES), F32),
                        pltpu.VMEM((hb, D_HEAD, D_HEAD), F32)],
        compiler_params=_params(),
        name="recurrence",
    )(p, kt, awe, c_row, dec, inter, wsb, gl, ga, gb, c0, n0, s0)


def _mixer_kernel(p_ref, kt_ref, awe_ref, c_ref, dec_ref, inter_ref, wsb_ref, gl_ref, ga_ref, gb_ref,
                  c0_ref, n0_ref, s0_ref, w_ref, x_ref, gf_ref,
                  o_ref, co_ref, no_ref, so_ref,
                  c_scr, n_scr, s_scr, y_scr, *, L):
    tile = pl.program_id(0)
    step = pl.program_id(1)
    head = jnp.minimum(step, N_HEADS - 1)
    d = o_ref.shape[1]

    def mm(x, y):
        return jnp.dot(x, y, preferred_element_type=F32)

    def load_initial_state():
        @pl.when(tile == 0)
        def _():
            c_scr[head] = c0_ref[0, 0]
            n_scr[head] = n0_ref[0, 0]
            s_scr[head] = s0_ref[0, 0]

    def store_final_state(cmat, nrep, smat):
        @pl.when(tile == pl.num_programs(0) - 1)
        def _():
            co_ref[0, 0] = cmat
            no_ref[0, 0] = nrep
            so_ref[0, 0] = smat

    def recurrence_step():
        ones_l = jnp.ones((L, LANES), BF16)
        row = lax.broadcasted_iota(jnp.int32, (L, L), 0)
        col = lax.broadcasted_iota(jnp.int32, (L, L), 1)
        causal = col <= row
        mine = lax.broadcasted_iota(jnp.int32, (L, LANES), 1) == head

        def pick(x):
            return jnp.sum(jnp.where(mine, x, 0.0), axis=1, keepdims=True)

        subs = range(MIXER_SUB)
        rows = [slice(u * L, (u + 1) * L) for u in subs]

        def sec(u, i):
            return p_ref[rows[u], i * D_HEAD:(i + 1) * D_HEAD]

        dec = dec_ref[head]
        inter = _lane_tile(inter_ref[head], D_HEAD)
        wsb = wsb_ref[head].astype(BF16)
        gl = _lane_tile(gl_ref[head], D_HEAD)
        c_all = c_ref[0, head]
        y_half = y_scr.at[step % 2]

        qa = [sec(u, 0) for u in subs]
        va = [sec(u, 1) for u in subs]
        qb = [sec(u, 4) for u in subs]
        vb = [sec(u, 5) for u in subs]
        kat = [kt_ref[0, 0, 0:D_HEAD, rows[u]] for u in subs]
        kbt = [kt_ref[0, 0, D_HEAD:2 * D_HEAD, rows[u]] for u in subs]
        a = [pick(awe_ref[rows[u], 0:LANES]) for u in subs]
        wi = [pick(awe_ref[rows[u], LANES:2 * LANES]) for u in subs]
        e = [pick(awe_ref[rows[u], 2 * LANES:3 * LANES]) for u in subs]
        c = [c_all[:, rows[u]] for u in subs]
        s_bf = [(mm(qa[u], kat[u]) * jnp.exp2(jnp.where(causal, a[u] + c[u], -jnp.inf))).astype(BF16)
                for u in subs]
        sb_bf = [(mm(qb[u], kbt[u]) * dec).astype(BF16) for u in subs]
        kw = [kat[u] * jnp.exp2(a[u][L - 1:L, :] + c[u]).astype(BF16) for u in subs]
        kwb = [kbt[u] * wsb for u in subs]
        sv = [mm(s_bf[u], va[u]) for u in subs]
        rowsum = [mm(s_bf[u], ones_l) for u in subs]
        ob = [mm(sb_bf[u], vb[u]) for u in subs]
        c_upd = [mm(kw[u], va[u]) for u in subs]
        n_upd = [mm(kw[u], ones_l) for u in subs]
        s_upd = [mm(kwb[u], vb[u]) for u in subs]

        cmat = c_scr[head]
        nrep = n_scr[head]
        smat = s_scr[head]
        for u in subs:
            qc = mm(qa[u], cmat.astype(BF16))
            qn = mm(qa[u], nrep.astype(BF16))
            qs = mm(qb[u], smat.astype(BF16))
            den = wi[u] * qn + rowsum[u]
            r = 1.0 / jnp.maximum(jnp.abs(den), e[u])
            h = (wi[u] * qc + sv[u]) * _lane_tile(r, D_HEAD)
            ya = _head_norm(sec(u, 2).astype(F32) * h, ga_ref[head]) * sec(u, 3).astype(F32)
            y_half[rows[u], 0:D_HEAD] = ya.astype(BF16)
            yb = _head_norm(ob[u] + inter * qs, gb_ref[head]) * sec(u, 6).astype(F32)
            y_half[rows[u], D_HEAD:Y_HEAD] = yb.astype(BF16)
            decay = wi[u][L - 1:L, :]
            cmat = decay * cmat + c_upd[u]
            nrep = decay * nrep + n_upd[u]
            smat = gl * smat + s_upd[u]
        c_scr[head] = cmat
        n_scr[head] = nrep
        s_scr[head] = smat
        return cmat, nrep, smat

    def accumulate():
        y = y_scr[(step + 1) % 2]
        tq = d // OUT_COL_SPLIT
        for q in range(OUT_COL_SPLIT):
            o_ref[:, q * tq:(q + 1) * tq] += mm(y, w_ref[:, q * tq:(q + 1) * tq])

    def add_residual():
        tx = d // N_HEADS
        for hh in range(N_HEADS):
            @pl.when(step == hh + 1)
            def _(hh=hh):
                o_ref[:, hh * tx:(hh + 1) * tx] += x_ref[...]

    @pl.when(step == 0)
    def _():
        load_initial_state()
        o_ref[...] = jnp.zeros(o_ref.shape, F32)
        store_final_state(*recurrence_step())

    @pl.when((step > 0) & (step < N_HEADS))
    def _():
        load_initial_state()
        accumulate()
        state = recurrence_step()
        store_final_state(*state)
        add_residual()

    @pl.when(step == N_HEADS)
    def _():
        tx = d // N_HEADS
        lo = (N_HEADS - 1) * tx
        halves = [slice(u * L, (u + 1) * L) for u in range(MIXER_SUB)]
        y = y_scr[(step + 1) % 2]
        zs = [o_ref[r, :] + mm(y[r], w_ref[...]) for r in halves]
        for r, z in zip(halves, zs):
            z = jnp.concatenate([z[:, :lo], z[:, lo:] + x_ref[r, :]], axis=1)
            ms = jnp.mean(z * z, axis=-1, keepdims=True)
            o_ref[r, :] = z * lax.rsqrt(ms + EPS) * gf_ref[...]


def _mixer(p, kt, awe, c_row, tabs, ga, gb, c0, n0, s0, w_bf, x, gf_row, L):
    dec, inter, wsb, gl = tabs
    seq, d = x.shape
    tile_rows = MIXER_SUB * L
    n_tiles = seq // tile_rows

    def resident(arr):
        nd = arr.ndim
        return pl.BlockSpec(arr.shape, lambda t, s: (0,) * nd, pipeline_mode=pl.Buffered(1))

    def rec_head(s):
        return jnp.minimum(s, N_HEADS - 1)

    def acc_head(s):
        return jnp.maximum(s - 1, 0)

    def first_tile(*tail):
        return pl.BlockSpec((1, 1) + tail,
                            lambda t, s: (0, jnp.where(t == 0, rec_head(s), N_HEADS - 1), 0, 0))

    def last_tile(*tail):
        return pl.BlockSpec((1, 1) + tail,
                            lambda t, s: (0, jnp.where(t == n_tiles - 1, rec_head(s), 0), 0, 0))

    in_specs = [pl.BlockSpec((tile_rows, P_HEAD), lambda t, s: (t, rec_head(s))),
                pl.BlockSpec((1, 1, 2 * D_HEAD, tile_rows), lambda t, s: (0, rec_head(s), 0, t)),
                pl.BlockSpec((tile_rows, 3 * LANES), lambda t, s: (t, 0)),
                pl.BlockSpec((1, N_HEADS, 1, tile_rows), lambda t, s: (0, 0, 0, t)),
                resident(dec), resident(inter), resident(wsb), resident(gl), resident(ga), resident(gb),
                first_tile(D_HEAD, D_HEAD), first_tile(D_HEAD, LANES), first_tile(D_HEAD, D_HEAD),
                pl.BlockSpec((Y_HEAD, d), lambda t, s: (acc_head(s), 0)),
                pl.BlockSpec((tile_rows, d // N_HEADS), lambda t, s: (t, acc_head(s))),
                resident(gf_row)]
    st = jax.ShapeDtypeStruct((1, N_HEADS, D_HEAD, D_HEAD), F32)
    return pl.pallas_call(
        functools.partial(_mixer_kernel, L=L),
        out_shape=(jax.ShapeDtypeStruct((seq, d), F32),
                   st, jax.ShapeDtypeStruct((1, N_HEADS, D_HEAD, LANES), F32), st),
        grid=(n_tiles, N_HEADS + 1),
        in_specs=in_specs,
        out_specs=(pl.BlockSpec((tile_rows, d), lambda t, s: (t, 0)),
                   last_tile(D_HEAD, D_HEAD), last_tile(D_HEAD, LANES), last_tile(D_HEAD, D_HEAD)),
        scratch_shapes=[pltpu.VMEM((N_HEADS, D_HEAD, D_HEAD), F32),
                        pltpu.VMEM((N_HEADS, D_HEAD, LANES), F32),
                        pltpu.VMEM((N_HEADS, D_HEAD, D_HEAD), F32),
                        pltpu.VMEM((2, tile_rows, Y_HEAD), BF16)],
        compiler_params=_params(),
        name="mixer",
    )(p, kt, awe, c_row, dec, inter, wsb, gl, ga, gb, c0, n0, s0, w_bf, x, gf_row)


def _out_kernel(y_ref, w_ref, x_ref, g_ref, o_ref, *, n_k, tx):
    k = pl.program_id(1)

    @pl.when(k == 0)
    def _():
        o_ref[...] = jnp.dot(y_ref[...], w_ref[...], preferred_element_type=F32)

    @pl.when(k > 0)
    def _():
        o_ref[...] += jnp.dot(y_ref[...], w_ref[...], preferred_element_type=F32)

    for kk in range(n_k):
        @pl.when(k == kk)
        def _(kk=kk):
            o_ref[:, kk * tx:(kk + 1) * tx] += x_ref[...]

    @pl.when(k == n_k - 1)
    def _():
        z = o_ref[...]
        ms = jnp.mean(z * z, axis=-1, keepdims=True)
        o_ref[...] = z * lax.rsqrt(ms + EPS) * g_ref[...]


def _out_proj(y, w_bf, x, g_row, tm, n_k):
    rows, d = x.shape
    tk = y.shape[1] // n_k
    tx = d // n_k
    return pl.pallas_call(
        functools.partial(_out_kernel, n_k=n_k, tx=tx),
        out_shape=jax.ShapeDtypeStruct((rows, d), F32),
        grid=(rows // tm, n_k),
        in_specs=[pl.BlockSpec((tm, tk), lambda i, k: (i, k)),
                  pl.BlockSpec((tk, d), lambda i, k: (k, 0)),
                  pl.BlockSpec((tm, tx), lambda i, k: (i, k)),
                  pl.BlockSpec((1, d), lambda i, k: (0, 0))],
        out_specs=pl.BlockSpec((tm, d), lambda i, k: (i, 0)),
        compiler_params=_params(),
        name="out_proj",
    )(y, w_bf, x, g_row)


def _rope_tables(base, off):
    inv = ROPE_BASE ** (-jnp.arange(0, D_HEAD, 2, dtype=F32) / D_HEAD)
    ab = base.astype(F32)[:, None] * inv[None, :]
    ao = off.astype(F32)[:, None] * inv[None, :]
    cb, sb = jnp.cos(ab)[:, None, :], jnp.sin(ab)[:, None, :]
    co, so = jnp.cos(ao)[None, :, :], jnp.sin(ao)[None, :, :]
    n = base.shape[0] * off.shape[0]
    return (cb * co - sb * so).reshape(n, HALF), (sb * co + cb * so).reshape(n, HALF)


def _retention_tables(L):
    log_gamma = jnp.log1p(-jnp.power(2.0, -RET_DECAY_BASE - jnp.arange(N_HEADS, dtype=F32)))
    j = jnp.arange(L, dtype=F32)
    diff = j[:, None] - j[None, :]
    dec = jnp.where(diff >= 0, jnp.exp(log_gamma[:, None, None] * jnp.maximum(diff, 0.0)), 0.0)
    inter = jnp.exp(log_gamma[:, None] * (j + 1.0))
    inter = jnp.broadcast_to(inter[:, :, None], (N_HEADS, L, LANES))
    wsb = jnp.exp(log_gamma[:, None] * (L - 1.0 - j))[:, None, :]
    gl = jnp.broadcast_to(jnp.exp(log_gamma * L)[:, None, None], (N_HEADS, 1, LANES))
    return dec, inter, wsb, gl


def _pad_lanes(v):
    return jnp.zeros((1, LANES), F32).at[0, :v.shape[0]].set(v.astype(F32))


def _run_group(p, kt, gates, bi_row, bf_row, tabs, ga, gb, state, n_streams, n_chunks, L, row_off):
    c0, n0, m0, s0 = state
    awe, c_row, m_out = _gate_prep(gates, bi_row, bf_row, m0, n_streams, n_chunks, L, row_off)
    y, c_out, n_out, s_out = _recurrence(p, kt, awe, c_row, tabs, ga, gb, c0, n0, s0,
                                         n_streams, n_chunks, L, row_off)
    return y, (c_out, n_out, m_out, s_out)


def kernel(x_prompt, x_sample, state_mlstm_C, state_mlstm_n, state_mlstm_m, state_ret_S,
           meta_tokens, g_norm1, w_in, b_igate, b_fgate, g_head_a, g_head_b, w_out, g_final):
    depth = w_in.shape[0]
    assert depth == 1 and x_prompt.shape[0] == 1
    d = x_prompt.shape[-1]
    seq = x_prompt.shape[1]
    n_dec, t_dec = x_sample.shape[0], x_sample.shape[1]
    n_meta = meta_tokens.shape[0]
    width = N_HEADS * D_HEAD
    assert seq % MAIN_CHUNK == 0 and MAIN_CHUNK % t_dec == 0 and t_dec % n_meta == 0

    wt = w_in[0].T
    wt_bf = wt.astype(BF16)
    wg_t = jnp.zeros((2 * LANES, d), F32)
    wg_t = wg_t.at[:N_HEADS].set(wt[9 * width:9 * width + N_HEADS])
    wg_t = wg_t.at[LANES:LANES + N_HEADS].set(wt[9 * width + N_HEADS:])
    wg_t = wg_t.astype(BF16)
    w_out_bf = (w_out[0].reshape(2, N_HEADS, D_HEAD, d).transpose(1, 0, 2, 3)
                .reshape(2 * width, d).astype(BF16))
    g1 = g_norm1[0][None, :]
    bi_row = _pad_lanes(b_igate[0])
    bf_row = _pad_lanes(b_fgate[0])
    ga = g_head_a[0].reshape(N_HEADS, 1, D_HEAD)
    gb = g_head_b[0].reshape(N_HEADS, 1, D_HEAD)

    x_main = x_prompt[0]
    n_samp = n_dec * t_dec
    meta_at = seq + n_samp
    n_rows = -(-(meta_at + n_meta) // PROJ_ROWS) * PROJ_ROWS
    x_samp = x_sample.reshape(n_samp, d)
    meta_rows = -(-(n_rows - meta_at) // NORM_ROWS) * NORM_ROWS
    x_meta = jnp.pad(meta_tokens.astype(F32), ((0, meta_rows - n_meta), (0, 0)))
    blk = jnp.arange(n_rows // t_dec)
    base = jnp.where(blk < seq // t_dec, n_meta + t_dec * blk,
                     jnp.where(blk < meta_at // t_dec, n_meta + PAST_LEN, 0))
    cos, sin = _rope_tables(base, jnp.arange(t_dec))

    xn, gates = _prenorm([x_main, x_samp, x_meta], g1, wg_t, NORM_ROWS)
    p, kt = _proj(xn, wt_bf, cos, sin, cos.T, sin.T, PROJ_ROWS)

    zeros_c = jnp.zeros((1, N_HEADS, D_HEAD, D_HEAD), F32)
    zero_state = (zeros_c, jnp.zeros((1, N_HEADS, D_HEAD, LANES), F32), jnp.zeros((1, 1, LANES), F32), zeros_c)
    kt_meta = kt[:, :, meta_at:meta_at + n_meta][None]
    _, st_meta = _run_group(p, kt_meta, gates, bi_row, bf_row, _retention_tables(n_meta),
                            ga, gb, zero_state, 1, 1, n_meta, meta_at // n_meta)

    gf = g_final[None, :]
    c_meta, n_meta_st, m_meta, s_meta = st_meta
    awe_p, c_row_p, m_p = _gate_prep(gates, bi_row, bf_row, m_meta, 1, seq // MAIN_CHUNK, MAIN_CHUNK, 0)
    y_prompt, c_p, n_p, s_p = _mixer(p, kt[None], awe_p, c_row_p, _retention_tables(MAIN_CHUNK), ga, gb,
                                     c_meta, n_meta_st, s_meta, w_out_bf, x_main, gf, MAIN_CHUNK)
    st_p = (c_p, n_p, m_p, s_p)

    kt_samp = kt[:, :, seq:meta_at].reshape(N_HEADS, 2 * D_HEAD, n_dec, t_dec).transpose(2, 0, 1, 3)
    n0 = jnp.broadcast_to(state_mlstm_n[0].astype(F32)[..., None], (n_dec, N_HEADS, D_HEAD, LANES))
    m0 = jnp.zeros((n_dec, 1, LANES), F32).at[:, 0, :N_HEADS].set(state_mlstm_m[0].astype(F32))
    st0 = (state_mlstm_C[0].astype(F32), n0, m0, state_ret_S[0].astype(F32))
    y_s, st_s = _run_group(p, kt_samp, gates, bi_row, bf_row, _retention_tables(t_dec),
                           ga, gb, st0, n_dec, 1, t_dec, seq // t_dec)

    y_prompt = y_prompt[None]
    y_sample = _out_proj(y_s, w_out_bf, x_sample.reshape(n_samp, d), gf, 512, 4)
    y_sample = y_sample.reshape(n_dec, t_dec, d)

    dp = x_prompt.dtype
    ds = state_mlstm_C.dtype

    def states(st, dt):
        c_out, n_out, m_out, s_out = st
        return (c_out[None].astype(dt), n_out[..., 0][None].astype(dt),
                m_out[:, 0, :N_HEADS][None].astype(dt), s_out[None].astype(dt))

    pc, pn, pm, ps = states(st_p, dp)
    sc, sn, sm, ss = states(st_s, ds)
    return (y_prompt, y_sample, pc, pn, pm, ps, sc, sn, sm, ss)
```

```python
import functools

import jax
import jax.numpy as jnp
from jax import lax
from jax.experimental import pallas as pl
from jax.experimental.pallas import tpu as pltpu

F32 = jnp.float32
BF16 = jnp.bfloat16

EPS = 1e-6
LOG2_E = 1.4426950408889634
ROPE_BASE = 10000.0
RET_DECAY_BASE = 5.0
PAST_LEN = 2048

N_HEADS = 8
D_HEAD = 256
HALF = D_HEAD // 2
LANES = 128
MAIN_CHUNK = 256
HEADS_PER_STEP = 8
PREP_CHUNKS = 4
MIXER_SUB = 2
OUT_COL_SPLIT = 4
NORM_ROWS = 256
PROJ_ROWS = 768
PROJ_PREFETCHED = 3
P_SECTIONS = 7
P_HEAD = P_SECTIONS * D_HEAD
Y_HEAD = 2 * D_HEAD

VMEM_LIMIT = 56 * 1024 * 1024


def _params(vmem=VMEM_LIMIT):
    return pltpu.CompilerParams(vmem_limit_bytes=vmem)


def _nt_dot(a, b):
    return lax.dot_general(a, b, (((1,), (1,)), ((), ())), preferred_element_type=F32)


def _prenorm_kernel(*refs, starts):
    n_in = len(starts) - 1
    x_refs = refs[:n_in]
    g_ref, wg_ref, xn_ref, gates_ref = refs[n_in:]
    i = pl.program_id(0)
    for k, x_ref in enumerate(x_refs):
        @pl.when((i >= starts[k]) & (i < starts[k + 1]))
        def _(x_ref=x_ref):
            x = x_ref[...]
            ms = jnp.mean(x * x, axis=-1, keepdims=True)
            xn = (x * lax.rsqrt(ms + EPS) * g_ref[...]).astype(BF16)
            xn_ref[...] = xn
            gates_ref[...] = _nt_dot(xn, wg_ref[...])


def _prenorm(xs, g_row, wg_t, tm):
    d = xs[0].shape[1]
    starts = [0]
    for x in xs:
        assert x.shape[0] % tm == 0
        starts.append(starts[-1] + x.shape[0] // tm)

    def x_spec(k):
        return pl.BlockSpec((tm, d), lambda i: (jnp.clip(i - starts[k], 0, starts[k + 1] - starts[k] - 1), 0))

    rows = starts[-1] * tm
    return pl.pallas_call(
        functools.partial(_prenorm_kernel, starts=tuple(starts)),
        out_shape=(jax.ShapeDtypeStruct((rows, d), BF16),
                   jax.ShapeDtypeStruct((rows, 2 * LANES), F32)),
        grid=(starts[-1],),
        in_specs=[x_spec(k) for k in range(len(xs))] + [
            pl.BlockSpec((1, d), lambda i: (0, 0)),
            pl.BlockSpec((2 * LANES, d), lambda i: (0, 0))],
        out_specs=(pl.BlockSpec((tm, d), lambda i: (i, 0)),
                   pl.BlockSpec((tm, 2 * LANES), lambda i: (i, 0))),
        compiler_params=_params(),
        name="prenorm",
    )(*xs, g_row, wg_t)


def _silu(z):
    return z * jax.nn.sigmoid(z)


def _proj_kernel(xn_ref, wqa, wva, woa, wza, wqb, wvb, wzb, wka, wkb,
                 cos_ref, sin_ref, cos_t_ref, sin_t_ref, p_ref, kt_ref):
    xn = xn_ref[...]

    def tok(w_ref):
        return _nt_dot(xn, w_ref[...])

    def put(sec, val):
        p_ref[:, sec * D_HEAD:(sec + 1) * D_HEAD] = val.astype(BF16)

    put(0, tok(wqa))
    put(1, tok(wva))
    put(2, jax.nn.sigmoid(tok(woa)))
    put(3, _silu(tok(wza)))
    q = tok(wqb)
    c = cos_ref[...]
    s = sin_ref[...]
    x1 = q[:, :HALF]
    x2 = q[:, HALF:]
    p_ref[:, 4 * D_HEAD:4 * D_HEAD + HALF] = (x1 * c - x2 * s).astype(BF16)
    p_ref[:, 4 * D_HEAD + HALF:5 * D_HEAD] = (x1 * s + x2 * c).astype(BF16)
    put(5, tok(wvb))
    put(6, _silu(tok(wzb)))

    scale = D_HEAD ** -0.5
    kt_ref[0, 0:D_HEAD, :] = (_nt_dot(wka[...], xn) * scale).astype(BF16)
    kb = _nt_dot(wkb[...], xn)
    ct = cos_t_ref[...] * scale
    st = sin_t_ref[...] * scale
    k1 = kb[:HALF]
    k2 = kb[HALF:]
    kt_ref[0, D_HEAD:D_HEAD + HALF, :] = (k1 * ct - k2 * st).astype(BF16)
    kt_ref[0, D_HEAD + HALF:2 * D_HEAD, :] = (k1 * st + k2 * ct).astype(BF16)


def _proj(xn, wt_bf, cos, sin, cos_t, sin_t, tm):
    rows, d = xn.shape
    def w_spec(wsec, buffers):
        return pl.BlockSpec((D_HEAD, d), lambda h, m: (wsec * N_HEADS + h, 0),
                            pipeline_mode=pl.Buffered(buffers))

    order = (0, 2, 3, 4, 5, 7, 8, 1, 6)
    w_specs = [w_spec(wsec, 2 if i < PROJ_PREFETCHED else 1) for i, wsec in enumerate(order)]
    tab = pl.BlockSpec((tm, HALF), lambda h, m: (m, 0))
    tab_t = pl.BlockSpec((HALF, tm), lambda h, m: (0, m))
    return pl.pallas_call(
        _proj_kernel,
        out_shape=(jax.ShapeDtypeStruct((rows, N_HEADS * P_HEAD), BF16),
                   jax.ShapeDtypeStruct((N_HEADS, 2 * D_HEAD, rows), BF16)),
        grid=(N_HEADS, rows // tm),
        in_specs=[pl.BlockSpec((tm, d), lambda h, m: (m, 0))] + w_specs + [tab, tab, tab_t, tab_t],
        out_specs=(pl.BlockSpec((tm, P_HEAD), lambda h, m: (m, h)),
                   pl.BlockSpec((1, 2 * D_HEAD, tm), lambda h, m: (h, 0, m))),
        compiler_params=_params(),
        name="proj",
    )(xn, wt_bf, wt_bf, wt_bf, wt_bf, wt_bf, wt_bf, wt_bf, wt_bf, wt_bf, cos, sin, cos_t, sin_t)


def _gate_prep_kernel(g_ref, bi_ref, bf_ref, m0_ref, awe_ref, crow_ref, mo_ref, m_scr, *, L):
    @pl.when(pl.program_id(1) == 0)
    def _():
        m_scr[...] = m0_ref[0]

    rows = g_ref.shape[0]
    g = g_ref[...]
    ig = g[:, :LANES] + bi_ref[...]
    z = g[:, LANES:] + bf_ref[...]
    lf = jnp.minimum(z, 0.0) - jnp.log1p(jnp.exp(-jnp.abs(z)))
    assert L & (L - 1) == 0
    pos = lax.broadcasted_iota(jnp.int32, (rows, LANES), 0) & (L - 1)
    b = lf
    d = 1
    while d < L:
        b = b + jnp.where(pos >= d, pltpu.roll(b, d, 0), 0.0)
        d *= 2
    c = ig - b
    cm = c
    d = 1
    while d < L:
        cm = jnp.maximum(cm, jnp.where(pos >= d, pltpu.roll(cm, d, 0), -jnp.inf))
        d *= 2
    m_prev = m_scr[...]
    for k in range(rows // L):
        sl = slice(k * L, (k + 1) * L)
        mm = jnp.maximum(m_prev, cm[sl])
        m_t = b[sl] + mm
        awe_ref[sl, 0:LANES] = -mm * LOG2_E
        awe_ref[sl, LANES:2 * LANES] = jnp.exp(m_prev - mm)
        awe_ref[sl, 2 * LANES:3 * LANES] = jnp.exp(-m_t)
        m_prev = m_t[L - 1:L, :]
    c2 = c * LOG2_E
    if rows < LANES:
        c_sq = jnp.concatenate([c2, jnp.zeros((LANES - rows, LANES), F32)], axis=0)
    else:
        c_sq = c2
    c_t = c_sq.T
    for hh in range(N_HEADS):
        crow_ref[0, hh] = c_t[hh:hh + 1, :rows]
    m_scr[...] = m_prev
    mo_ref[0] = m_prev


def _gate_prep(gates, bi_row, bf_row, m0, n_streams, n_chunks, L, row_off):
    rows_out = n_streams * n_chunks * L
    per_step = PREP_CHUNKS if n_chunks % PREP_CHUNKS == 0 else 1
    n_steps = n_chunks // per_step
    rb = per_step * L
    assert (row_off * L) % rb == 0
    blk_off = row_off * L // rb
    return pl.pallas_call(
        functools.partial(_gate_prep_kernel, L=L),
        out_shape=(jax.ShapeDtypeStruct((rows_out, 3 * LANES), F32),
                   jax.ShapeDtypeStruct((n_streams, N_HEADS, 1, n_chunks * L), F32),
                   jax.ShapeDtypeStruct((n_streams, 1, LANES), F32)),
        grid=(n_streams, n_steps),
        in_specs=[pl.BlockSpec((rb, 2 * LANES), lambda s, c: (blk_off + s * n_steps + c, 0)),
                  pl.BlockSpec((1, LANES), lambda s, c: (0, 0)),
                  pl.BlockSpec((1, LANES), lambda s, c: (0, 0)),
                  pl.BlockSpec((1, 1, LANES), lambda s, c: (s, 0, 0))],
        out_specs=(pl.BlockSpec((rb, 3 * LANES), lambda s, c: (s * n_steps + c, 0)),
                   pl.BlockSpec((1, N_HEADS, 1, rb), lambda s, c: (s, 0, 0, c)),
                   pl.BlockSpec((1, 1, LANES), lambda s, c: (s, 0, 0))),
        scratch_shapes=[pltpu.VMEM((1, LANES), F32)],
        compiler_params=_params(),
        name="gate_prep",
    )(gates, bi_row, bf_row, m0)


def _lane_tile(x, n):
    if n <= LANES:
        return x[:, :n]
    return jnp.concatenate([x] * (n // LANES), axis=1)


def _head_norm(h, g_row):
    ms = jnp.mean(h * h, axis=-1, keepdims=True)
    return h * lax.rsqrt(ms + EPS) * g_row


def _rec_kernel(p_ref, kt_ref, awe_ref, c_ref, dec_ref, inter_ref, wsb_ref, gl_ref, ga_ref, gb_ref,
                c0_ref, n0_ref, s0_ref,
                y_ref, co_ref, no_ref, so_ref,
                c_scr, n_scr, s_scr, *, L):
    ci = pl.program_id(2)

    @pl.when(ci == 0)
    def _():
        c_scr[...] = c0_ref[0]
        n_scr[...] = n0_ref[0]
        s_scr[...] = s0_ref[0]

    ones_l = jnp.ones((L, LANES), BF16)
    row = lax.broadcasted_iota(jnp.int32, (L, L), 0)
    col = lax.broadcasted_iota(jnp.int32, (L, L), 1)
    causal = col <= row
    lane = lax.broadcasted_iota(jnp.int32, (L, LANES), 1)
    last = ci == pl.num_programs(2) - 1

    hb = range(HEADS_PER_STEP)

    def sec(j, i):
        lo = j * P_HEAD + i * D_HEAD
        return p_ref[:, lo:lo + D_HEAD]

    def pick(x, j):
        mine = lane == pl.program_id(1) * HEADS_PER_STEP + j
        return jnp.sum(jnp.where(mine, x, 0.0), axis=1, keepdims=True)

    def mm(x, y):
        return jnp.dot(x, y, preferred_element_type=F32)

    qa = [sec(j, 0) for j in hb]
    va = [sec(j, 1) for j in hb]
    qb = [sec(j, 4) for j in hb]
    vb = [sec(j, 5) for j in hb]
    kat = [kt_ref[0, j, 0:D_HEAD, :] for j in hb]
    kbt = [kt_ref[0, j, D_HEAD:2 * D_HEAD, :] for j in hb]
    cmat = [c_scr[j] for j in hb]
    nrep = [n_scr[j] for j in hb]
    smat = [s_scr[j] for j in hb]

    s_raw = [mm(qa[j], kat[j]) for j in hb]
    sb_raw = [mm(qb[j], kbt[j]) for j in hb]
    qc = [mm(qa[j], cmat[j].astype(BF16)) for j in hb]
    qn = [mm(qa[j], nrep[j].astype(BF16)) for j in hb]
    qs = [mm(qb[j], smat[j].astype(BF16)) for j in hb]

    a = [pick(awe_ref[:, 0:LANES], j) for j in hb]
    wi = [pick(awe_ref[:, LANES:2 * LANES], j) for j in hb]
    e = [pick(awe_ref[:, 2 * LANES:3 * LANES], j) for j in hb]
    c = [c_ref[0, j] for j in hb]
    s_bf = [(s_raw[j] * jnp.exp2(jnp.where(causal, a[j] + c[j], -jnp.inf))).astype(BF16) for j in hb]
    sb_bf = [(sb_raw[j] * dec_ref[j]).astype(BF16) for j in hb]

    w_state = [jnp.exp2(a[j][L - 1:L, :] + c[j]) for j in hb]
    kw = [kat[j] * w_state[j].astype(BF16) for j in hb]
    kwb = [kbt[j] * wsb_ref[j].astype(BF16) for j in hb]

    sv = [mm(s_bf[j], va[j]) for j in hb]
    rowsum = [mm(s_bf[j], ones_l) for j in hb]
    ob = [mm(sb_bf[j], vb[j]) for j in hb]
    c_upd = [mm(kw[j], va[j]) for j in hb]
    n_upd = [mm(kw[j], ones_l) for j in hb]
    s_upd = [mm(kwb[j], vb[j]) for j in hb]

    for j in hb:
        den = wi[j] * qn[j] + rowsum[j]
        r = 1.0 / jnp.maximum(jnp.abs(den), e[j])
        h = (wi[j] * qc[j] + sv[j]) * _lane_tile(r, D_HEAD)
        ya = _head_norm(sec(j, 2).astype(F32) * h, ga_ref[j]) * sec(j, 3).astype(F32)
        y_ref[:, j * Y_HEAD:j * Y_HEAD + D_HEAD] = ya.astype(BF16)
        o = ob[j] + _lane_tile(inter_ref[j], D_HEAD) * qs[j]
        yb = _head_norm(o, gb_ref[j]) * sec(j, 6).astype(F32)
        y_ref[:, j * Y_HEAD + D_HEAD:(j + 1) * Y_HEAD] = yb.astype(BF16)

    for j in hb:
        decay = wi[j][L - 1:L, :]
        c_new = decay * cmat[j] + c_upd[j]
        n_new = decay * nrep[j] + n_upd[j]
        s_new = _lane_tile(gl_ref[j], D_HEAD) * smat[j] + s_upd[j]
        c_scr[j] = c_new
        n_scr[j] = n_new
        s_scr[j] = s_new

        @pl.when(last)
        def _(j=j, c_new=c_new, n_new=n_new, s_new=s_new):
            co_ref[0, j] = c_new
            no_ref[0, j] = n_new
            so_ref[0, j] = s_new


def _recurrence(p, kt, awe, c_row, tabs, ga, gb, c0, n0, s0, n_streams, n_chunks, L, row_off):
    dec, inter, wsb, gl = tabs
    hb = HEADS_PER_STEP
    y_rows = n_streams * n_chunks * L

    def per_head(*tail):
        return pl.BlockSpec((hb,) + tail, lambda s, h, c: (h,) + (0,) * len(tail))

    state_spec = pl.BlockSpec((1, hb, D_HEAD, D_HEAD), lambda s, h, c: (s, h, 0, 0))
    nstate_spec = pl.BlockSpec((1, hb, D_HEAD, LANES), lambda s, h, c: (s, h, 0, 0))
    in_specs = [pl.BlockSpec((L, hb * P_HEAD), lambda s, h, c: (row_off + s * n_chunks + c, h)),
                pl.BlockSpec((1, hb, 2 * D_HEAD, L), lambda s, h, c: (s, h, 0, c)),
                pl.BlockSpec((L, 3 * LANES), lambda s, h, c: (s * n_chunks + c, 0)),
                pl.BlockSpec((1, hb, 1, L), lambda s, h, c: (s, h, 0, c)),
                per_head(L, L), per_head(L, LANES), per_head(1, L), per_head(1, LANES),
                per_head(1, D_HEAD), per_head(1, D_HEAD),
                state_spec, nstate_spec, state_spec]
    st = jax.ShapeDtypeStruct((n_streams, N_HEADS, D_HEAD, D_HEAD), F32)
    return pl.pallas_call(
        functools.partial(_rec_kernel, L=L),
        out_shape=(jax.ShapeDtypeStruct((y_rows, N_HEADS * Y_HEAD), BF16),
                   st, jax.ShapeDtypeStruct((n_streams, N_HEADS, D_HEAD, LANES), F32), st),
        grid=(n_streams, N_HEADS // hb, n_chunks),
        in_specs=in_specs,
        out_specs=(pl.BlockSpec((L, hb * Y_HEAD), lambda s, h, c: (s * n_chunks + c, h)),
                   state_spec, nstate_spec, state_spec),
        scratch_shapes=[pltpu.VMEM((hb, D_HEAD, D_HEAD), F32),
                        pltpu.VMEM((hb, D_HEAD, LANES), F32),
                        pltpu.VMEM((hb, D_HEAD, D_HEAD), F32)],
        compiler_params=_params(),
        name="recurrence",
    )(p, kt, awe, c_row, dec, inter, wsb, gl, ga, gb, c0, n0, s0)


def _mixer_kernel(p_ref, kt_ref, awe_ref, c_ref, dec_ref, inter_ref, wsb_ref, gl_ref, ga_ref, gb_ref,
                  c0_ref, n0_ref, s0_ref, w_ref, x_ref, gf_ref,
                  o_ref, co_ref, no_ref, so_ref,
                  c_scr, n_scr, s_scr, y_scr, *, L):
    tile = pl.program_id(0)
    step = pl.program_id(1)
    head = jnp.minimum(step, N_HEADS - 1)
    d = o_ref.shape[1]

    def mm(x, y):
        return jnp.dot(x, y, preferred_element_type=F32)

    def load_initial_state():
        @pl.when(tile == 0)
        def _():
            c_scr[head] = c0_ref[0, 0]
            n_scr[head] = n0_ref[0, 0]
            s_scr[head] = s0_ref[0, 0]

    def store_final_state(cmat, nrep, smat):
        @pl.when(tile == pl.num_programs(0) - 1)
        def _():
            co_ref[0, 0] = cmat
            no_ref[0, 0] = nrep
            so_ref[0, 0] = smat

    def recurrence_step():
        ones_l = jnp.ones((L, LANES), BF16)
        row = lax.broadcasted_iota(jnp.int32, (L, L), 0)
        col = lax.broadcasted_iota(jnp.int32, (L, L), 1)
        causal = col <= row
        mine = lax.broadcasted_iota(jnp.int32, (L, LANES), 1) == head

        def pick(x):
            return jnp.sum(jnp.where(mine, x, 0.0), axis=1, keepdims=True)

        subs = range(MIXER_SUB)
        rows = [slice(u * L, (u + 1) * L) for u in subs]

        def sec(u, i):
            return p_ref[rows[u], i * D_HEAD:(i + 1) * D_HEAD]

        dec = dec_ref[head]
        inter = _lane_tile(inter_ref[head], D_HEAD)
        wsb = wsb_ref[head].astype(BF16)
        gl = _lane_tile(gl_ref[head], D_HEAD)
        c_all = c_ref[0, head]
        y_half = y_scr.at[step % 2]

        qa = [sec(u, 0) for u in subs]
        va = [sec(u, 1) for u in subs]
        qb = [sec(u, 4) for u in subs]
        vb = [sec(u, 5) for u in subs]
        kat = [kt_ref[0, 0, 0:D_HEAD, rows[u]] for u in subs]
        kbt = [kt_ref[0, 0, D_HEAD:2 * D_HEAD, rows[u]] for u in subs]
        a = [pick(awe_ref[rows[u], 0:LANES]) for u in subs]
        wi = [pick(awe_ref[rows[u], LANES:2 * LANES]) for u in subs]
        e = [pick(awe_ref[rows[u], 2 * LANES:3 * LANES]) for u in subs]
        c = [c_all[:, rows[u]] for u in subs]
        s_bf = [(mm(qa[u], kat[u]) * jnp.exp2(jnp.where(causal, a[u] + c[u], -jnp.inf))).astype(BF16)
                for u in subs]
        sb_bf = [(mm(qb[u], kbt[u]) * dec).astype(BF16) for u in subs]
        kw = [kat[u] * jnp.exp2(a[u][L - 1:L, :] + c[u]).astype(BF16) for u in subs]
        kwb = [kbt[u] * wsb for u in subs]
        sv = [mm(s_bf[u], va[u]) for u in subs]
        rowsum = [mm(s_bf[u], ones_l) for u in subs]
        ob = [mm(sb_bf[u], vb[u]) for u in subs]
        c_upd = [mm(kw[u], va[u]) for u in subs]
        n_upd = [mm(kw[u], ones_l) for u in subs]
        s_upd = [mm(kwb[u], vb[u]) for u in subs]

        cmat = c_scr[head]
        nrep = n_scr[head]
        smat = s_scr[head]
        for u in subs:
            qc = mm(qa[u], cmat.astype(BF16))
            qn = mm(qa[u], nrep.astype(BF16))
            qs = mm(qb[u], smat.astype(BF16))
            den = wi[u] * qn + rowsum[u]
            r = 1.0 / jnp.maximum(jnp.abs(den), e[u])
            h = (wi[u] * qc + sv[u]) * _lane_tile(r, D_HEAD)
            ya = _head_norm(sec(u, 2).astype(F32) * h, ga_ref[head]) * sec(u, 3).astype(F32)
            y_half[rows[u], 0:D_HEAD] = ya.astype(BF16)
            yb = _head_norm(ob[u] + inter * qs, gb_ref[head]) * sec(u, 6).astype(F32)
            y_half[rows[u], D_HEAD:Y_HEAD] = yb.astype(BF16)
            decay = wi[u][L - 1:L, :]
            cmat = decay * cmat + c_upd[u]
            nrep = decay * nrep + n_upd[u]
            smat = gl * smat + s_upd[u]
        c_scr[head] = cmat
        n_scr[head] = nrep
        s_scr[head] = smat
        return cmat, nrep, smat

    def accumulate():
        y = y_scr[(step + 1) % 2]
        tq = d // OUT_COL_SPLIT
        for q in range(OUT_COL_SPLIT):
            o_ref[:, q * tq:(q + 1) * tq] += mm(y, w_ref[:, q * tq:(q + 1) * tq])

    def add_residual():
        tx = d // N_HEADS
        for hh in range(N_HEADS):
            @pl.when(step == hh + 1)
            def _(hh=hh):
                o_ref[:, hh * tx:(hh + 1) * tx] += x_ref[...]

    @pl.when(step == 0)
    def _():
        load_initial_state()
        o_ref[...] = jnp.zeros(o_ref.shape, F32)
        store_final_state(*recurrence_step())

    @pl.when((step > 0) & (step < N_HEADS))
    def _():
        load_initial_state()
        accumulate()
        state = recurrence_step()
        store_final_state(*state)
        add_residual()

    @pl.when(step == N_HEADS)
    def _():
        tx = d // N_HEADS
        lo = (N_HEADS - 1) * tx
        halves = [slice(u * L, (u + 1) * L) for u in range(MIXER_SUB)]
        y = y_scr[(step + 1) % 2]
        zs = [o_ref[r, :] + mm(y[r], w_ref[...]) for r in halves]
        for r, z in zip(halves, zs):
            z = jnp.concatenate([z[:, :lo], z[:, lo:] + x_ref[r, :]], axis=1)
            ms = jnp.mean(z * z, axis=-1, keepdims=True)
            o_ref[r, :] = z * lax.rsqrt(ms + EPS) * gf_ref[...]


def _mixer(p, kt, awe, c_row, tabs, ga, gb, c0, n0, s0, w_bf, x, gf_row, L):
    dec, inter, wsb, gl = tabs
    seq, d = x.shape
    tile_rows = MIXER_SUB * L
    n_tiles = seq // tile_rows

    def resident(arr):
        nd = arr.ndim
        return pl.BlockSpec(arr.shape, lambda t, s: (0,) * nd, pipeline_mode=pl.Buffered(1))

    def rec_head(s):
        return jnp.minimum(s, N_HEADS - 1)

    def acc_head(s):
        return jnp.maximum(s - 1, 0)

    def first_tile(*tail):
        return pl.BlockSpec((1, 1) + tail,
                            lambda t, s: (0, jnp.where(t == 0, rec_head(s), N_HEADS - 1), 0, 0))

    def last_tile(*tail):
        return pl.BlockSpec((1, 1) + tail,
                            lambda t, s: (0, jnp.where(t == n_tiles - 1, rec_head(s), 0), 0, 0))

    in_specs = [pl.BlockSpec((tile_rows, P_HEAD), lambda t, s: (t, rec_head(s))),
                pl.BlockSpec((1, 1, 2 * D_HEAD, tile_rows), lambda t, s: (0, rec_head(s), 0, t)),
                pl.BlockSpec((tile_rows, 3 * LANES), lambda t, s: (t, 0)),
                pl.BlockSpec((1, N_HEADS, 1, tile_rows), lambda t, s: (0, 0, 0, t)),
                resident(dec), resident(inter), resident(wsb), resident(gl), resident(ga), resident(gb),
                first_tile(D_HEAD, D_HEAD), first_tile(D_HEAD, LANES), first_tile(D_HEAD, D_HEAD),
                pl.BlockSpec((Y_HEAD, d), lambda t, s: (acc_head(s), 0)),
                pl.BlockSpec((tile_rows, d // N_HEADS), lambda t, s: (t, acc_head(s))),
                resident(gf_row)]
    st = jax.ShapeDtypeStruct((1, N_HEADS, D_HEAD, D_HEAD), F32)
    return pl.pallas_call(
        functools.partial(_mixer_kernel, L=L),
        out_shape=(jax.ShapeDtypeStruct((seq, d), F32),
                   st, jax.ShapeDtypeStruct((1, N_HEADS, D_HEAD, LANES), F32), st),
        grid=(n_tiles, N_HEADS + 1),
        in_specs=in_specs,
        out_specs=(pl.BlockSpec((tile_rows, d), lambda t, s: (t, 0)),
                   last_tile(D_HEAD, D_HEAD), last_tile(D_HEAD, LANES), last_tile(D_HEAD, D_HEAD)),
        scratch_shapes=[pltpu.VMEM((N_HEADS, D_HEAD, D_HEAD), F32),
                        pltpu.VMEM((N_HEADS, D_HEAD, LANES), F32),
                        pltpu.VMEM((N_HEADS, D_HEAD, D_HEAD), F32),
                        pltpu.VMEM((2, tile_rows, Y_HEAD), BF16)],
        compiler_params=_params(),
        name="mixer",
    )(p, kt, awe, c_row, dec, inter, wsb, gl, ga, gb, c0, n0, s0, w_bf, x, gf_row)


def _out_kernel(y_ref, w_ref, x_ref, g_ref, o_ref, *, n_k, tx):
    k = pl.program_id(1)

    @pl.when(k == 0)
    def _():
        o_ref[...] = jnp.dot(y_ref[...], w_ref[...], preferred_element_type=F32)

    @pl.when(k > 0)
    def _():
        o_ref[...] += jnp.dot(y_ref[...], w_ref[...], preferred_element_type=F32)

    for kk in range(n_k):
        @pl.when(k == kk)
        def _(kk=kk):
            o_ref[:, kk * tx:(kk + 1) * tx] += x_ref[...]

    @pl.when(k == n_k - 1)
    def _():
        z = o_ref[...]
        ms = jnp.mean(z * z, axis=-1, keepdims=True)
        o_ref[...] = z * lax.rsqrt(ms + EPS) * g_ref[...]


def _out_proj(y, w_bf, x, g_row, tm, n_k):
    rows, d = x.shape
    tk = y.shape[1] // n_k
    tx = d // n_k
    return pl.pallas_call(
        functools.partial(_out_kernel, n_k=n_k, tx=tx),
        out_shape=jax.ShapeDtypeStruct((rows, d), F32),
        grid=(rows // tm, n_k),
        in_specs=[pl.BlockSpec((tm, tk), lambda i, k: (i, k)),
                  pl.BlockSpec((tk, d), lambda i, k: (k, 0)),
                  pl.BlockSpec((tm, tx), lambda i, k: (i, k)),
                  pl.BlockSpec((1, d), lambda i, k: (0, 0))],
        out_specs=pl.BlockSpec((tm, d), lambda i, k: (i, 0)),
        compiler_params=_params(),
        name="out_proj",
    )(y, w_bf, x, g_row)


def _rope_tables(base, off):
    inv = ROPE_BASE ** (-jnp.arange(0, D_HEAD, 2, dtype=F32) / D_HEAD)
    ab = base.astype(F32)[:, None] * inv[None, :]
    ao = off.astype(F32)[:, None] * inv[None, :]
    cb, sb = jnp.cos(ab)[:, None, :], jnp.sin(ab)[:, None, :]
    co, so = jnp.cos(ao)[None, :, :], jnp.sin(ao)[None, :, :]
    n = base.shape[0] * off.shape[0]
    return (cb * co - sb * so).reshape(n, HALF), (sb * co + cb * so).reshape(n, HALF)


def _retention_tables(L):
    log_gamma = jnp.log1p(-jnp.power(2.0, -RET_DECAY_BASE - jnp.arange(N_HEADS, dtype=F32)))
    j = jnp.arange(L, dtype=F32)
    diff = j[:, None] - j[None, :]
    dec = jnp.where(diff >= 0, jnp.exp(log_gamma[:, None, None] * jnp.maximum(diff, 0.0)), 0.0)
    inter = jnp.exp(log_gamma[:, None] * (j + 1.0))
    inter = jnp.broadcast_to(inter[:, :, None], (N_HEADS, L, LANES))
    wsb = jnp.exp(log_gamma[:, None] * (L - 1.0 - j))[:, None, :]
    gl = jnp.broadcast_to(jnp.exp(log_gamma * L)[:, None, None], (N_HEADS, 1, LANES))
    return dec, inter, wsb, gl


def _pad_lanes(v):
    return jnp.zeros((1, LANES), F32).at[0, :v.shape[0]].set(v.astype(F32))


def _run_group(p, kt, gates, bi_row, bf_row, tabs, ga, gb, state, n_streams, n_chunks, L, row_off):
    c0, n0, m0, s0 = state
    awe, c_row, m_out = _gate_prep(gates, bi_row, bf_row, m0, n_streams, n_chunks, L, row_off)
    y, c_out, n_out, s_out = _recurrence(p, kt, awe, c_row, tabs, ga, gb, c0, n0, s0,
                                         n_streams, n_chunks, L, row_off)
    return y, (c_out, n_out, m_out, s_out)


def kernel(x_prompt, x_sample, state_mlstm_C, state_mlstm_n, state_mlstm_m, state_ret_S,
           meta_tokens, g_norm1, w_in, b_igate, b_fgate, g_head_a, g_head_b, w_out, g_final):
    depth = w_in.shape[0]
    assert depth == 1 and x_prompt.shape[0] == 1
    d = x_prompt.shape[-1]
    seq = x_prompt.shape[1]
    n_dec, t_dec = x_sample.shape[0], x_sample.shape[1]
    n_meta = meta_tokens.shape[0]
    width = N_HEADS * D_HEAD
    assert seq % MAIN_CHUNK == 0 and MAIN_CHUNK % t_dec == 0 and t_dec % n_meta == 0

    wt = w_in[0].T
    wt_bf = wt.astype(BF16)
    wg_t = jnp.zeros((2 * LANES, d), F32)
    wg_t = wg_t.at[:N_HEADS].set(wt[9 * width:9 * width + N_HEADS])
    wg_t = wg_t.at[LANES:LANES + N_HEADS].set(wt[9 * width + N_HEADS:])
    wg_t = wg_t.astype(BF16)
    w_out_bf = (w_out[0].reshape(2, N_HEADS, D_HEAD, d).transpose(1, 0, 2, 3)
                .reshape(2 * width, d).astype(BF16))
    g1 = g_norm1[0][None, :]
    bi_row = _pad_lanes(b_igate[0])
    bf_row = _pad_lanes(b_fgate[0])
    ga = g_head_a[0].reshape(N_HEADS, 1, D_HEAD)
    gb = g_head_b[0].reshape(N_HEADS, 1, D_HEAD)

    x_main = x_prompt[0]
    n_samp = n_dec * t_dec
    meta_at = seq + n_samp
    meta_pad = -(meta_at + n_meta) % PROJ_ROWS
    x_samp = x_sample.reshape(n_samp, d)
    x_meta = jnp.pad(meta_tokens.astype(F32), ((0, meta_pad), (0, 0)))
    n_rows = meta_at + x_meta.shape[0]
    blk = jnp.arange(n_rows // t_dec)
    base = jnp.where(blk < seq // t_dec, n_meta + t_dec * blk,
                     jnp.where(blk < meta_at // t_dec, n_meta + PAST_LEN, 0))
    cos, sin = _rope_tables(base, jnp.arange(t_dec))

    xn, gates = _prenorm([x_main, x_samp, x_meta], g1, wg_t, NORM_ROWS)
    p, kt = _proj(xn, wt_bf, cos, sin, cos.T, sin.T, PROJ_ROWS)

    zeros_c = jnp.zeros((1, N_HEADS, D_HEAD, D_HEAD), F32)
    zero_state = (zeros_c, jnp.zeros((1, N_HEADS, D_HEAD, LANES), F32), jnp.zeros((1, 1, LANES), F32), zeros_c)
    kt_meta = kt[:, :, meta_at:meta_at + n_meta][None]
    _, st_meta = _run_group(p, kt_meta, gates, bi_row, bf_row, _retention_tables(n_meta),
                            ga, gb, zero_state, 1, 1, n_meta, meta_at // n_meta)

    gf = g_final[None, :]
    c_meta, n_meta_st, m_meta, s_meta = st_meta
    awe_p, c_row_p, m_p = _gate_prep(gates, bi_row, bf_row, m_meta, 1, seq // MAIN_CHUNK, MAIN_CHUNK, 0)
    y_prompt, c_p, n_p, s_p = _mixer(p, kt[None], awe_p, c_row_p, _retention_tables(MAIN_CHUNK), ga, gb,
                                     c_meta, n_meta_st, s_meta, w_out_bf, x_main, gf, MAIN_CHUNK)
    st_p = (c_p, n_p, m_p, s_p)

    kt_samp = kt[:, :, seq:meta_at].reshape(N_HEADS, 2 * D_HEAD, n_dec, t_dec).transpose(2, 0, 1, 3)
    n0 = jnp.broadcast_to(state_mlstm_n[0].astype(F32)[..., None], (n_dec, N_HEADS, D_HEAD, LANES))
    m0 = jnp.zeros((n_dec, 1, LANES), F32).at[:, 0, :N_HEADS].set(state_mlstm_m[0].astype(F32))
    st0 = (state_mlstm_C[0].astype(F32), n0, m0, state_ret_S[0].astype(F32))
    y_s, st_s = _run_group(p, kt_samp, gates, bi_row, bf_row, _retention_tables(t_dec),
                           ga, gb, st0, n_dec, 1, t_dec, seq // t_dec)

    y_prompt = y_prompt[None]
    y_sample = _out_proj(y_s, w_out_bf, x_sample.reshape(n_samp, d), gf, 512, 4)
    y_sample = y_sample.reshape(n_dec, t_dec, d)

    dp = x_prompt.dtype
    ds = state_mlstm_C.dtype

    def states(st, dt):
        c_out, n_out, m_out, s_out = st
        return (c_out[None].astype(dt), n_out[..., 0][None].astype(dt),
                m_out[:, 0, :N_HEADS][None].astype(dt), s_out[None].astype(dt))

    pc, pn, pm, ps = states(st_p, dp)
    sc, sn, sm, ss = states(st_s, ds)
    return (y_prompt, y_sample, pc, pn, pm, ps, sc, sn, sm, ss)
```

```python
import functools

import jax
import jax.numpy as jnp
from jax import lax
from jax.experimental import pallas as pl
from jax.experimental.pallas import tpu as pltpu

F32 = jnp.float32
BF16 = jnp.bfloat16

EPS = 1e-6
LOG2_E = 1.4426950408889634
ROPE_BASE = 10000.0
RET_DECAY_BASE = 5.0
PAST_LEN = 2048

N_HEADS = 8
D_HEAD = 256
HALF = D_HEAD // 2
LANES = 128
MAIN_CHUNK = 256
HEADS_PER_STEP = 8
PREP_CHUNKS = 4
MIXER_SUB = 2
OUT_COL_SPLIT = 4
NORM_ROWS = 256
PROJ_ROWS = 768
PROJ_PREFETCHED = 3
WO_ROWS = 32
P_SECTIONS = 7
P_HEAD = P_SECTIONS * D_HEAD
Y_HEAD = 2 * D_HEAD

VMEM_LIMIT = 56 * 1024 * 1024


def _params(vmem=VMEM_LIMIT):
    return pltpu.CompilerParams(vmem_limit_bytes=vmem)


def _nt_dot(a, b):
    return lax.dot_general(a, b, (((1,), (1,)), ((), ())), preferred_element_type=F32)


def _prenorm_kernel(*refs, starts):
    n_in = len(starts) - 1
    x_refs = refs[:n_in]
    g_ref, wg_ref, xn_ref, gates_ref = refs[n_in:]
    i = pl.program_id(0)
    for k, x_ref in enumerate(x_refs):
        @pl.when((i >= starts[k]) & (i < starts[k + 1]))
        def _(x_ref=x_ref):
            x = x_ref[...]
            ms = jnp.mean(x * x, axis=-1, keepdims=True)
            xn = (x * lax.rsqrt(ms + EPS) * g_ref[...]).astype(BF16)
            xn_ref[...] = xn
            gates_ref[...] = _nt_dot(xn, wg_ref[...])


def _prenorm(xs, g_row, wg_t, tm):
    d = xs[0].shape[1]
    starts = [0]
    for x in xs:
        assert x.shape[0] % tm == 0
        starts.append(starts[-1] + x.shape[0] // tm)

    def x_spec(k):
        return pl.BlockSpec((tm, d), lambda i: (jnp.clip(i - starts[k], 0, starts[k + 1] - starts[k] - 1), 0))

    rows = starts[-1] * tm
    return pl.pallas_call(
        functools.partial(_prenorm_kernel, starts=tuple(starts)),
        out_shape=(jax.ShapeDtypeStruct((rows, d), BF16),
                   jax.ShapeDtypeStruct((rows, 2 * LANES), F32)),
        grid=(starts[-1],),
        in_specs=[x_spec(k) for k in range(len(xs))] + [
            pl.BlockSpec((1, d), lambda i: (0, 0)),
            pl.BlockSpec((2 * LANES, d), lambda i: (0, 0))],
        out_specs=(pl.BlockSpec((tm, d), lambda i: (i, 0)),
                   pl.BlockSpec((tm, 2 * LANES), lambda i: (i, 0))),
        compiler_params=_params(),
        name="prenorm",
    )(*xs, g_row, wg_t)


def _silu(z):
    return z * jax.nn.sigmoid(z)


def _proj_kernel(xn_ref, wqa, wva, woa, wza, wqb, wvb, wzb, wka, wkb,
                 cos_ref, sin_ref, cos_t_ref, sin_t_ref, wo_ref, p_ref, kt_ref, wo_bf_ref, *, wo_blocks):
    @pl.when(pl.program_id(0) * pl.num_programs(1) + pl.program_id(1) < wo_blocks)
    def _():
        wo_bf_ref[...] = wo_ref[...].astype(BF16)

    xn = xn_ref[...]

    def tok(w_ref):
        return _nt_dot(xn, w_ref[...])

    def put(sec, val):
        p_ref[:, sec * D_HEAD:(sec + 1) * D_HEAD] = val.astype(BF16)

    put(0, tok(wqa))
    put(1, tok(wva))
    put(2, jax.nn.sigmoid(tok(woa)))
    put(3, _silu(tok(wza)))
    q = tok(wqb)
    c = cos_ref[...]
    s = sin_ref[...]
    x1 = q[:, :HALF]
    x2 = q[:, HALF:]
    p_ref[:, 4 * D_HEAD:4 * D_HEAD + HALF] = (x1 * c - x2 * s).astype(BF16)
    p_ref[:, 4 * D_HEAD + HALF:5 * D_HEAD] = (x1 * s + x2 * c).astype(BF16)
    put(5, tok(wvb))
    put(6, _silu(tok(wzb)))

    scale = D_HEAD ** -0.5
    kt_ref[0, 0:D_HEAD, :] = (_nt_dot(wka[...], xn) * scale).astype(BF16)
    kb = _nt_dot(wkb[...], xn)
    ct = cos_t_ref[...] * scale
    st = sin_t_ref[...] * scale
    k1 = kb[:HALF]
    k2 = kb[HALF:]
    kt_ref[0, D_HEAD:D_HEAD + HALF, :] = (k1 * ct - k2 * st).astype(BF16)
    kt_ref[0, D_HEAD + HALF:2 * D_HEAD, :] = (k1 * st + k2 * ct).astype(BF16)


def _proj(xn, wt_bf, cos, sin, cos_t, sin_t, w_out, tm):
    rows, d = xn.shape
    n_m = rows // tm
    wo_blocks = w_out.shape[0] // WO_ROWS
    per_group = wo_blocks // 2
    per_head = per_group // N_HEADS
    assert wo_blocks <= N_HEADS * n_m and per_head * N_HEADS * 2 == wo_blocks

    def wo_src(h, m):
        return (jnp.minimum(h * n_m + m, wo_blocks - 1), 0)

    def wo_dst(h, m):
        g = jnp.minimum(h * n_m + m, wo_blocks - 1)
        grp, head, part = g // per_group, (g % per_group) // per_head, g % per_head
        return ((head * 2 + grp) * per_head + part, 0)

    def w_spec(wsec, buffers):
        return pl.BlockSpec((D_HEAD, d), lambda h, m: (wsec * N_HEADS + h, 0),
                            pipeline_mode=pl.Buffered(buffers))

    order = (0, 2, 3, 4, 5, 7, 8, 1, 6)
    w_specs = [w_spec(wsec, 2 if i < PROJ_PREFETCHED else 1) for i, wsec in enumerate(order)]
    tab = pl.BlockSpec((tm, HALF), lambda h, m: (m, 0))
    tab_t = pl.BlockSpec((HALF, tm), lambda h, m: (0, m))
    d_out = w_out.shape[1]
    return pl.pallas_call(
        functools.partial(_proj_kernel, wo_blocks=wo_blocks),
        out_shape=(jax.ShapeDtypeStruct((rows, N_HEADS * P_HEAD), BF16),
                   jax.ShapeDtypeStruct((N_HEADS, 2 * D_HEAD, rows), BF16),
                   jax.ShapeDtypeStruct(w_out.shape, BF16)),
        grid=(N_HEADS, n_m),
        in_specs=[pl.BlockSpec((tm, d), lambda h, m: (m, 0))] + w_specs + [
            tab, tab, tab_t, tab_t, pl.BlockSpec((WO_ROWS, d_out), wo_src)],
        out_specs=(pl.BlockSpec((tm, P_HEAD), lambda h, m: (m, h)),
                   pl.BlockSpec((1, 2 * D_HEAD, tm), lambda h, m: (h, 0, m)),
                   pl.BlockSpec((WO_ROWS, d_out), wo_dst)),
        compiler_params=_params(),
        name="proj",
    )(xn, wt_bf, wt_bf, wt_bf, wt_bf, wt_bf, wt_bf, wt_bf, wt_bf, wt_bf, cos, sin, cos_t, sin_t, w_out)


def _gate_prep_kernel(g_ref, bi_ref, bf_ref, m0_ref, awe_ref, crow_ref, mo_ref, m_scr, *, L):
    @pl.when(pl.program_id(1) == 0)
    def _():
        m_scr[...] = m0_ref[0]

    rows = g_ref.shape[0]
    g = g_ref[...]
    ig = g[:, :LANES] + bi_ref[...]
    z = g[:, LANES:] + bf_ref[...]
    lf = jnp.minimum(z, 0.0) - jnp.log1p(jnp.exp(-jnp.abs(z)))
    assert L & (L - 1) == 0
    pos = lax.broadcasted_iota(jnp.int32, (rows, LANES), 0) & (L - 1)
    b = lf
    d = 1
    while d < L:
        b = b + jnp.where(pos >= d, pltpu.roll(b, d, 0), 0.0)
        d *= 2
    c = ig - b
    cm = c
    d = 1
    while d < L:
        cm = jnp.maximum(cm, jnp.where(pos >= d, pltpu.roll(cm, d, 0), -jnp.inf))
        d *= 2
    m_prev = m_scr[...]
    for k in range(rows // L):
        sl = slice(k * L, (k + 1) * L)
        mm = jnp.maximum(m_prev, cm[sl])
        m_t = b[sl] + mm
        awe_ref[sl, 0:LANES] = -mm * LOG2_E
        awe_ref[sl, LANES:2 * LANES] = jnp.exp(m_prev - mm)
        awe_ref[sl, 2 * LANES:3 * LANES] = jnp.exp(-m_t)
        m_prev = m_t[L - 1:L, :]
    c2 = c * LOG2_E
    if rows < LANES:
        c_sq = jnp.concatenate([c2, jnp.zeros((LANES - rows, LANES), F32)], axis=0)
    else:
        c_sq = c2
    c_t = c_sq.T
    for hh in range(N_HEADS):
        crow_ref[0, hh] = c_t[hh:hh + 1, :rows]
    m_scr[...] = m_prev
    mo_ref[0] = m_prev


def _gate_prep(gates, bi_row, bf_row, m0, n_streams, n_chunks, L, row_off):
    rows_out = n_streams * n_chunks * L
    per_step = PREP_CHUNKS if n_chunks % PREP_CHUNKS == 0 else 1
    n_steps = n_chunks // per_step
    rb = per_step * L
    assert (row_off * L) % rb == 0
    blk_off = row_off * L // rb
    return pl.pallas_call(
        functools.partial(_gate_prep_kernel, L=L),
        out_shape=(jax.ShapeDtypeStruct((rows_out, 3 * LANES), F32),
                   jax.ShapeDtypeStruct((n_streams, N_HEADS, 1, n_chunks * L), F32),
                   jax.ShapeDtypeStruct((n_streams, 1, LANES), F32)),
        grid=(n_streams, n_steps),
        in_specs=[pl.BlockSpec((rb, 2 * LANES), lambda s, c: (blk_off + s * n_steps + c, 0)),
                  pl.BlockSpec((1, LANES), lambda s, c: (0, 0)),
                  pl.BlockSpec((1, LANES), lambda s, c: (0, 0)),
                  pl.BlockSpec((1, 1, LANES), lambda s, c: (s, 0, 0))],
        out_specs=(pl.BlockSpec((rb, 3 * LANES), lambda s, c: (s * n_steps + c, 0)),
                   pl.BlockSpec((1, N_HEADS, 1, rb), lambda s, c: (s, 0, 0, c)),
                   pl.BlockSpec((1, 1, LANES), lambda s, c: (s, 0, 0))),
        scratch_shapes=[pltpu.VMEM((1, LANES), F32)],
        compiler_params=_params(),
        name="gate_prep",
    )(gates, bi_row, bf_row, m0)


def _lane_tile(x, n):
    if n <= LANES:
        return x[:, :n]
    return jnp.concatenate([x] * (n // LANES), axis=1)


def _head_norm(h, g_row):
    ms = jnp.mean(h * h, axis=-1, keepdims=True)
    return h * lax.rsqrt(ms + EPS) * g_row


def _rec_kernel(p_ref, kt_ref, awe_ref, c_ref, dec_ref, inter_ref, wsb_ref, gl_ref, ga_ref, gb_ref,
                c0_ref, n0_ref, s0_ref,
                y_ref, co_ref, no_ref, so_ref,
                c_scr, n_scr, s_scr, *, L):
    ci = pl.program_id(2)

    @pl.when(ci == 0)
    def _():
        c_scr[...] = c0_ref[0]
        n_scr[...] = n0_ref[0]
        s_scr[...] = s0_ref[0]

    ones_l = jnp.ones((L, LANES), BF16)
    row = lax.broadcasted_iota(jnp.int32, (L, L), 0)
    col = lax.broadcasted_iota(jnp.int32, (L, L), 1)
    causal = col <= row
    lane = lax.broadcasted_iota(jnp.int32, (L, LANES), 1)
    last = ci == pl.num_programs(2) - 1

    hb = range(HEADS_PER_STEP)

    def sec(j, i):
        lo = j * P_HEAD + i * D_HEAD
        return p_ref[:, lo:lo + D_HEAD]

    def pick(x, j):
        mine = lane == pl.program_id(1) * HEADS_PER_STEP + j
        return jnp.sum(jnp.where(mine, x, 0.0), axis=1, keepdims=True)

    def mm(x, y):
        return jnp.dot(x, y, preferred_element_type=F32)

    qa = [sec(j, 0) for j in hb]
    va = [sec(j, 1) for j in hb]
    qb = [sec(j, 4) for j in hb]
    vb = [sec(j, 5) for j in hb]
    kat = [kt_ref[0, j, 0:D_HEAD, :] for j in hb]
    kbt = [kt_ref[0, j, D_HEAD:2 * D_HEAD, :] for j in hb]
    cmat = [c_scr[j] for j in hb]
    nrep = [n_scr[j] for j in hb]
    smat = [s_scr[j] for j in hb]

    s_raw = [mm(qa[j], kat[j]) for j in hb]
    sb_raw = [mm(qb[j], kbt[j]) for j in hb]
    qc = [mm(qa[j], cmat[j].astype(BF16)) for j in hb]
    qn = [mm(qa[j], nrep[j].astype(BF16)) for j in hb]
    qs = [mm(qb[j], smat[j].astype(BF16)) for j in hb]

    a = [pick(awe_ref[:, 0:LANES], j) for j in hb]
    wi = [pick(awe_ref[:, LANES:2 * LANES], j) for j in hb]
    e = [pick(awe_ref[:, 2 * LANES:3 * LANES], j) for j in hb]
    c = [c_ref[0, j] for j in hb]
    s_bf = [(s_raw[j] * jnp.exp2(jnp.where(causal, a[j] + c[j], -jnp.inf))).astype(BF16) for j in hb]
    sb_bf = [(sb_raw[j] * dec_ref[j]).astype(BF16) for j in hb]

    w_state = [jnp.exp2(a[j][L - 1:L, :] + c[j]) for j in hb]
    kw = [kat[j] * w_state[j].astype(BF16) for j in hb]
    kwb = [kbt[j] * wsb_ref[j].astype(BF16) for j in hb]

    sv = [mm(s_bf[j], va[j]) for j in hb]
    rowsum = [mm(s_bf[j], ones_l) for j in hb]
    ob = [mm(sb_bf[j], vb[j]) for j in hb]
    c_upd = [mm(kw[j], va[j]) for j in hb]
    n_upd = [mm(kw[j], ones_l) for j in hb]
    s_upd = [mm(kwb[j], vb[j]) for j in hb]

    for j in hb:
        den = wi[j] * qn[j] + rowsum[j]
        r = 1.0 / jnp.maximum(jnp.abs(den), e[j])
        h = (wi[j] * qc[j] + sv[j]) * _lane_tile(r, D_HEAD)
        ya = _head_norm(sec(j, 2).astype(F32) * h, ga_ref[j]) * sec(j, 3).astype(F32)
        y_ref[:, j * Y_HEAD:j * Y_HEAD + D_HEAD] = ya.astype(BF16)
        o = ob[j] + _lane_tile(inter_ref[j], D_HEAD) * qs[j]
        yb = _head_norm(o, gb_ref[j]) * sec(j, 6).astype(F32)
        y_ref[:, j * Y_HEAD + D_HEAD:(j + 1) * Y_HEAD] = yb.astype(BF16)

    for j in hb:
        decay = wi[j][L - 1:L, :]
        c_new = decay * cmat[j] + c_upd[j]
        n_new = decay * nrep[j] + n_upd[j]
        s_new = _lane_tile(gl_ref[j], D_HEAD) * smat[j] + s_upd[j]
        c_scr[j] = c_new
        n_scr[j] = n_new
        s_scr[j] = s_new

        @pl.when(last)
        def _(j=j, c_new=c_new, n_new=n_new, s_new=s_new):
            co_ref[0, j] = c_new
            no_ref[0, j] = n_new
            so_ref[0, j] = s_new


def _recurrence(p, kt, awe, c_row, tabs, ga, gb, c0, n0, s0, n_streams, n_chunks, L, row_off):
    dec, inter, wsb, gl = tabs
    hb = HEADS_PER_STEP
    y_rows = n_streams * n_chunks * L

    def per_head(*tail):
        return pl.BlockSpec((hb,) + tail, lambda s, h, c: (h,) + (0,) * len(tail))

    state_spec = pl.BlockSpec((1, hb, D_HEAD, D_HEAD), lambda s, h, c: (s, h, 0, 0))
    nstate_spec = pl.BlockSpec((1, hb, D_HEAD, LANES), lambda s, h, c: (s, h, 0, 0))
    in_specs = [pl.BlockSpec((L, hb * P_HEAD), lambda s, h, c: (row_off + s * n_chunks + c, h)),
                pl.BlockSpec((1, hb, 2 * D_HEAD, L), lambda s, h, c: (s, h, 0, c)),
                pl.BlockSpec((L, 3 * LANES), lambda s, h, c: (s * n_chunks + c, 0)),
                pl.BlockSpec((1, hb, 1, L), lambda s, h, c: (s, h, 0, c)),
                per_head(L, L), per_head(L, LANES), per_head(1, L), per_head(1, LANES),
                per_head(1, D_HEAD), per_head(1, D_HEAD),
                state_spec, nstate_spec, state_spec]
    st = jax.ShapeDtypeStruct((n_streams, N_HEADS, D_HEAD, D_HEAD), F32)
    return pl.pallas_call(
        functools.partial(_rec_kernel, L=L),
        out_shape=(jax.ShapeDtypeStruct((y_rows, N_HEADS * Y_HEAD), BF16),
                   st, jax.ShapeDtypeStruct((n_streams, N_HEADS, D_HEAD, LANES), F32), st),
        grid=(n_streams, N_HEADS // hb, n_chunks),
        in_specs=in_specs,
        out_specs=(pl.BlockSpec((L, hb * Y_HEAD), lambda s, h, c: (s * n_chunks + c, h)),
                   state_spec, nstate_spec, state_spec),
        scratch_shapes=[pltpu.VMEM((hb, D_HEAD, D_HEAD), F32),
                        pltpu.VMEM((hb, D_HEAD, LANES), F32),
                        pltpu.VMEM((hb, D_HEAD, D_HEAD), F32)],
        compiler_params=_params(),
        name="recurrence",
    )(p, kt, awe, c_row, dec, inter, wsb, gl, ga, gb, c0, n0, s0)


def _mixer_kernel(p_ref, kt_ref, awe_ref, c_ref, dec_ref, inter_ref, wsb_ref, gl_ref, ga_ref, gb_ref,
                  c0_ref, n0_ref, s0_ref, w_ref, x_ref, gf_ref,
                  o_ref, co_ref, no_ref, so_ref,
                  c_scr, n_scr, s_scr, y_scr, *, L):
    tile = pl.program_id(0)
    step = pl.program_id(1)
    head = jnp.minimum(step, N_HEADS - 1)
    d = o_ref.shape[1]

    def mm(x, y):
        return jnp.dot(x, y, preferred_element_type=F32)

    def load_initial_state():
        @pl.when(tile == 0)
        def _():
            c_scr[head] = c0_ref[0, 0]
            n_scr[head] = n0_ref[0, 0]
            s_scr[head] = s0_ref[0, 0]

    def store_final_state(cmat, nrep, smat):
        @pl.when(tile == pl.num_programs(0) - 1)
        def _():
            co_ref[0, 0] = cmat
            no_ref[0, 0] = nrep
            so_ref[0, 0] = smat

    def recurrence_step():
        ones_l = jnp.ones((L, LANES), BF16)
        row = lax.broadcasted_iota(jnp.int32, (L, L), 0)
        col = lax.broadcasted_iota(jnp.int32, (L, L), 1)
        causal = col <= row
        mine = lax.broadcasted_iota(jnp.int32, (L, LANES), 1) == head

        def pick(x):
            return jnp.sum(jnp.where(mine, x, 0.0), axis=1, keepdims=True)

        subs = range(MIXER_SUB)
        rows = [slice(u * L, (u + 1) * L) for u in subs]

        def sec(u, i):
            return p_ref[rows[u], i * D_HEAD:(i + 1) * D_HEAD]

        dec = dec_ref[head]
        inter = _lane_tile(inter_ref[head], D_HEAD)
        wsb = wsb_ref[head].astype(BF16)
        gl = _lane_tile(gl_ref[head], D_HEAD)
        c_all = c_ref[0, head]
        y_half = y_scr.at[step % 2]

        qa = [sec(u, 0) for u in subs]
        va = [sec(u, 1) for u in subs]
        qb = [sec(u, 4) for u in subs]
        vb = [sec(u, 5) for u in subs]
        kat = [kt_ref[0, 0, 0:D_HEAD, rows[u]] for u in subs]
        kbt = [kt_ref[0, 0, D_HEAD:2 * D_HEAD, rows[u]] for u in subs]
        a = [pick(awe_ref[rows[u], 0:LANES]) for u in subs]
        wi = [pick(awe_ref[rows[u], LANES:2 * LANES]) for u in subs]
        e = [pick(awe_ref[rows[u], 2 * LANES:3 * LANES]) for u in subs]
        c = [c_all[:, rows[u]] for u in subs]
        s_bf = [(mm(qa[u], kat[u]) * jnp.exp2(jnp.where(causal, a[u] + c[u], -jnp.inf))).astype(BF16)
                for u in subs]
        sb_bf = [(mm(qb[u], kbt[u]) * dec).astype(BF16) for u in subs]
        kw = [kat[u] * jnp.exp2(a[u][L - 1:L, :] + c[u]).astype(BF16) for u in subs]
        kwb = [kbt[u] * wsb for u in subs]
        sv = [mm(s_bf[u], va[u]) for u in subs]
        rowsum = [mm(s_bf[u], ones_l) for u in subs]
        ob = [mm(sb_bf[u], vb[u]) for u in subs]
        c_upd = [mm(kw[u], va[u]) for u in subs]
        n_upd = [mm(kw[u], ones_l) for u in subs]
        s_upd = [mm(kwb[u], vb[u]) for u in subs]

        cmat = c_scr[head]
        nrep = n_scr[head]
        smat = s_scr[head]
        for u in subs:
            qc = mm(qa[u], cmat.astype(BF16))
            qn = mm(qa[u], nrep.astype(BF16))
            qs = mm(qb[u], smat.astype(BF16))
            den = wi[u] * qn + rowsum[u]
            r = 1.0 / jnp.maximum(jnp.abs(den), e[u])
            h = (wi[u] * qc + sv[u]) * _lane_tile(r, D_HEAD)
            ya = _head_norm(sec(u, 2).astype(F32) * h, ga_ref[head]) * sec(u, 3).astype(F32)
            y_half[rows[u], 0:D_HEAD] = ya.astype(BF16)
            yb = _head_norm(ob[u] + inter * qs, gb_ref[head]) * sec(u, 6).astype(F32)
            y_half[rows[u], D_HEAD:Y_HEAD] = yb.astype(BF16)
            decay = wi[u][L - 1:L, :]
            cmat = decay * cmat + c_upd[u]
            nrep = decay * nrep + n_upd[u]
            smat = gl * smat + s_upd[u]
        c_scr[head] = cmat
        n_scr[head] = nrep
        s_scr[head] = smat
        return cmat, nrep, smat

    def accumulate():
        y = y_scr[(step + 1) % 2]
        tq = d // OUT_COL_SPLIT
        for q in range(OUT_COL_SPLIT):
            o_ref[:, q * tq:(q + 1) * tq] += mm(y, w_ref[:, q * tq:(q + 1) * tq])

    def add_residual():
        tx = d // N_HEADS
        for hh in range(N_HEADS):
            @pl.when(step == hh + 1)
            def _(hh=hh):
                o_ref[:, hh * tx:(hh + 1) * tx] += x_ref[...]

    @pl.when(step == 0)
    def _():
        load_initial_state()
        o_ref[...] = jnp.zeros(o_ref.shape, F32)
        store_final_state(*recurrence_step())

    @pl.when((step > 0) & (step < N_HEADS))
    def _():
        load_initial_state()
        accumulate()
        state = recurrence_step()
        store_final_state(*state)
        add_residual()

    @pl.when(step == N_HEADS)
    def _():
        tx = d // N_HEADS
        lo = (N_HEADS - 1) * tx
        halves = [slice(u * L, (u + 1) * L) for u in range(MIXER_SUB)]
        y = y_scr[(step + 1) % 2]
        zs = [o_ref[r, :] + mm(y[r], w_ref[...]) for r in halves]
        for r, z in zip(halves, zs):
            z = jnp.concatenate([z[:, :lo], z[:, lo:] + x_ref[r, :]], axis=1)
            ms = jnp.mean(z * z, axis=-1, keepdims=True)
            o_ref[r, :] = z * lax.rsqrt(ms + EPS) * gf_ref[...]


def _mixer(p, kt, awe, c_row, tabs, ga, gb, c0, n0, s0, w_bf, x, gf_row, L):
    dec, inter, wsb, gl = tabs
    seq, d = x.shape
    tile_rows = MIXER_SUB * L
    n_tiles = seq // tile_rows

    def resident(arr):
        nd = arr.ndim
        return pl.BlockSpec(arr.shape, lambda t, s: (0,) * nd, pipeline_mode=pl.Buffered(1))

    def rec_head(s):
        return jnp.minimum(s, N_HEADS - 1)

    def acc_head(s):
        return jnp.maximum(s - 1, 0)

    def first_tile(*tail):
        return pl.BlockSpec((1, 1) + tail,
                            lambda t, s: (0, jnp.where(t == 0, rec_head(s), N_HEADS - 1), 0, 0))

    def last_tile(*tail):
        return pl.BlockSpec((1, 1) + tail,
                            lambda t, s: (0, jnp.where(t == n_tiles - 1, rec_head(s), 0), 0, 0))

    in_specs = [pl.BlockSpec((tile_rows, P_HEAD), lambda t, s: (t, rec_head(s))),
                pl.BlockSpec((1, 1, 2 * D_HEAD, tile_rows), lambda t, s: (0, rec_head(s), 0, t)),
                pl.BlockSpec((tile_rows, 3 * LANES), lambda t, s: (t, 0)),
                pl.BlockSpec((1, N_HEADS, 1, tile_rows), lambda t, s: (0, 0, 0, t)),
                resident(dec), resident(inter), resident(wsb), resident(gl), resident(ga), resident(gb),
                first_tile(D_HEAD, D_HEAD), first_tile(D_HEAD, LANES), first_tile(D_HEAD, D_HEAD),
                pl.BlockSpec((Y_HEAD, d), lambda t, s: (acc_head(s), 0)),
                pl.BlockSpec((tile_rows, d // N_HEADS), lambda t, s: (t, acc_head(s))),
                resident(gf_row)]
    st = jax.ShapeDtypeStruct((1, N_HEADS, D_HEAD, D_HEAD), F32)
    return pl.pallas_call(
        functools.partial(_mixer_kernel, L=L),
        out_shape=(jax.ShapeDtypeStruct((seq, d), F32),
                   st, jax.ShapeDtypeStruct((1, N_HEADS, D_HEAD, LANES), F32), st),
        grid=(n_tiles, N_HEADS + 1),
        in_specs=in_specs,
        out_specs=(pl.BlockSpec((tile_rows, d), lambda t, s: (t, 0)),
                   last_tile(D_HEAD, D_HEAD), last_tile(D_HEAD, LANES), last_tile(D_HEAD, D_HEAD)),
        scratch_shapes=[pltpu.VMEM((N_HEADS, D_HEAD, D_HEAD), F32),
                        pltpu.VMEM((N_HEADS, D_HEAD, LANES), F32),
                        pltpu.VMEM((N_HEADS, D_HEAD, D_HEAD), F32),
                        pltpu.VMEM((2, tile_rows, Y_HEAD), BF16)],
        compiler_params=_params(),
        name="mixer",
    )(p, kt, awe, c_row, dec, inter, wsb, gl, ga, gb, c0, n0, s0, w_bf, x, gf_row)


def _out_kernel(y_ref, w_ref, x_ref, g_ref, o_ref, *, n_k, tx):
    k = pl.program_id(1)

    @pl.when(k == 0)
    def _():
        o_ref[...] = jnp.dot(y_ref[...], w_ref[...], preferred_element_type=F32)

    @pl.when(k > 0)
    def _():
        o_ref[...] += jnp.dot(y_ref[...], w_ref[...], preferred_element_type=F32)

    for kk in range(n_k):
        @pl.when(k == kk)
        def _(kk=kk):
            o_ref[:, kk * tx:(kk + 1) * tx] += x_ref[...]

    @pl.when(k == n_k - 1)
    def _():
        z = o_ref[...]
        ms = jnp.mean(z * z, axis=-1, keepdims=True)
        o_ref[...] = z * lax.rsqrt(ms + EPS) * g_ref[...]


def _out_proj(y, w_bf, x, g_row, tm, n_k):
    rows, d = x.shape
    tk = y.shape[1] // n_k
    tx = d // n_k
    return pl.pallas_call(
        functools.partial(_out_kernel, n_k=n_k, tx=tx),
        out_shape=jax.ShapeDtypeStruct((rows, d), F32),
        grid=(rows // tm, n_k),
        in_specs=[pl.BlockSpec((tm, tk), lambda i, k: (i, k)),
                  pl.BlockSpec((tk, d), lambda i, k: (k, 0)),
                  pl.BlockSpec((tm, tx), lambda i, k: (i, k)),
                  pl.BlockSpec((1, d), lambda i, k: (0, 0))],
        out_specs=pl.BlockSpec((tm, d), lambda i, k: (i, 0)),
        compiler_params=_params(),
        name="out_proj",
    )(y, w_bf, x, g_row)


def _rope_tables(base, off):
    inv = ROPE_BASE ** (-jnp.arange(0, D_HEAD, 2, dtype=F32) / D_HEAD)
    ab = base.astype(F32)[:, None] * inv[None, :]
    ao = off.astype(F32)[:, None] * inv[None, :]
    cb, sb = jnp.cos(ab)[:, None, :], jnp.sin(ab)[:, None, :]
    co, so = jnp.cos(ao)[None, :, :], jnp.sin(ao)[None, :, :]
    n = base.shape[0] * off.shape[0]
    return (cb * co - sb * so).reshape(n, HALF), (sb * co + cb * so).reshape(n, HALF)


def _retention_tables(L):
    log_gamma = jnp.log1p(-jnp.power(2.0, -RET_DECAY_BASE - jnp.arange(N_HEADS, dtype=F32)))
    j = jnp.arange(L, dtype=F32)
    diff = j[:, None] - j[None, :]
    dec = jnp.where(diff >= 0, jnp.exp(log_gamma[:, None, None] * jnp.maximum(diff, 0.0)), 0.0)
    inter = jnp.exp(log_gamma[:, None] * (j + 1.0))
    inter = jnp.broadcast_to(inter[:, :, None], (N_HEADS, L, LANES))
    wsb = jnp.exp(log_gamma[:, None] * (L - 1.0 - j))[:, None, :]
    gl = jnp.broadcast_to(jnp.exp(log_gamma * L)[:, None, None], (N_HEADS, 1, LANES))
    return dec, inter, wsb, gl


def _pad_lanes(v):
    return jnp.zeros((1, LANES), F32).at[0, :v.shape[0]].set(v.astype(F32))


def _run_group(p, kt, gates, bi_row, bf_row, tabs, ga, gb, state, n_streams, n_chunks, L, row_off):
    c0, n0, m0, s0 = state
    awe, c_row, m_out = _gate_prep(gates, bi_row, bf_row, m0, n_streams, n_chunks, L, row_off)
    y, c_out, n_out, s_out = _recurrence(p, kt, awe, c_row, tabs, ga, gb, c0, n0, s0,
                                         n_streams, n_chunks, L, row_off)
    return y, (c_out, n_out, m_out, s_out)


def kernel(x_prompt, x_sample, state_mlstm_C, state_mlstm_n, state_mlstm_m, state_ret_S,
           meta_tokens, g_norm1, w_in, b_igate, b_fgate, g_head_a, g_head_b, w_out, g_final):
    depth = w_in.shape[0]
    assert depth == 1 and x_prompt.shape[0] == 1
    d = x_prompt.shape[-1]
    seq = x_prompt.shape[1]
    n_dec, t_dec = x_sample.shape[0], x_sample.shape[1]
    n_meta = meta_tokens.shape[0]
    width = N_HEADS * D_HEAD
    assert seq % MAIN_CHUNK == 0 and MAIN_CHUNK % t_dec == 0 and t_dec % n_meta == 0

    wt = w_in[0].T
    wt_bf = wt.astype(BF16)
    wg_t = jnp.zeros((2 * LANES, d), F32)
    wg_t = wg_t.at[:N_HEADS].set(wt[9 * width:9 * width + N_HEADS])
    wg_t = wg_t.at[LANES:LANES + N_HEADS].set(wt[9 * width + N_HEADS:])
    wg_t = wg_t.astype(BF16)
    g1 = g_norm1[0][None, :]
    bi_row = _pad_lanes(b_igate[0])
    bf_row = _pad_lanes(b_fgate[0])
    ga = g_head_a[0].reshape(N_HEADS, 1, D_HEAD)
    gb = g_head_b[0].reshape(N_HEADS, 1, D_HEAD)

    x_main = x_prompt[0]
    n_samp = n_dec * t_dec
    meta_at = seq + n_samp
    meta_pad = -(meta_at + n_meta) % PROJ_ROWS
    x_samp = x_sample.reshape(n_samp, d)
    x_meta = jnp.pad(meta_tokens.astype(F32), ((0, meta_pad), (0, 0)))
    n_rows = meta_at + x_meta.shape[0]
    blk = jnp.arange(n_rows // t_dec)
    base = jnp.where(blk < seq // t_dec, n_meta + t_dec * blk,
                     jnp.where(blk < meta_at // t_dec, n_meta + PAST_LEN, 0))
    cos, sin = _rope_tables(base, jnp.arange(t_dec))

    xn, gates = _prenorm([x_main, x_samp, x_meta], g1, wg_t, NORM_ROWS)
    p, kt, w_out_bf = _proj(xn, wt_bf, cos, sin, cos.T, sin.T, w_out[0], PROJ_ROWS)

    zeros_c = jnp.zeros((1, N_HEADS, D_HEAD, D_HEAD), F32)
    zero_state = (zeros_c, jnp.zeros((1, N_HEADS, D_HEAD, LANES), F32), jnp.zeros((1, 1, LANES), F32), zeros_c)
    kt_meta = kt[:, :, meta_at:meta_at + n_meta][None]
    _, st_meta = _run_group(p, kt_meta, gates, bi_row, bf_row, _retention_tables(n_meta),
                            ga, gb, zero_state, 1, 1, n_meta, meta_at // n_meta)

    gf = g_final[None, :]
    c_meta, n_meta_st, m_meta, s_meta = st_meta
    awe_p, c_row_p, m_p = _gate_prep(gates, bi_row, bf_row, m_meta, 1, seq // MAIN_CHUNK, MAIN_CHUNK, 0)
    y_prompt, c_p, n_p, s_p = _mixer(p, kt[None], awe_p, c_row_p, _retention_tables(MAIN_CHUNK), ga, gb,
                                     c_meta, n_meta_st, s_meta, w_out_bf, x_main, gf, MAIN_CHUNK)
    st_p = (c_p, n_p, m_p, s_p)

    kt_samp = kt[:, :, seq:meta_at].reshape(N_HEADS, 2 * D_HEAD, n_dec, t_dec).transpose(2, 0, 1, 3)
    n0 = jnp.broadcast_to(state_mlstm_n[0].astype(F32)[..., None], (n_dec, N_HEADS, D_HEAD, LANES))
    m0 = jnp.zeros((n_dec, 1, LANES), F32).at[:, 0, :N_HEADS].set(state_mlstm_m[0].astype(F32))
    st0 = (state_mlstm_C[0].astype(F32), n0, m0, state_ret_S[0].astype(F32))
    y_s, st_s = _run_group(p, kt_samp, gates, bi_row, bf_row, _retention_tables(t_dec),
                           ga, gb, st0, n_dec, 1, t_dec, seq // t_dec)

    y_prompt = y_prompt[None]
    y_sample = _out_proj(y_s, w_out_bf, x_sample.reshape(n_samp, d), gf, 512, 4)
    y_sample = y_sample.reshape(n_dec, t_dec, d)

    dp = x_prompt.dtype
    ds = state_mlstm_C.dtype

    def states(st, dt):
        c_out, n_out, m_out, s_out = st
        return (c_out[None].astype(dt), n_out[..., 0][None].astype(dt),
                m_out[:, 0, :N_HEADS][None].astype(dt), s_out[None].astype(dt))

    pc, pn, pm, ps = states(st_p, dp)
    sc, sn, sm, ss = states(st_s, ds)
    return (y_prompt, y_sample, pc, pn, pm, ps, sc, sn, sm, ss)
```

```python
import functools

import jax
import jax.numpy as jnp
from jax import lax
from jax.experimental import pallas as pl
from jax.experimental.pallas import tpu as pltpu

F32 = jnp.float32
BF16 = jnp.bfloat16

EPS = 1e-6
LOG2_E = 1.4426950408889634
ROPE_BASE = 10000.0
RET_DECAY_BASE = 5.0
PAST_LEN = 2048

N_HEADS = 8
D_HEAD = 256
HALF = D_HEAD // 2
LANES = 128
MAIN_CHUNK = 256
HEADS_PER_STEP = 8
PREP_CHUNKS = 4
MIXER_SUB = 2
OUT_COL_SPLIT = 4
NORM_ROWS = 256
PROJ_ROWS = 768
PROJ_PREFETCHED = 3
WO_ROWS = 32
P_SECTIONS = 7
P_HEAD = P_SECTIONS * D_HEAD
Y_HEAD = 2 * D_HEAD

VMEM_LIMIT = 56 * 1024 * 1024


def _params(vmem=VMEM_LIMIT):
    return pltpu.CompilerParams(vmem_limit_bytes=vmem)


def _nt_dot(a, b):
    return lax.dot_general(a, b, (((1,), (1,)), ((), ())), preferred_element_type=F32)


def _prenorm_kernel(*refs, starts):
    n_in = len(starts) - 1
    x_refs = refs[:n_in]
    g_ref, wg_ref, xn_ref, gates_ref = refs[n_in:]
    i = pl.program_id(0)
    for k, x_ref in enumerate(x_refs):
        @pl.when((i >= starts[k]) & (i < starts[k + 1]))
        def _(x_ref=x_ref):
            x = x_ref[...]
            ms = jnp.mean(x * x, axis=-1, keepdims=True)
            xn = (x * lax.rsqrt(ms + EPS) * g_ref[...]).astype(BF16)
            xn_ref[...] = xn
            gates_ref[...] = _nt_dot(xn, wg_ref[...])


def _prenorm(xs, g_row, wg_t, tm):
    d = xs[0].shape[1]
    starts = [0]
    for x in xs:
        assert x.shape[0] % tm == 0
        starts.append(starts[-1] + x.shape[0] // tm)

    def x_spec(k):
        return pl.BlockSpec((tm, d), lambda i: (jnp.clip(i - starts[k], 0, starts[k + 1] - starts[k] - 1), 0))

    rows = starts[-1] * tm
    return pl.pallas_call(
        functools.partial(_prenorm_kernel, starts=tuple(starts)),
        out_shape=(jax.ShapeDtypeStruct((rows, d), BF16),
                   jax.ShapeDtypeStruct((rows, 2 * LANES), F32)),
        grid=(starts[-1],),
        in_specs=[x_spec(k) for k in range(len(xs))] + [
            pl.BlockSpec((1, d), lambda i: (0, 0)),
            pl.BlockSpec((2 * LANES, d), lambda i: (0, 0))],
        out_specs=(pl.BlockSpec((tm, d), lambda i: (i, 0)),
                   pl.BlockSpec((tm, 2 * LANES), lambda i: (i, 0))),
        compiler_params=_params(),
        name="prenorm",
    )(*xs, g_row, wg_t)


def _silu(z):
    return z * jax.nn.sigmoid(z)


def _proj_kernel(xn_ref, wqa, wva, woa, wza, wqb, wvb, wzb, wka, wkb,
                 cos_ref, sin_ref, cos_t_ref, sin_t_ref, wo_ref, p_ref, kt_ref, wo_bf_ref, *, wo_blocks):
    @pl.when(pl.program_id(0) * pl.num_programs(1) + pl.program_id(1) < wo_blocks)
    def _():
        wo_bf_ref[...] = wo_ref[...].astype(BF16)

    xn = xn_ref[...]

    def tok(w_ref):
        return _nt_dot(xn, w_ref[...])

    def put(sec, val):
        p_ref[:, sec * D_HEAD:(sec + 1) * D_HEAD] = val.astype(BF16)

    put(0, tok(wqa))
    put(1, tok(wva))
    put(2, jax.nn.sigmoid(tok(woa)))
    put(3, _silu(tok(wza)))
    q = tok(wqb)
    c = cos_ref[...]
    s = sin_ref[...]
    x1 = q[:, :HALF]
    x2 = q[:, HALF:]
    p_ref[:, 4 * D_HEAD:4 * D_HEAD + HALF] = (x1 * c - x2 * s).astype(BF16)
    p_ref[:, 4 * D_HEAD + HALF:5 * D_HEAD] = (x1 * s + x2 * c).astype(BF16)
    put(5, tok(wvb))
    put(6, _silu(tok(wzb)))

    scale = D_HEAD ** -0.5
    kt_ref[0, 0:D_HEAD, :] = (_nt_dot(wka[...], xn) * scale).astype(BF16)
    kb = _nt_dot(wkb[...], xn)
    ct = cos_t_ref[...] * scale
    st = sin_t_ref[...] * scale
    k1 = kb[:HALF]
    k2 = kb[HALF:]
    kt_ref[0, D_HEAD:D_HEAD + HALF, :] = (k1 * ct - k2 * st).astype(BF16)
    kt_ref[0, D_HEAD + HALF:2 * D_HEAD, :] = (k1 * st + k2 * ct).astype(BF16)


def _proj(xn, wt_bf, cos, sin, cos_t, sin_t, w_out, tm):
    rows, d = xn.shape
    n_m = rows // tm
    wo_blocks = w_out.shape[0] // WO_ROWS
    per_group = wo_blocks // 2
    per_head = per_group // N_HEADS
    assert wo_blocks <= N_HEADS * n_m and per_head * N_HEADS * 2 == wo_blocks

    def wo_src(h, m):
        return (jnp.minimum(h * n_m + m, wo_blocks - 1), 0)

    def wo_dst(h, m):
        g = jnp.minimum(h * n_m + m, wo_blocks - 1)
        grp, head, part = g // per_group, (g % per_group) // per_head, g % per_head
        return ((head * 2 + grp) * per_head + part, 0)

    def w_spec(wsec, buffers):
        return pl.BlockSpec((D_HEAD, d), lambda h, m: (wsec * N_HEADS + h, 0),
                            pipeline_mode=pl.Buffered(buffers))

    order = (0, 2, 3, 4, 5, 7, 8, 1, 6)
    w_specs = [w_spec(wsec, 2 if i < PROJ_PREFETCHED else 1) for i, wsec in enumerate(order)]
    tab = pl.BlockSpec((tm, HALF), lambda h, m: (m, 0))
    tab_t = pl.BlockSpec((HALF, tm), lambda h, m: (0, m))
    d_out = w_out.shape[1]
    return pl.pallas_call(
        functools.partial(_proj_kernel, wo_blocks=wo_blocks),
        out_shape=(jax.ShapeDtypeStruct((rows, N_HEADS * P_HEAD), BF16),
                   jax.ShapeDtypeStruct((N_HEADS, 2 * D_HEAD, rows), BF16),
                   jax.ShapeDtypeStruct(w_out.shape, BF16)),
        grid=(N_HEADS, n_m),
        in_specs=[pl.BlockSpec((tm, d), lambda h, m: (m, 0))] + w_specs + [
            tab, tab, tab_t, tab_t, pl.BlockSpec((WO_ROWS, d_out), wo_src)],
        out_specs=(pl.BlockSpec((tm, P_HEAD), lambda h, m: (m, h)),
                   pl.BlockSpec((1, 2 * D_HEAD, tm), lambda h, m: (h, 0, m)),
                   pl.BlockSpec((WO_ROWS, d_out), wo_dst)),
        compiler_params=_params(),
        name="proj",
    )(xn, wt_bf, wt_bf, wt_bf, wt_bf, wt_bf, wt_bf, wt_bf, wt_bf, wt_bf, cos, sin, cos_t, sin_t, w_out)


def _gate_prep_kernel(g_ref, bi_ref, bf_ref, m0_ref, awe_ref, crow_ref, mo_ref, m_scr, *, L):
    @pl.when(pl.program_id(1) == 0)
    def _():
        m_scr[...] = m0_ref[0]

    rows = g_ref.shape[0]
    g = g_ref[...]
    ig = g[:, :LANES] + bi_ref[...]
    z = g[:, LANES:] + bf_ref[...]
    lf = jnp.minimum(z, 0.0) - jnp.log1p(jnp.exp(-jnp.abs(z)))
    assert L & (L - 1) == 0
    pos = lax.broadcasted_iota(jnp.int32, (rows, LANES), 0) & (L - 1)
    b = lf
    d = 1
    while d < L:
        b = b + jnp.where(pos >= d, pltpu.roll(b, d, 0), 0.0)
        d *= 2
    c = ig - b
    cm = c
    d = 1
    while d < L:
        cm = jnp.maximum(cm, jnp.where(pos >= d, pltpu.roll(cm, d, 0), -jnp.inf))
        d *= 2
    m_prev = m_scr[...]
    for k in range(rows // L):
        sl = slice(k * L, (k + 1) * L)
        mm = jnp.maximum(m_prev, cm[sl])
        m_t = b[sl] + mm
        awe_ref[sl, 0:LANES] = -mm * LOG2_E
        awe_ref[sl, LANES:2 * LANES] = jnp.exp(m_prev - mm)
        awe_ref[sl, 2 * LANES:3 * LANES] = jnp.exp(-m_t)
        m_prev = m_t[L - 1:L, :]
    c2 = c * LOG2_E
    if rows < LANES:
        c_sq = jnp.concatenate([c2, jnp.zeros((LANES - rows, LANES), F32)], axis=0)
    else:
        c_sq = c2
    c_t = c_sq.T
    for hh in range(N_HEADS):
        crow_ref[0, hh] = c_t[hh:hh + 1, :rows]
    m_scr[...] = m_prev
    mo_ref[0] = m_prev


def _gate_prep(gates, bi_row, bf_row, m0, n_streams, n_chunks, L, row_off):
    rows_out = n_streams * n_chunks * L
    per_step = PREP_CHUNKS if n_chunks % PREP_CHUNKS == 0 else 1
    n_steps = n_chunks // per_step
    rb = per_step * L
    assert (row_off * L) % rb == 0
    blk_off = row_off * L // rb
    return pl.pallas_call(
        functools.partial(_gate_prep_kernel, L=L),
        out_shape=(jax.ShapeDtypeStruct((rows_out, 3 * LANES), F32),
                   jax.ShapeDtypeStruct((n_streams, N_HEADS, 1, n_chunks * L), F32),
                   jax.ShapeDtypeStruct((n_streams, 1, LANES), F32)),
        grid=(n_streams, n_steps),
        in_specs=[pl.BlockSpec((rb, 2 * LANES), lambda s, c: (blk_off + s * n_steps + c, 0)),
                  pl.BlockSpec((1, LANES), lambda s, c: (0, 0)),
                  pl.BlockSpec((1, LANES), lambda s, c: (0, 0)),
                  pl.BlockSpec((1, 1, LANES), lambda s, c: (s, 0, 0))],
        out_specs=(pl.BlockSpec((rb, 3 * LANES), lambda s, c: (s * n_steps + c, 0)),
                   pl.BlockSpec((1, N_HEADS, 1, rb), lambda s, c: (s, 0, 0, c)),
                   pl.BlockSpec((1, 1, LANES), lambda s, c: (s, 0, 0))),
        scratch_shapes=[pltpu.VMEM((1, LANES), F32)],
        compiler_params=_params(),
        name="gate_prep",
    )(gates, bi_row, bf_row, m0)


def _lane_tile(x, n):
    if n <= LANES:
        return x[:, :n]
    return jnp.concatenate([x] * (n // LANES), axis=1)


def _head_norm(h, g_row):
    ms = jnp.mean(h * h, axis=-1, keepdims=True)
    return h * lax.rsqrt(ms + EPS) * g_row


def _rec_kernel(p_ref, kt_ref, awe_ref, c_ref, dec_ref, inter_ref, wsb_ref, gl_ref, ga_ref, gb_ref,
                c0_ref, n0_ref, s0_ref,
                y_ref, co_ref, no_ref, so_ref,
                c_scr, n_scr, s_scr, *, L):
    ci = pl.program_id(2)

    @pl.when(ci == 0)
    def _():
        c_scr[...] = c0_ref[0]
        n_scr[...] = n0_ref[0]
        s_scr[...] = s0_ref[0]

    ones_l = jnp.ones((L, LANES), BF16)
    row = lax.broadcasted_iota(jnp.int32, (L, L), 0)
    col = lax.broadcasted_iota(jnp.int32, (L, L), 1)
    causal = col <= row
    lane = lax.broadcasted_iota(jnp.int32, (L, LANES), 1)
    last = ci == pl.num_programs(2) - 1

    hb = range(HEADS_PER_STEP)

    def sec(j, i):
        lo = j * P_HEAD + i * D_HEAD
        return p_ref[:, lo:lo + D_HEAD]

    def pick(x, j):
        mine = lane == pl.program_id(1) * HEADS_PER_STEP + j
        return jnp.sum(jnp.where(mine, x, 0.0), axis=1, keepdims=True)

    def mm(x, y):
        return jnp.dot(x, y, preferred_element_type=F32)

    qa = [sec(j, 0) for j in hb]
    va = [sec(j, 1) for j in hb]
    qb = [sec(j, 4) for j in hb]
    vb = [sec(j, 5) for j in hb]
    kat = [kt_ref[0, j, 0:D_HEAD, :] for j in hb]
    kbt = [kt_ref[0, j, D_HEAD:2 * D_HEAD, :] for j in hb]
    cmat = [c_scr[j] for j in hb]
    nrep = [n_scr[j] for j in hb]
    smat = [s_scr[j] for j in hb]

    s_raw = [mm(qa[j], kat[j]) for j in hb]
    sb_raw = [mm(qb[j], kbt[j]) for j in hb]
    qc = [mm(qa[j], cmat[j].astype(BF16)) for j in hb]
    qn = [mm(qa[j], nrep[j].astype(BF16)) for j in hb]
    qs = [mm(qb[j], smat[j].astype(BF16)) for j in hb]

    a = [pick(awe_ref[:, 0:LANES], j) for j in hb]
    wi = [pick(awe_ref[:, LANES:2 * LANES], j) for j in hb]
    e = [pick(awe_ref[:, 2 * LANES:3 * LANES], j) for j in hb]
    c = [c_ref[0, j] for j in hb]
    s_bf = [(s_raw[j] * jnp.exp2(jnp.where(causal, a[j] + c[j], -jnp.inf))).astype(BF16) for j in hb]
    sb_bf = [(sb_raw[j] * dec_ref[j]).astype(BF16) for j in hb]

    w_state = [jnp.exp2(a[j][L - 1:L, :] + c[j]) for j in hb]
    kw = [kat[j] * w_state[j].astype(BF16) for j in hb]
    kwb = [kbt[j] * wsb_ref[j].astype(BF16) for j in hb]

    sv = [mm(s_bf[j], va[j]) for j in hb]
    rowsum = [mm(s_bf[j], ones_l) for j in hb]
    ob = [mm(sb_bf[j], vb[j]) for j in hb]
    c_upd = [mm(kw[j], va[j]) for j in hb]
    n_upd = [mm(kw[j], ones_l) for j in hb]
    s_upd = [mm(kwb[j], vb[j]) for j in hb]

    for j in hb:
        den = wi[j] * qn[j] + rowsum[j]
        r = 1.0 / jnp.maximum(jnp.abs(den), e[j])
        h = (wi[j] * qc[j] + sv[j]) * _lane_tile(r, D_HEAD)
        ya = _head_norm(sec(j, 2).astype(F32) * h, ga_ref[j]) * sec(j, 3).astype(F32)
        y_ref[:, j * Y_HEAD:j * Y_HEAD + D_HEAD] = ya.astype(BF16)
        o = ob[j] + _lane_tile(inter_ref[j], D_HEAD) * qs[j]
        yb = _head_norm(o, gb_ref[j]) * sec(j, 6).astype(F32)
        y_ref[:, j * Y_HEAD + D_HEAD:(j + 1) * Y_HEAD] = yb.astype(BF16)

    for j in hb:
        decay = wi[j][L - 1:L, :]
        c_new = decay * cmat[j] + c_upd[j]
        n_new = decay * nrep[j] + n_upd[j]
        s_new = _lane_tile(gl_ref[j], D_HEAD) * smat[j] + s_upd[j]
        c_scr[j] = c_new
        n_scr[j] = n_new
        s_scr[j] = s_new

        @pl.when(last)
        def _(j=j, c_new=c_new, n_new=n_new, s_new=s_new):
            co_ref[0, j] = c_new
            no_ref[0, j] = n_new
            so_ref[0, j] = s_new


def _recurrence(p, kt, awe, c_row, tabs, ga, gb, c0, n0, s0, n_streams, n_chunks, L, row_off):
    dec, inter, wsb, gl = tabs
    hb = HEADS_PER_STEP
    y_rows = n_streams * n_chunks * L

    def per_head(*tail):
        return pl.BlockSpec((hb,) + tail, lambda s, h, c: (h,) + (0,) * len(tail))

    state_spec = pl.BlockSpec((1, hb, D_HEAD, D_HEAD), lambda s, h, c: (s, h, 0, 0))
    nstate_spec = pl.BlockSpec((1, hb, D_HEAD, LANES), lambda s, h, c: (s, h, 0, 0))
    in_specs = [pl.BlockSpec((L, hb * P_HEAD), lambda s, h, c: (row_off + s * n_chunks + c, h)),
                pl.BlockSpec((1, hb, 2 * D_HEAD, L), lambda s, h, c: (s, h, 0, c)),
                pl.BlockSpec((L, 3 * LANES), lambda s, h, c: (s * n_chunks + c, 0)),
                pl.BlockSpec((1, hb, 1, L), lambda s, h, c: (s, h, 0, c)),
                per_head(L, L), per_head(L, LANES), per_head(1, L), per_head(1, LANES),
                per_head(1, D_HEAD), per_head(1, D_HEAD),
                state_spec, nstate_spec, state_spec]
    st = jax.ShapeDtypeStruct((n_streams, N_HEADS, D_HEAD, D_HEAD), F32)
    return pl.pallas_call(
        functools.partial(_rec_kernel, L=L),
        out_shape=(jax.ShapeDtypeStruct((y_rows, N_HEADS * Y_HEAD), BF16),
                   st, jax.ShapeDtypeStruct((n_streams, N_HEADS, D_HEAD, LANES), F32), st),
        grid=(n_streams, N_HEADS // hb, n_chunks),
        in_specs=in_specs,
        out_specs=(pl.BlockSpec((L, hb * Y_HEAD), lambda s, h, c: (s * n_chunks + c, h)),
                   state_spec, nstate_spec, state_spec),
        scratch_shapes=[pltpu.VMEM((hb, D_HEAD, D_HEAD), F32),
                        pltpu.VMEM((hb, D_HEAD, LANES), F32),
                        pltpu.VMEM((hb, D_HEAD, D_HEAD), F32)],
        compiler_params=_params(),
        name="recurrence",
    )(p, kt, awe, c_row, dec, inter, wsb, gl, ga, gb, c0, n0, s0)


def _mixer_kernel(p_ref, kt_ref, awe_ref, c_ref, dec_ref, inter_ref, wsb_ref, gl_ref, ga_ref, gb_ref,
                  c0_ref, n0_ref, s0_ref, w_ref, x_ref, gf_ref,
                  o_ref, co_ref, no_ref, so_ref,
                  c_scr, n_scr, s_scr, y_scr, *, L):
    tile = pl.program_id(0)
    step = pl.program_id(1)
    head = jnp.minimum(step, N_HEADS - 1)
    d = o_ref.shape[1]

    def mm(x, y):
        return jnp.dot(x, y, preferred_element_type=F32)

    def load_initial_state():
        @pl.when(tile == 0)
        def _():
            c_scr[head] = c0_ref[0, 0]
            n_scr[head] = n0_ref[0, 0]
            s_scr[head] = s0_ref[0, 0]

    def store_final_state(cmat, nrep, smat):
        @pl.when(tile == pl.num_programs(0) - 1)
        def _():
            co_ref[0, 0] = cmat
            no_ref[0, 0] = nrep
            so_ref[0, 0] = smat

    def recurrence_step():
        ones_l = jnp.ones((L, LANES), BF16)
        row = lax.broadcasted_iota(jnp.int32, (L, L), 0)
        col = lax.broadcasted_iota(jnp.int32, (L, L), 1)
        causal = col <= row
        mine = lax.broadcasted_iota(jnp.int32, (L, LANES), 1) == head

        def pick(x):
            return jnp.sum(jnp.where(mine, x, 0.0), axis=1, keepdims=True)

        subs = range(MIXER_SUB)
        rows = [slice(u * L, (u + 1) * L) for u in subs]

        def sec(u, i):
            return p_ref[rows[u], i * D_HEAD:(i + 1) * D_HEAD]

        dec = dec_ref[head]
        inter = _lane_tile(inter_ref[head], D_HEAD)
        wsb = wsb_ref[head].astype(BF16)
        gl = _lane_tile(gl_ref[head], D_HEAD)
        c_all = c_ref[0, head]
        y_half = y_scr.at[step % 2]

        qa = [sec(u, 0) for u in subs]
        va = [sec(u, 1) for u in subs]
        qb = [sec(u, 4) for u in subs]
        vb = [sec(u, 5) for u in subs]
        kat = [kt_ref[0, 0, 0:D_HEAD, rows[u]] for u in subs]
        kbt = [kt_ref[0, 0, D_HEAD:2 * D_HEAD, rows[u]] for u in subs]
        a = [pick(awe_ref[rows[u], 0:LANES]) for u in subs]
        wi = [pick(awe_ref[rows[u], LANES:2 * LANES]) for u in subs]
        e = [pick(awe_ref[rows[u], 2 * LANES:3 * LANES]) for u in subs]
        c = [c_all[:, rows[u]] for u in subs]
        s_bf = [(mm(qa[u], kat[u]) * jnp.exp2(jnp.where(causal, a[u] + c[u], -jnp.inf))).astype(BF16)
                for u in subs]
        sb_bf = [(mm(qb[u], kbt[u]) * dec).astype(BF16) for u in subs]
        kw = [kat[u] * jnp.exp2(a[u][L - 1:L, :] + c[u]).astype(BF16) for u in subs]
        kwb = [kbt[u] * wsb for u in subs]
        sv = [mm(s_bf[u], va[u]) for u in subs]
        rowsum = [mm(s_bf[u], ones_l) for u in subs]
        ob = [mm(sb_bf[u], vb[u]) for u in subs]
        c_upd = [mm(kw[u], va[u]) for u in subs]
        n_upd = [mm(kw[u], ones_l) for u in subs]
        s_upd = [mm(kwb[u], vb[u]) for u in subs]

        cmat = c_scr[head]
        nrep = n_scr[head]
        smat = s_scr[head]
        for u in subs:
            qc = mm(qa[u], cmat.astype(BF16))
            qn = mm(qa[u], nrep.astype(BF16))
            qs = mm(qb[u], smat.astype(BF16))
            den = wi[u] * qn + rowsum[u]
            r = 1.0 / jnp.maximum(jnp.abs(den), e[u])
            h = (wi[u] * qc + sv[u]) * _lane_tile(r, D_HEAD)
            ya = _head_norm(sec(u, 2).astype(F32) * h, ga_ref[head]) * sec(u, 3).astype(F32)
            y_half[rows[u], 0:D_HEAD] = ya.astype(BF16)
            yb = _head_norm(ob[u] + inter * qs, gb_ref[head]) * sec(u, 6).astype(F32)
            y_half[rows[u], D_HEAD:Y_HEAD] = yb.astype(BF16)
            decay = wi[u][L - 1:L, :]
            cmat = decay * cmat + c_upd[u]
            nrep = decay * nrep + n_upd[u]
            smat = gl * smat + s_upd[u]
        c_scr[head] = cmat
        n_scr[head] = nrep
        s_scr[head] = smat
        return cmat, nrep, smat

    def accumulate(first):
        y = y_scr[(step + 1) % 2]
        tq = d // OUT_COL_SPLIT
        for q in range(OUT_COL_SPLIT):
            cols = slice(q * tq, (q + 1) * tq)
            part = mm(y, w_ref[:, cols])
            o_ref[:, cols] = part if first else o_ref[:, cols] + part

    def add_residual():
        tx = d // N_HEADS
        for hh in range(N_HEADS):
            @pl.when(step == hh + 1)
            def _(hh=hh):
                o_ref[:, hh * tx:(hh + 1) * tx] += x_ref[...]

    @pl.when(step == 0)
    def _():
        load_initial_state()
        store_final_state(*recurrence_step())

    for first in (True, False):
        @pl.when((step == 1) if first else ((step > 1) & (step < N_HEADS)))
        def _(first=first):
            load_initial_state()
            accumulate(first)
            state = recurrence_step()
            store_final_state(*state)
            add_residual()

    @pl.when(step == N_HEADS)
    def _():
        tx = d // N_HEADS
        lo = (N_HEADS - 1) * tx
        halves = [slice(u * L, (u + 1) * L) for u in range(MIXER_SUB)]
        y = y_scr[(step + 1) % 2]
        zs = [o_ref[r, :] + mm(y[r], w_ref[...]) for r in halves]
        for r, z in zip(halves, zs):
            z = jnp.concatenate([z[:, :lo], z[:, lo:] + x_ref[r, :]], axis=1)
            ms = jnp.mean(z * z, axis=-1, keepdims=True)
            o_ref[r, :] = z * lax.rsqrt(ms + EPS) * gf_ref[...]


def _mixer(p, kt, awe, c_row, tabs, ga, gb, c0, n0, s0, w_bf, x, gf_row, L):
    dec, inter, wsb, gl = tabs
    seq, d = x.shape
    tile_rows = MIXER_SUB * L
    n_tiles = seq // tile_rows

    def resident(arr):
        nd = arr.ndim
        return pl.BlockSpec(arr.shape, lambda t, s: (0,) * nd, pipeline_mode=pl.Buffered(1))

    def rec_head(s):
        return jnp.minimum(s, N_HEADS - 1)

    def acc_head(s):
        return jnp.maximum(s - 1, 0)

    def first_tile(*tail):
        return pl.BlockSpec((1, 1) + tail,
                            lambda t, s: (0, jnp.where(t == 0, rec_head(s), N_HEADS - 1), 0, 0))

    def last_tile(*tail):
        return pl.BlockSpec((1, 1) + tail,
                            lambda t, s: (0, jnp.where(t == n_tiles - 1, rec_head(s), 0), 0, 0))

    in_specs = [pl.BlockSpec((tile_rows, P_HEAD), lambda t, s: (t, rec_head(s))),
                pl.BlockSpec((1, 1, 2 * D_HEAD, tile_rows), lambda t, s: (0, rec_head(s), 0, t)),
                pl.BlockSpec((tile_rows, 3 * LANES), lambda t, s: (t, 0)),
                pl.BlockSpec((1, N_HEADS, 1, tile_rows), lambda t, s: (0, 0, 0, t)),
                resident(dec), resident(inter), resident(wsb), resident(gl), resident(ga), resident(gb),
                first_tile(D_HEAD, D_HEAD), first_tile(D_HEAD, LANES), first_tile(D_HEAD, D_HEAD),
                pl.BlockSpec((Y_HEAD, d), lambda t, s: (acc_head(s), 0)),
                pl.BlockSpec((tile_rows, d // N_HEADS), lambda t, s: (t, acc_head(s))),
                resident(gf_row)]
    st = jax.ShapeDtypeStruct((1, N_HEADS, D_HEAD, D_HEAD), F32)
    return pl.pallas_call(
        functools.partial(_mixer_kernel, L=L),
        out_shape=(jax.ShapeDtypeStruct((seq, d), F32),
                   st, jax.ShapeDtypeStruct((1, N_HEADS, D_HEAD, LANES), F32), st),
        grid=(n_tiles, N_HEADS + 1),
        in_specs=in_specs,
        out_specs=(pl.BlockSpec((tile_rows, d), lambda t, s: (t, 0)),
                   last_tile(D_HEAD, D_HEAD), last_tile(D_HEAD, LANES), last_tile(D_HEAD, D_HEAD)),
        scratch_shapes=[pltpu.VMEM((N_HEADS, D_HEAD, D_HEAD), F32),
                        pltpu.VMEM((N_HEADS, D_HEAD, LANES), F32),
                        pltpu.VMEM((N_HEADS, D_HEAD, D_HEAD), F32),
                        pltpu.VMEM((2, tile_rows, Y_HEAD), BF16)],
        compiler_params=_params(),
        name="mixer",
    )(p, kt, awe, c_row, dec, inter, wsb, gl, ga, gb, c0, n0, s0, w_bf, x, gf_row)


def _out_kernel(y_ref, w_ref, x_ref, g_ref, o_ref, *, n_k, tx):
    k = pl.program_id(1)

    @pl.when(k == 0)
    def _():
        o_ref[...] = jnp.dot(y_ref[...], w_ref[...], preferred_element_type=F32)

    @pl.when(k > 0)
    def _():
        o_ref[...] += jnp.dot(y_ref[...], w_ref[...], preferred_element_type=F32)

    for kk in range(n_k):
        @pl.when(k == kk)
        def _(kk=kk):
            o_ref[:, kk * tx:(kk + 1) * tx] += x_ref[...]

    @pl.when(k == n_k - 1)
    def _():
        z = o_ref[...]
        ms = jnp.mean(z * z, axis=-1, keepdims=True)
        o_ref[...] = z * lax.rsqrt(ms + EPS) * g_ref[...]


def _out_proj(y, w_bf, x, g_row, tm, n_k):
    rows, d = x.shape
    tk = y.shape[1] // n_k
    tx = d // n_k
    return pl.pallas_call(
        functools.partial(_out_kernel, n_k=n_k, tx=tx),
        out_shape=jax.ShapeDtypeStruct((rows, d), F32),
        grid=(rows // tm, n_k),
        in_specs=[pl.BlockSpec((tm, tk), lambda i, k: (i, k)),
                  pl.BlockSpec((tk, d), lambda i, k: (k, 0)),
                  pl.BlockSpec((tm, tx), lambda i, k: (i, k)),
                  pl.BlockSpec((1, d), lambda i, k: (0, 0))],
        out_specs=pl.BlockSpec((tm, d), lambda i, k: (i, 0)),
        compiler_params=_params(),
        name="out_proj",
    )(y, w_bf, x, g_row)


def _rope_tables(base, off):
    inv = ROPE_BASE ** (-jnp.arange(0, D_HEAD, 2, dtype=F32) / D_HEAD)
    ab = base.astype(F32)[:, None] * inv[None, :]
    ao = off.astype(F32)[:, None] * inv[None, :]
    cb, sb = jnp.cos(ab)[:, None, :], jnp.sin(ab)[:, None, :]
    co, so = jnp.cos(ao)[None, :, :], jnp.sin(ao)[None, :, :]
    n = base.shape[0] * off.shape[0]
    return (cb * co - sb * so).reshape(n, HALF), (sb * co + cb * so).reshape(n, HALF)


def _retention_tables(L):
    log_gamma = jnp.log1p(-jnp.power(2.0, -RET_DECAY_BASE - jnp.arange(N_HEADS, dtype=F32)))
    j = jnp.arange(L, dtype=F32)
    diff = j[:, None] - j[None, :]
    dec = jnp.where(diff >= 0, jnp.exp(log_gamma[:, None, None] * jnp.maximum(diff, 0.0)), 0.0)
    inter = jnp.exp(log_gamma[:, None] * (j + 1.0))
    inter = jnp.broadcast_to(inter[:, :, None], (N_HEADS, L, LANES))
    wsb = jnp.exp(log_gamma[:, None] * (L - 1.0 - j))[:, None, :]
    gl = jnp.broadcast_to(jnp.exp(log_gamma * L)[:, None, None], (N_HEADS, 1, LANES))
    return dec, inter, wsb, gl


def _pad_lanes(v):
    return jnp.zeros((1, LANES), F32).at[0, :v.shape[0]].set(v.astype(F32))


def _run_group(p, kt, gates, bi_row, bf_row, tabs, ga, gb, state, n_streams, n_chunks, L, row_off):
    c0, n0, m0, s0 = state
    awe, c_row, m_out = _gate_prep(gates, bi_row, bf_row, m0, n_streams, n_chunks, L, row_off)
    y, c_out, n_out, s_out = _recurrence(p, kt, awe, c_row, tabs, ga, gb, c0, n0, s0,
                                         n_streams, n_chunks, L, row_off)
    return y, (c_out, n_out, m_out, s_out)


def kernel(x_prompt, x_sample, state_mlstm_C, state_mlstm_n, state_mlstm_m, state_ret_S,
           meta_tokens, g_norm1, w_in, b_igate, b_fgate, g_head_a, g_head_b, w_out, g_final):
    depth = w_in.shape[0]
    assert depth == 1 and x_prompt.shape[0] == 1
    d = x_prompt.shape[-1]
    seq = x_prompt.shape[1]
    n_dec, t_dec = x_sample.shape[0], x_sample.shape[1]
    n_meta = meta_tokens.shape[0]
    width = N_HEADS * D_HEAD
    assert seq % MAIN_CHUNK == 0 and MAIN_CHUNK % t_dec == 0 and t_dec % n_meta == 0

    wt = w_in[0].T
    wt_bf = wt.astype(BF16)
    wg_t = jnp.zeros((2 * LANES, d), F32)
    wg_t = wg_t.at[:N_HEADS].set(wt[9 * width:9 * width + N_HEADS])
    wg_t = wg_t.at[LANES:LANES + N_HEADS].set(wt[9 * width + N_HEADS:])
    wg_t = wg_t.astype(BF16)
    g1 = g_norm1[0][None, :]
    bi_row = _pad_lanes(b_igate[0])
    bf_row = _pad_lanes(b_fgate[0])
    ga = g_head_a[0].reshape(N_HEADS, 1, D_HEAD)
    gb = g_head_b[0].reshape(N_HEADS, 1, D_HEAD)

    x_main = x_prompt[0]
    n_samp = n_dec * t_dec
    meta_at = seq + n_samp
    meta_pad = -(meta_at + n_meta) % PROJ_ROWS
    x_samp = x_sample.reshape(n_samp, d)
    x_meta = jnp.pad(meta_tokens.astype(F32), ((0, meta_pad), (0, 0)))
    n_rows = meta_at + x_meta.shape[0]
    blk = jnp.arange(n_rows // t_dec)
    base = jnp.where(blk < seq // t_dec, n_meta + t_dec * blk,
                     jnp.where(blk < meta_at // t_dec, n_meta + PAST_LEN, 0))
    cos, sin = _rope_tables(base, jnp.arange(t_dec))

    xn, gates = _prenorm([x_main, x_samp, x_meta], g1, wg_t, NORM_ROWS)
    p, kt, w_out_bf = _proj(xn, wt_bf, cos, sin, cos.T, sin.T, w_out[0], PROJ_ROWS)

    zeros_c = jnp.zeros((1, N_HEADS, D_HEAD, D_HEAD), F32)
    zero_state = (zeros_c, jnp.zeros((1, N_HEADS, D_HEAD, LANES), F32), jnp.zeros((1, 1, LANES), F32), zeros_c)
    kt_meta = kt[:, :, meta_at:meta_at + n_meta][None]
    _, st_meta = _run_group(p, kt_meta, gates, bi_row, bf_row, _retention_tables(n_meta),
                            ga, gb, zero_state, 1, 1, n_meta, meta_at // n_meta)

    gf = g_final[None, :]
    c_meta, n_meta_st, m_meta, s_meta = st_meta
    awe_p, c_row_p, m_p = _gate_prep(gates, bi_row, bf_row, m_meta, 1, seq // MAIN_CHUNK, MAIN_CHUNK, 0)
    y_prompt, c_p, n_p, s_p = _mixer(p, kt[None], awe_p, c_row_p, _retention_tables(MAIN_CHUNK), ga, gb,
                                     c_meta, n_meta_st, s_meta, w_out_bf, x_main, gf, MAIN_CHUNK)
    st_p = (c_p, n_p, m_p, s_p)

    kt_samp = kt[:, :, seq:meta_at].reshape(N_HEADS, 2 * D_HEAD, n_dec, t_dec).transpose(2, 0, 1, 3)
    n0 = jnp.broadcast_to(state_mlstm_n[0].astype(F32)[..., None], (n_dec, N_HEADS, D_HEAD, LANES))
    m0 = jnp.zeros((n_dec, 1, LANES), F32).at[:, 0, :N_HEADS].set(state_mlstm_m[0].astype(F32))
    st0 = (state_mlstm_C[0].astype(F32), n0, m0, state_ret_S[0].astype(F32))
    y_s, st_s = _run_group(p, kt_samp, gates, bi_row, bf_row, _retention_tables(t_dec),
                           ga, gb, st0, n_dec, 1, t_dec, seq // t_dec)

    y_prompt = y_prompt[None]
    y_sample = _out_proj(y_s, w_out_bf, x_sample.reshape(n_samp, d), gf, 512, 4)
    y_sample = y_sample.reshape(n_dec, t_dec, d)

    dp = x_prompt.dtype
    ds = state_mlstm_C.dtype

    def states(st, dt):
        c_out, n_out, m_out, s_out = st
        return (c_out[None].astype(dt), n_out[..., 0][None].astype(dt),
                m_out[:, 0, :N_HEADS][None].astype(dt), s_out[None].astype(dt))

    pc, pn, pm, ps = states(st_p, dp)
    sc, sn, sm, ss = states(st_s, ds)
    return (y_prompt, y_sample, pc, pn, pm, ps, sc, sn, sm, ss)
```

```python
import functools

import jax
import jax.numpy as jnp
from jax import lax
from jax.experimental import pallas as pl
from jax.experimental.pallas import tpu as pltpu

F32 = jnp.float32
BF16 = jnp.bfloat16

EPS = 1e-6
LOG2_E = 1.4426950408889634
ROPE_BASE = 10000.0
RET_DECAY_BASE = 5.0
PAST_LEN = 2048

N_HEADS = 8
D_HEAD = 256
HALF = D_HEAD // 2
LANES = 128
MAIN_CHUNK = 256
HEADS_PER_STEP = 8
PREP_CHUNKS = 4
MIXER_SUB = 2
OUT_COL_SPLIT = 4
NORM_ROWS = 256
PROJ_ROWS = 768
PROJ_PREFETCHED = 3
WO_ROWS = 32
P_SECTIONS = 7
P_HEAD = P_SECTIONS * D_HEAD
Y_HEAD = 2 * D_HEAD

VMEM_LIMIT = 56 * 1024 * 1024


def _params(vmem=VMEM_LIMIT):
    return pltpu.CompilerParams(vmem_limit_bytes=vmem)


def _nt_dot(a, b):
    return lax.dot_general(a, b, (((1,), (1,)), ((), ())), preferred_element_type=F32)


def _prenorm_kernel(*refs, starts):
    n_in = len(starts) - 1
    x_refs = refs[:n_in]
    g_ref, wg_ref, xn_ref, gates_ref = refs[n_in:]
    i = pl.program_id(0)
    for k, x_ref in enumerate(x_refs):
        @pl.when((i >= starts[k]) & (i < starts[k + 1]))
        def _(x_ref=x_ref):
            x = x_ref[...]
            ms = jnp.mean(x * x, axis=-1, keepdims=True)
            xn = (x * lax.rsqrt(ms + EPS) * g_ref[...]).astype(BF16)
            xn_ref[...] = xn
            gates_ref[...] = _nt_dot(xn, wg_ref[...])


def _prenorm(xs, g_row, wg_t, tm):
    d = xs[0].shape[1]
    starts = [0]
    for x in xs:
        assert x.shape[0] % tm == 0
        starts.append(starts[-1] + x.shape[0] // tm)

    def x_spec(k):
        return pl.BlockSpec((tm, d), lambda i: (jnp.clip(i - starts[k], 0, starts[k + 1] - starts[k] - 1), 0))

    rows = starts[-1] * tm
    return pl.pallas_call(
        functools.partial(_prenorm_kernel, starts=tuple(starts)),
        out_shape=(jax.ShapeDtypeStruct((rows, d), BF16),
                   jax.ShapeDtypeStruct((rows, 2 * LANES), F32)),
        grid=(starts[-1],),
        in_specs=[x_spec(k) for k in range(len(xs))] + [
            pl.BlockSpec((1, d), lambda i: (0, 0)),
            pl.BlockSpec((2 * LANES, d), lambda i: (0, 0))],
        out_specs=(pl.BlockSpec((tm, d), lambda i: (i, 0)),
                   pl.BlockSpec((tm, 2 * LANES), lambda i: (i, 0))),
        compiler_params=_params(),
        name="prenorm",
    )(*xs, g_row, wg_t)


def _silu(z):
    return z * jax.nn.sigmoid(z)


def _proj_kernel(xn_ref, wqa, wva, woa, wza, wqb, wvb, wzb, wka, wkb,
                 cos_ref, sin_ref, cos_t_ref, sin_t_ref, wo_ref, p_ref, kt_ref, wo_bf_ref, *, wo_blocks):
    @pl.when(pl.program_id(0) * pl.num_programs(1) + pl.program_id(1) < wo_blocks)
    def _():
        wo_bf_ref[...] = wo_ref[...].astype(BF16)

    xn = xn_ref[...]

    def tok(w_ref):
        return _nt_dot(xn, w_ref[...])

    def put(sec, val):
        p_ref[:, sec * D_HEAD:(sec + 1) * D_HEAD] = val.astype(BF16)

    put(0, tok(wqa))
    put(1, tok(wva))
    put(2, jax.nn.sigmoid(tok(woa)))
    put(3, _silu(tok(wza)))
    q = tok(wqb)
    c = cos_ref[...]
    s = sin_ref[...]
    x1 = q[:, :HALF]
    x2 = q[:, HALF:]
    p_ref[:, 4 * D_HEAD:4 * D_HEAD + HALF] = (x1 * c - x2 * s).astype(BF16)
    p_ref[:, 4 * D_HEAD + HALF:5 * D_HEAD] = (x1 * s + x2 * c).astype(BF16)
    put(5, tok(wvb))
    put(6, _silu(tok(wzb)))

    scale = D_HEAD ** -0.5
    kt_ref[0, 0:D_HEAD, :] = (_nt_dot(wka[...], xn) * scale).astype(BF16)
    kb = _nt_dot(wkb[...], xn)
    ct = cos_t_ref[...] * scale
    st = sin_t_ref[...] * scale
    k1 = kb[:HALF]
    k2 = kb[HALF:]
    kt_ref[0, D_HEAD:D_HEAD + HALF, :] = (k1 * ct - k2 * st).astype(BF16)
    kt_ref[0, D_HEAD + HALF:2 * D_HEAD, :] = (k1 * st + k2 * ct).astype(BF16)


def _proj(xn, wt_bf, cos, sin, cos_t, sin_t, w_out, tm):
    rows, d = xn.shape
    n_m = rows // tm
    wo_blocks = w_out.shape[0] // WO_ROWS
    per_group = wo_blocks // 2
    per_head = per_group // N_HEADS
    assert wo_blocks <= N_HEADS * n_m and per_head * N_HEADS * 2 == wo_blocks

    def wo_src(h, m):
        return (jnp.minimum(h * n_m + m, wo_blocks - 1), 0)

    def wo_dst(h, m):
        g = jnp.minimum(h * n_m + m, wo_blocks - 1)
        grp, head, part = g // per_group, (g % per_group) // per_head, g % per_head
        return ((head * 2 + grp) * per_head + part, 0)

    def w_spec(wsec, buffers):
        return pl.BlockSpec((D_HEAD, d), lambda h, m: (wsec * N_HEADS + h, 0),
                            pipeline_mode=pl.Buffered(buffers))

    order = (0, 2, 3, 4, 5, 7, 8, 1, 6)
    w_specs = [w_spec(wsec, 2 if i < PROJ_PREFETCHED else 1) for i, wsec in enumerate(order)]
    tab = pl.BlockSpec((tm, HALF), lambda h, m: (m, 0))
    tab_t = pl.BlockSpec((HALF, tm), lambda h, m: (0, m))
    d_out = w_out.shape[1]
    return pl.pallas_call(
        functools.partial(_proj_kernel, wo_blocks=wo_blocks),
        out_shape=(jax.ShapeDtypeStruct((rows, N_HEADS * P_HEAD), BF16),
                   jax.ShapeDtypeStruct((N_HEADS, 2 * D_HEAD, rows), BF16),
                   jax.ShapeDtypeStruct(w_out.shape, BF16)),
        grid=(N_HEADS, n_m),
        in_specs=[pl.BlockSpec((tm, d), lambda h, m: (m, 0))] + w_specs + [
            tab, tab, tab_t, tab_t, pl.BlockSpec((WO_ROWS, d_out), wo_src)],
        out_specs=(pl.BlockSpec((tm, P_HEAD), lambda h, m: (m, h)),
                   pl.BlockSpec((1, 2 * D_HEAD, tm), lambda h, m: (h, 0, m)),
                   pl.BlockSpec((WO_ROWS, d_out), wo_dst)),
        compiler_params=_params(),
        name="proj",
    )(xn, wt_bf, wt_bf, wt_bf, wt_bf, wt_bf, wt_bf, wt_bf, wt_bf, wt_bf, cos, sin, cos_t, sin_t, w_out)


def _gate_prep_kernel(g_ref, bi_ref, bf_ref, m0_ref, awe_ref, crow_ref, mo_ref, m_scr, *, L):
    @pl.when(pl.program_id(1) == 0)
    def _():
        m_scr[...] = m0_ref[0]

    rows = g_ref.shape[0]
    g = g_ref[...]
    ig = g[:, :LANES] + bi_ref[...]
    z = g[:, LANES:] + bf_ref[...]
    lf = jnp.minimum(z, 0.0) - jnp.log1p(jnp.exp(-jnp.abs(z)))
    assert L & (L - 1) == 0
    pos = lax.broadcasted_iota(jnp.int32, (rows, LANES), 0) & (L - 1)
    b = lf
    d = 1
    while d < L:
        b = b + jnp.where(pos >= d, pltpu.roll(b, d, 0), 0.0)
        d *= 2
    c = ig - b
    cm = c
    d = 1
    while d < L:
        cm = jnp.maximum(cm, jnp.where(pos >= d, pltpu.roll(cm, d, 0), -jnp.inf))
        d *= 2
    m_prev = m_scr[...]
    for k in range(rows // L):
        sl = slice(k * L, (k + 1) * L)
        mm = jnp.maximum(m_prev, cm[sl])
        m_t = b[sl] + mm
        awe_ref[sl, 0:LANES] = -mm * LOG2_E
        awe_ref[sl, LANES:2 * LANES] = jnp.exp(m_prev - mm)
        awe_ref[sl, 2 * LANES:3 * LANES] = jnp.exp(-m_t)
        m_prev = m_t[L - 1:L, :]
    c2 = c * LOG2_E
    if rows < LANES:
        c_sq = jnp.concatenate([c2, jnp.zeros((LANES - rows, LANES), F32)], axis=0)
    else:
        c_sq = c2
    c_t = c_sq.T
    for hh in range(N_HEADS):
        crow_ref[0, hh] = c_t[hh:hh + 1, :rows]
    m_scr[...] = m_prev
    mo_ref[0] = m_prev


def _gate_prep(gates, bi_row, bf_row, m0, n_streams, n_chunks, L, row_off):
    rows_out = n_streams * n_chunks * L
    per_step = PREP_CHUNKS if n_chunks % PREP_CHUNKS == 0 else 1
    n_steps = n_chunks // per_step
    rb = per_step * L
    assert (row_off * L) % rb == 0
    blk_off = row_off * L // rb
    return pl.pallas_call(
        functools.partial(_gate_prep_kernel, L=L),
        out_shape=(jax.ShapeDtypeStruct((rows_out, 3 * LANES), F32),
                   jax.ShapeDtypeStruct((n_streams, N_HEADS, 1, n_chunks * L), F32),
                   jax.ShapeDtypeStruct((n_streams, 1, LANES), F32)),
        grid=(n_streams, n_steps),
        in_specs=[pl.BlockSpec((rb, 2 * LANES), lambda s, c: (blk_off + s * n_steps + c, 0)),
                  pl.BlockSpec((1, LANES), lambda s, c: (0, 0)),
                  pl.BlockSpec((1, LANES), lambda s, c: (0, 0)),
                  pl.BlockSpec((1, 1, LANES), lambda s, c: (s, 0, 0))],
        out_specs=(pl.BlockSpec((rb, 3 * LANES), lambda s, c: (s * n_steps + c, 0)),
                   pl.BlockSpec((1, N_HEADS, 1, rb), lambda s, c: (s, 0, 0, c)),
                   pl.BlockSpec((1, 1, LANES), lambda s, c: (s, 0, 0))),
        scratch_shapes=[pltpu.VMEM((1, LANES), F32)],
        compiler_params=_params(),
        name="gate_prep",
    )(gates, bi_row, bf_row, m0)


def _lane_tile(x, n):
    if n <= LANES:
        return x[:, :n]
    return jnp.concatenate([x] * (n // LANES), axis=1)


def _head_norm(h, g_row):
    ms = jnp.mean(h * h, axis=-1, keepdims=True)
    return h * lax.rsqrt(ms + EPS) * g_row


def _rec_kernel(p_ref, kt_ref, awe_ref, c_ref, dec_ref, inter_ref, wsb_ref, gl_ref, ga_ref, gb_ref,
                c0_ref, n0_ref, s0_ref,
                y_ref, co_ref, no_ref, so_ref,
                c_scr, n_scr, s_scr, *, L):
    ci = pl.program_id(2)

    @pl.when(ci == 0)
    def _():
        c_scr[...] = c0_ref[0]
        n_scr[...] = n0_ref[0]
        s_scr[...] = s0_ref[0]

    ones_l = jnp.ones((L, LANES), BF16)
    row = lax.broadcasted_iota(jnp.int32, (L, L), 0)
    col = lax.broadcasted_iota(jnp.int32, (L, L), 1)
    causal = col <= row
    lane = lax.broadcasted_iota(jnp.int32, (L, LANES), 1)
    last = ci == pl.num_programs(2) - 1

    hb = range(HEADS_PER_STEP)

    def sec(j, i):
        lo = j * P_HEAD + i * D_HEAD
        return p_ref[:, lo:lo + D_HEAD]

    def pick(x, j):
        mine = lane == pl.program_id(1) * HEADS_PER_STEP + j
        return jnp.sum(jnp.where(mine, x, 0.0), axis=1, keepdims=True)

    def mm(x, y):
        return jnp.dot(x, y, preferred_element_type=F32)

    qa = [sec(j, 0) for j in hb]
    va = [sec(j, 1) for j in hb]
    qb = [sec(j, 4) for j in hb]
    vb = [sec(j, 5) for j in hb]
    kat = [kt_ref[0, j, 0:D_HEAD, :] for j in hb]
    kbt = [kt_ref[0, j, D_HEAD:2 * D_HEAD, :] for j in hb]
    cmat = [c_scr[j] for j in hb]
    nrep = [n_scr[j] for j in hb]
    smat = [s_scr[j] for j in hb]

    s_raw = [mm(qa[j], kat[j]) for j in hb]
    sb_raw = [mm(qb[j], kbt[j]) for j in hb]
    qc = [mm(qa[j], cmat[j].astype(BF16)) for j in hb]
    qn = [mm(qa[j], nrep[j].astype(BF16)) for j in hb]
    qs = [mm(qb[j], smat[j].astype(BF16)) for j in hb]

    a = [pick(awe_ref[:, 0:LANES], j) for j in hb]
    wi = [pick(awe_ref[:, LANES:2 * LANES], j) for j in hb]
    e = [pick(awe_ref[:, 2 * LANES:3 * LANES], j) for j in hb]
    c = [c_ref[0, j] for j in hb]
    s_bf = [(s_raw[j] * jnp.exp2(jnp.where(causal, a[j] + c[j], -jnp.inf))).astype(BF16) for j in hb]
    sb_bf = [(sb_raw[j] * dec_ref[j]).astype(BF16) for j in hb]

    w_state = [jnp.exp2(a[j][L - 1:L, :] + c[j]) for j in hb]
    kw = [kat[j] * w_state[j].astype(BF16) for j in hb]
    kwb = [kbt[j] * wsb_ref[j].astype(BF16) for j in hb]

    sv = [mm(s_bf[j], va[j]) for j in hb]
    rowsum = [mm(s_bf[j], ones_l) for j in hb]
    ob = [mm(sb_bf[j], vb[j]) for j in hb]
    c_upd = [mm(kw[j], va[j]) for j in hb]
    n_upd = [mm(kw[j], ones_l) for j in hb]
    s_upd = [mm(kwb[j], vb[j]) for j in hb]

    for j in hb:
        den = wi[j] * qn[j] + rowsum[j]
        r = 1.0 / jnp.maximum(jnp.abs(den), e[j])
        h = (wi[j] * qc[j] + sv[j]) * _lane_tile(r, D_HEAD)
        ya = _head_norm(sec(j, 2).astype(F32) * h, ga_ref[j]) * sec(j, 3).astype(F32)
        y_ref[:, j * Y_HEAD:j * Y_HEAD + D_HEAD] = ya.astype(BF16)
        o = ob[j] + _lane_tile(inter_ref[j], D_HEAD) * qs[j]
        yb = _head_norm(o, gb_ref[j]) * sec(j, 6).astype(F32)
        y_ref[:, j * Y_HEAD + D_HEAD:(j + 1) * Y_HEAD] = yb.astype(BF16)

    for j in hb:
        decay = wi[j][L - 1:L, :]
        c_new = decay * cmat[j] + c_upd[j]
        n_new = decay * nrep[j] + n_upd[j]
        s_new = _lane_tile(gl_ref[j], D_HEAD) * smat[j] + s_upd[j]
        c_scr[j] = c_new
        n_scr[j] = n_new
        s_scr[j] = s_new

        @pl.when(last)
        def _(j=j, c_new=c_new, n_new=n_new, s_new=s_new):
            co_ref[0, j] = c_new
            no_ref[0, j] = n_new
            so_ref[0, j] = s_new


def _recurrence(p, kt, awe, c_row, tabs, ga, gb, c0, n0, s0, n_streams, n_chunks, L, row_off):
    dec, inter, wsb, gl = tabs
    hb = HEADS_PER_STEP
    y_rows = n_streams * n_chunks * L

    def per_head(*tail):
        return pl.BlockSpec((hb,) + tail, lambda s, h, c: (h,) + (0,) * len(tail))

    state_spec = pl.BlockSpec((1, hb, D_HEAD, D_HEAD), lambda s, h, c: (s, h, 0, 0))
    nstate_spec = pl.BlockSpec((1, hb, D_HEAD, LANES), lambda s, h, c: (s, h, 0, 0))
    in_specs = [pl.BlockSpec((L, hb * P_HEAD), lambda s, h, c: (row_off + s * n_chunks + c, h)),
                pl.BlockSpec((1, hb, 2 * D_HEAD, L), lambda s, h, c: (s, h, 0, c)),
                pl.BlockSpec((L, 3 * LANES), lambda s, h, c: (s * n_chunks + c, 0)),
                pl.BlockSpec((1, hb, 1, L), lambda s, h, c: (s, h, 0, c)),
                per_head(L, L), per_head(L, LANES), per_head(1, L), per_head(1, LANES),
                per_head(1, D_HEAD), per_head(1, D_HEAD),
                state_spec, nstate_spec, state_spec]
    st = jax.ShapeDtypeStruct((n_streams, N_HEADS, D_HEAD, D_HEAD), F32)
    return pl.pallas_call(
        functools.partial(_rec_kernel, L=L),
        out_shape=(jax.ShapeDtypeStruct((y_rows, N_HEADS * Y_HEAD), BF16),
                   st, jax.ShapeDtypeStruct((n_streams, N_HEADS, D_HEAD, LANES), F32), st),
        grid=(n_streams, N_HEADS // hb, n_chunks),
        in_specs=in_specs,
        out_specs=(pl.BlockSpec((L, hb * Y_HEAD), lambda s, h, c: (s * n_chunks + c, h)),
                   state_spec, nstate_spec, state_spec),
        scratch_shapes=[pltpu.VMEM((hb, D_HEAD, D_HEAD), F32),
                        pltpu.VMEM((hb, D_HEAD, LANES), F32),
                        pltpu.VMEM((hb, D_HEAD, D_HEAD), F32)],
        compiler_params=_params(),
        name="recurrence",
    )(p, kt, awe, c_row, dec, inter, wsb, gl, ga, gb, c0, n0, s0)


def _mixer_kernel(p_ref, kt_ref, awe_ref, c_ref, dec_ref, inter_ref, wsb_ref, gl_ref, ga_ref, gb_ref,
                  c0_ref, n0_ref, s0_ref, w_ref, x_ref, gf_ref,
                  o_full, co_ref, no_ref, so_ref,
                  c_scr, n_scr, s_scr, y_scr, *, L):
    half = pl.program_id(2)
    step = pl.program_id(1)
    tile_rows = MIXER_SUB * L
    o_ref = o_full.at[pl.ds(pl.multiple_of(half * tile_rows, tile_rows), tile_rows), :]
    is_first = (pl.program_id(0) == 0) & (half == 0)
    is_last = (pl.program_id(0) == pl.num_programs(0) - 1) & (half == 1)
    head = jnp.minimum(step, N_HEADS - 1)
    d = o_ref.shape[1]

    def mm(x, y):
        return jnp.dot(x, y, preferred_element_type=F32)

    def load_initial_state():
        @pl.when(is_first)
        def _():
            c_scr[head] = c0_ref[0, 0]
            n_scr[head] = n0_ref[0, 0]
            s_scr[head] = s0_ref[0, 0]

    def store_final_state(cmat, nrep, smat):
        @pl.when(is_last)
        def _():
            co_ref[0, 0] = cmat
            no_ref[0, 0] = nrep
            so_ref[0, 0] = smat

    def recurrence_step():
        ones_l = jnp.ones((L, LANES), BF16)
        row = lax.broadcasted_iota(jnp.int32, (L, L), 0)
        col = lax.broadcasted_iota(jnp.int32, (L, L), 1)
        causal = col <= row
        mine = lax.broadcasted_iota(jnp.int32, (L, LANES), 1) == head

        def pick(x):
            return jnp.sum(jnp.where(mine, x, 0.0), axis=1, keepdims=True)

        subs = range(MIXER_SUB)
        rows = [slice(u * L, (u + 1) * L) for u in subs]

        def sec(u, i):
            return p_ref[rows[u], i * D_HEAD:(i + 1) * D_HEAD]

        dec = dec_ref[head]
        inter = _lane_tile(inter_ref[head], D_HEAD)
        wsb = wsb_ref[head].astype(BF16)
        gl = _lane_tile(gl_ref[head], D_HEAD)
        c_all = c_ref[0, head]
        y_half = y_scr.at[step % 2, half]

        qa = [sec(u, 0) for u in subs]
        va = [sec(u, 1) for u in subs]
        qb = [sec(u, 4) for u in subs]
        vb = [sec(u, 5) for u in subs]
        kat = [kt_ref[0, 0, 0:D_HEAD, rows[u]] for u in subs]
        kbt = [kt_ref[0, 0, D_HEAD:2 * D_HEAD, rows[u]] for u in subs]
        a = [pick(awe_ref[rows[u], 0:LANES]) for u in subs]
        wi = [pick(awe_ref[rows[u], LANES:2 * LANES]) for u in subs]
        e = [pick(awe_ref[rows[u], 2 * LANES:3 * LANES]) for u in subs]
        c = [c_all[:, rows[u]] for u in subs]
        s_bf = [(mm(qa[u], kat[u]) * jnp.exp2(jnp.where(causal, a[u] + c[u], -jnp.inf))).astype(BF16)
                for u in subs]
        sb_bf = [(mm(qb[u], kbt[u]) * dec).astype(BF16) for u in subs]
        kw = [kat[u] * jnp.exp2(a[u][L - 1:L, :] + c[u]).astype(BF16) for u in subs]
        kwb = [kbt[u] * wsb for u in subs]
        sv = [mm(s_bf[u], va[u]) for u in subs]
        rowsum = [mm(s_bf[u], ones_l) for u in subs]
        ob = [mm(sb_bf[u], vb[u]) for u in subs]
        c_upd = [mm(kw[u], va[u]) for u in subs]
        n_upd = [mm(kw[u], ones_l) for u in subs]
        s_upd = [mm(kwb[u], vb[u]) for u in subs]

        cmat = c_scr[head]
        nrep = n_scr[head]
        smat = s_scr[head]
        for u in subs:
            qc = mm(qa[u], cmat.astype(BF16))
            qn = mm(qa[u], nrep.astype(BF16))
            qs = mm(qb[u], smat.astype(BF16))
            den = wi[u] * qn + rowsum[u]
            r = 1.0 / jnp.maximum(jnp.abs(den), e[u])
            h = (wi[u] * qc + sv[u]) * _lane_tile(r, D_HEAD)
            ya = _head_norm(sec(u, 2).astype(F32) * h, ga_ref[head]) * sec(u, 3).astype(F32)
            y_half[rows[u], 0:D_HEAD] = ya.astype(BF16)
            yb = _head_norm(ob[u] + inter * qs, gb_ref[head]) * sec(u, 6).astype(F32)
            y_half[rows[u], D_HEAD:Y_HEAD] = yb.astype(BF16)
            decay = wi[u][L - 1:L, :]
            cmat = decay * cmat + c_upd[u]
            nrep = decay * nrep + n_upd[u]
            smat = gl * smat + s_upd[u]
        c_scr[head] = cmat
        n_scr[head] = nrep
        s_scr[head] = smat
        return cmat, nrep, smat

    def accumulate():
        y = y_scr[(step + 1) % 2, half]
        tq = d // OUT_COL_SPLIT
        for q in range(OUT_COL_SPLIT):
            o_ref[:, q * tq:(q + 1) * tq] += mm(y, w_ref[:, q * tq:(q + 1) * tq])

    def add_residual():
        tx = d // N_HEADS
        for hh in range(N_HEADS):
            @pl.when(step == hh + 1)
            def _(hh=hh):
                o_ref[:, hh * tx:(hh + 1) * tx] += x_ref[...]

    @pl.when(step == 0)
    def _():
        load_initial_state()
        o_ref[...] = jnp.zeros(o_ref.shape, F32)
        store_final_state(*recurrence_step())

    @pl.when((step > 0) & (step < N_HEADS))
    def _():
        load_initial_state()
        accumulate()
        state = recurrence_step()
        store_final_state(*state)
        add_residual()

    @pl.when(step == N_HEADS)
    def _():
        tx = d // N_HEADS
        lo = (N_HEADS - 1) * tx
        halves = [slice(u * L, (u + 1) * L) for u in range(MIXER_SUB)]
        y = y_scr[(step + 1) % 2, half]
        zs = [o_ref[r, :] + mm(y[r], w_ref[...]) for r in halves]
        for r, z in zip(halves, zs):
            z = jnp.concatenate([z[:, :lo], z[:, lo:] + x_ref[r, :]], axis=1)
            ms = jnp.mean(z * z, axis=-1, keepdims=True)
            o_ref[r, :] = z * lax.rsqrt(ms + EPS) * gf_ref[...]


def _mixer(p, kt, awe, c_row, tabs, ga, gb, c0, n0, s0, w_bf, x, gf_row, L):
    dec, inter, wsb, gl = tabs
    seq, d = x.shape
    tile_rows = MIXER_SUB * L
    n_pairs = seq // (2 * tile_rows)

    def resident(arr):
        nd = arr.ndim
        return pl.BlockSpec(arr.shape, lambda t, s, f: (0,) * nd, pipeline_mode=pl.Buffered(1))

    def rec_head(s):
        return jnp.minimum(s, N_HEADS - 1)

    def acc_head(s):
        return jnp.maximum(s - 1, 0)

    def first_tile(*tail):
        return pl.BlockSpec((1, 1) + tail,
                            lambda t, s, f: (0, jnp.where(t == 0, rec_head(s), N_HEADS - 1), 0, 0))

    def last_tile(*tail):
        return pl.BlockSpec((1, 1) + tail,
                            lambda t, s, f: (0, jnp.where(t == n_pairs - 1, rec_head(s), 0), 0, 0))

    in_specs = [pl.BlockSpec((tile_rows, P_HEAD), lambda t, s, f: (2 * t + f, rec_head(s))),
                pl.BlockSpec((1, 1, 2 * D_HEAD, tile_rows), lambda t, s, f: (0, rec_head(s), 0, 2 * t + f)),
                pl.BlockSpec((tile_rows, 3 * LANES), lambda t, s, f: (2 * t + f, 0)),
                pl.BlockSpec((1, N_HEADS, 1, tile_rows), lambda t, s, f: (0, 0, 0, 2 * t + f)),
                resident(dec), resident(inter), resident(wsb), resident(gl), resident(ga), resident(gb),
                first_tile(D_HEAD, D_HEAD), first_tile(D_HEAD, LANES), first_tile(D_HEAD, D_HEAD),
                pl.BlockSpec((Y_HEAD, d), lambda t, s, f: (acc_head(s), 0)),
                pl.BlockSpec((tile_rows, d // N_HEADS), lambda t, s, f: (2 * t + f, acc_head(s))),
                resident(gf_row)]
    st = jax.ShapeDtypeStruct((1, N_HEADS, D_HEAD, D_HEAD), F32)
    return pl.pallas_call(
        functools.partial(_mixer_kernel, L=L),
        out_shape=(jax.ShapeDtypeStruct((seq, d), F32),
                   st, jax.ShapeDtypeStruct((1, N_HEADS, D_HEAD, LANES), F32), st),
        grid=(n_pairs, N_HEADS + 1, 2),
        in_specs=in_specs,
        out_specs=(pl.BlockSpec((2 * tile_rows, d), lambda t, s, f: (t, 0), pipeline_mode=pl.Buffered(1)),
                   last_tile(D_HEAD, D_HEAD), last_tile(D_HEAD, LANES), last_tile(D_HEAD, D_HEAD)),
        scratch_shapes=[pltpu.VMEM((N_HEADS, D_HEAD, D_HEAD), F32),
                        pltpu.VMEM((N_HEADS, D_HEAD, LANES), F32),
                        pltpu.VMEM((N_HEADS, D_HEAD, D_HEAD), F32),
                        pltpu.VMEM((2, 2, tile_rows, Y_HEAD), BF16)],
        compiler_params=_params(),
        name="mixer",
    )(p, kt, awe, c_row, dec, inter, wsb, gl, ga, gb, c0, n0, s0, w_bf, x, gf_row)


def _out_kernel(y_ref, w_ref, x_ref, g_ref, o_ref, *, n_k, tx):
    k = pl.program_id(1)

    @pl.when(k == 0)
    def _():
        o_ref[...] = jnp.dot(y_ref[...], w_ref[...], preferred_element_type=F32)

    @pl.when(k > 0)
    def _():
        o_ref[...] += jnp.dot(y_ref[...], w_ref[...], preferred_element_type=F32)

    for kk in range(n_k):
        @pl.when(k == kk)
        def _(kk=kk):
            o_ref[:, kk * tx:(kk + 1) * tx] += x_ref[...]

    @pl.when(k == n_k - 1)
    def _():
        z = o_ref[...]
        ms = jnp.mean(z * z, axis=-1, keepdims=True)
        o_ref[...] = z * lax.rsqrt(ms + EPS) * g_ref[...]


def _out_proj(y, w_bf, x, g_row, tm, n_k):
    rows, d = x.shape
    tk = y.shape[1] // n_k
    tx = d // n_k
    return pl.pallas_call(
        functools.partial(_out_kernel, n_k=n_k, tx=tx),
        out_shape=jax.ShapeDtypeStruct((rows, d), F32),
        grid=(rows // tm, n_k),
        in_specs=[pl.BlockSpec((tm, tk), lambda i, k: (i, k)),
                  pl.BlockSpec((tk, d), lambda i, k: (k, 0)),
                  pl.BlockSpec((tm, tx), lambda i, k: (i, k)),
                  pl.BlockSpec((1, d), lambda i, k: (0, 0))],
        out_specs=pl.BlockSpec((tm, d), lambda i, k: (i, 0)),
        compiler_params=_params(),
        name="out_proj",
    )(y, w_bf, x, g_row)


def _rope_tables(base, off):
    inv = ROPE_BASE ** (-jnp.arange(0, D_HEAD, 2, dtype=F32) / D_HEAD)
    ab = base.astype(F32)[:, None] * inv[None, :]
    ao = off.astype(F32)[:, None] * inv[None, :]
    cb, sb = jnp.cos(ab)[:, None, :], jnp.sin(ab)[:, None, :]
    co, so = jnp.cos(ao)[None, :, :], jnp.sin(ao)[None, :, :]
    n = base.shape[0] * off.shape[0]
    return (cb * co - sb * so).reshape(n, HALF), (sb * co + cb * so).reshape(n, HALF)


def _retention_tables(L):
    log_gamma = jnp.log1p(-jnp.power(2.0, -RET_DECAY_BASE - jnp.arange(N_HEADS, dtype=F32)))
    j = jnp.arange(L, dtype=F32)
    diff = j[:, None] - j[None, :]
    dec = jnp.where(diff >= 0, jnp.exp(log_gamma[:, None, None] * jnp.maximum(diff, 0.0)), 0.0)
    inter = jnp.exp(log_gamma[:, None] * (j + 1.0))
    inter = jnp.broadcast_to(inter[:, :, None], (N_HEADS, L, LANES))
    wsb = jnp.exp(log_gamma[:, None] * (L - 1.0 - j))[:, None, :]
    gl = jnp.broadcast_to(jnp.exp(log_gamma * L)[:, None, None], (N_HEADS, 1, LANES))
    return dec, inter, wsb, gl


def _pad_lanes(v):
    return jnp.zeros((1, LANES), F32).at[0, :v.shape[0]].set(v.astype(F32))


def _run_group(p, kt, gates, bi_row, bf_row, tabs, ga, gb, state, n_streams, n_chunks, L, row_off):
    c0, n0, m0, s0 = state
    awe, c_row, m_out = _gate_prep(gates, bi_row, bf_row, m0, n_streams, n_chunks, L, row_off)
    y, c_out, n_out, s_out = _recurrence(p, kt, awe, c_row, tabs, ga, gb, c0, n0, s0,
                                         n_streams, n_chunks, L, row_off)
    return y, (c_out, n_out, m_out, s_out)


def kernel(x_prompt, x_sample, state_mlstm_C, state_mlstm_n, state_mlstm_m, state_ret_S,
           meta_tokens, g_norm1, w_in, b_igate, b_fgate, g_head_a, g_head_b, w_out, g_final):
    depth = w_in.shape[0]
    assert depth == 1 and x_prompt.shape[0] == 1
    d = x_prompt.shape[-1]
    seq = x_prompt.shape[1]
    n_dec, t_dec = x_sample.shape[0], x_sample.shape[1]
    n_meta = meta_tokens.shape[0]
    width = N_HEADS * D_HEAD
    assert seq % MAIN_CHUNK == 0 and MAIN_CHUNK % t_dec == 0 and t_dec % n_meta == 0

    wt = w_in[0].T
    wt_bf = wt.astype(BF16)
    wg_t = jnp.zeros((2 * LANES, d), F32)
    wg_t = wg_t.at[:N_HEADS].set(wt[9 * width:9 * width + N_HEADS])
    wg_t = wg_t.at[LANES:LANES + N_HEADS].set(wt[9 * width + N_HEADS:])
    wg_t = wg_t.astype(BF16)
    g1 = g_norm1[0][None, :]
    bi_row = _pad_lanes(b_igate[0])
    bf_row = _pad_lanes(b_fgate[0])
    ga = g_head_a[0].reshape(N_HEADS, 1, D_HEAD)
    gb = g_head_b[0].reshape(N_HEADS, 1, D_HEAD)

    x_main = x_prompt[0]
    n_samp = n_dec * t_dec
    meta_at = seq + n_samp
    meta_pad = -(meta_at + n_meta) % PROJ_ROWS
    x_samp = x_sample.reshape(n_samp, d)
    x_meta = jnp.pad(meta_tokens.astype(F32), ((0, meta_pad), (0, 0)))
    n_rows = meta_at + x_meta.shape[0]
    blk = jnp.arange(n_rows // t_dec)
    base = jnp.where(blk < seq // t_dec, n_meta + t_dec * blk,
                     jnp.where(blk < meta_at // t_dec, n_meta + PAST_LEN, 0))
    cos, sin = _rope_tables(base, jnp.arange(t_dec))

    xn, gates = _prenorm([x_main, x_samp, x_meta], g1, wg_t, NORM_ROWS)
    p, kt, w_out_bf = _proj(xn, wt_bf, cos, sin, cos.T, sin.T, w_out[0], PROJ_ROWS)

    zeros_c = jnp.zeros((1, N_HEADS, D_HEAD, D_HEAD), F32)
    zero_state = (zeros_c, jnp.zeros((1, N_HEADS, D_HEAD, LANES), F32), jnp.zeros((1, 1, LANES), F32), zeros_c)
    kt_meta = kt[:, :, meta_at:meta_at + n_meta][None]
    _, st_meta = _run_group(p, kt_meta, gates, bi_row, bf_row, _retention_tables(n_meta),
                            ga, gb, zero_state, 1, 1, n_meta, meta_at // n_meta)

    gf = g_final[None, :]
    c_meta, n_meta_st, m_meta, s_meta = st_meta
    awe_p, c_row_p, m_p = _gate_prep(gates, bi_row, bf_row, m_meta, 1, seq // MAIN_CHUNK, MAIN_CHUNK, 0)
    y_prompt, c_p, n_p, s_p = _mixer(p, kt[None], awe_p, c_row_p, _retention_tables(MAIN_CHUNK), ga, gb,
                                     c_meta, n_meta_st, s_meta, w_out_bf, x_main, gf, MAIN_CHUNK)
    st_p = (c_p, n_p, m_p, s_p)

    kt_samp = kt[:, :, seq:meta_at].reshape(N_HEADS, 2 * D_HEAD, n_dec, t_dec).transpose(2, 0, 1, 3)
    n0 = jnp.broadcast_to(state_mlstm_n[0].astype(F32)[..., None], (n_dec, N_HEADS, D_HEAD, LANES))
    m0 = jnp.zeros((n_dec, 1, LANES), F32).at[:, 0, :N_HEADS].set(state_mlstm_m[0].astype(F32))
    st0 = (state_mlstm_C[0].astype(F32), n0, m0, state_ret_S[0].astype(F32))
    y_s, st_s = _run_group(p, kt_samp, gates, bi_row, bf_row, _retention_tables(t_dec),
                           ga, gb, st0, n_dec, 1, t_dec, seq // t_dec)

    y_prompt = y_prompt[None]
    y_sample = _out_proj(y_s, w_out_bf, x_sample.reshape(n_samp, d), gf, 512, 4)
    y_sample = y_sample.reshape(n_dec, t_dec, d)

    dp = x_prompt.dtype
    ds = state_mlstm_C.dtype

    def states(st, dt):
        c_out, n_out, m_out, s_out = st
        return (c_out[None].astype(dt), n_out[..., 0][None].astype(dt),
                m_out[:, 0, :N_HEADS][None].astype(dt), s_out[None].astype(dt))

    pc, pn, pm, ps = states(st_p, dp)
    sc, sn, sm, ss = states(st_s, ds)
    return (y_prompt, y_sample, pc, pn, pm, ps, sc, sn, sm, ss)
```
